```python
import math
import jax
import jax.numpy as jnp
from jax import lax
import numpy as np


D_MODEL = 1024
BATCH = 1
SEQ = 16384
DEPTH = 4

GRID_W = 64
CTX_LEN = 256
BLOCK = 128
WINDOW = 128
ROPE_THETA = 10000.0
A_HQ = 8
A_HKV = 2
A_HD = 64
B_H = 4
B_HD = 64
B_VD = 2 * B_HD
C_HQ = 8
C_HKV = 2
C_HD = 128
N_EXPERTS = 16
N_GROUPS = 4
EXPERTS_PER_GROUP = N_EXPERTS // N_GROUPS
TOP_K = 2
D_EXPERT = 512
N_EVEN = (DEPTH + 1) // 2
N_ODD = DEPTH // 2
ALPHA = (2 * DEPTH) ** 0.25
BETA = (8 * DEPTH) ** -0.25
LN_EPS = 1e-5
QK_EPS = 1e-6
SUBLN_EPS = 1e-5
EVEN_SPLITS = (A_HQ * A_HD, A_HKV * A_HD, A_HKV * A_HD, B_H * 2 * B_HD, B_H * 2 * B_HD, B_H * B_VD)
EVEN_IN = sum(EVEN_SPLITS)
EVEN_OUT = A_HQ * A_HD + B_H * B_VD
ODD_SPLITS = (C_HQ * C_HD, C_HKV * C_HD, C_HKV * C_HD)
ODD_IN = sum(ODD_SPLITS)
ODD_OUT = C_HQ * C_HD

kernel_name = 'hybrid_diffusion_window_diff_axial_moe'


def _split(z, sizes):
    idx, acc = [], 0
    for s in sizes[:-1]:
        acc += s
        idx.append(acc)
    return jnp.split(z, idx, axis=-1)


def layer_norm(x, g, b):
    xf = x.astype(jnp.float32)
    mu = jnp.mean(xf, -1, keepdims=True)
    var = jnp.mean(jnp.square(xf - mu), -1, keepdims=True)
    y = (xf - mu) * lax.rsqrt(var + LN_EPS)
    return (y * g.astype(jnp.float32) + b.astype(jnp.float32)).astype(x.dtype)


def rms_norm(x, g, eps):
    xf = x.astype(jnp.float32)
    y = xf * lax.rsqrt(jnp.mean(jnp.square(xf), -1, keepdims=True) + eps)
    return (y * g.astype(jnp.float32)).astype(x.dtype)


def axial_rope_tables(n_tokens, head_dim):
    rows = n_tokens // GRID_W
    rr, cc = jnp.meshgrid(jnp.arange(rows), jnp.arange(GRID_W), indexing='ij')
    quarter = head_dim // 4
    inv_freq = ROPE_THETA ** (-jnp.arange(quarter, dtype=jnp.float32) / quarter)
    ang_r = rr.reshape(-1)[:, None].astype(jnp.float32) * inv_freq
    ang_c = cc.reshape(-1)[:, None].astype(jnp.float32) * inv_freq
    return jnp.cos(ang_r), jnp.sin(ang_r), jnp.cos(ang_c), jnp.sin(ang_c)


def _rotate(x, cos, sin):
    x1, x2 = jnp.split(x, 2, axis=-1)
    return jnp.concatenate([x1 * cos - x2 * sin, x1 * sin + x2 * cos], axis=-1)


def apply_axial_rope(x, tabs):
    cr, sr, cc, sc = [t[:, None, :].astype(x.dtype) for t in tabs]
    xr, xc = jnp.split(x, 2, axis=-1)
    return jnp.concatenate([_rotate(xr, cr, sr), _rotate(xc, cc, sc)], axis=-1)


def gqa_softmax(q, k, v, sink=None):
    hkv, grp = q.shape[2], q.shape[3]
    s = jnp.einsum('bqhgd,bkhd->bhgqk', q, k).astype(jnp.float32) * (q.shape[-1] ** -0.5)
    if sink is not None:
        s_sink = jnp.broadcast_to(sink.astype(jnp.float32).reshape(1, hkv, grp, 1, 1), s.shape[:-1] + (1,))
        p = jax.nn.softmax(jnp.concatenate([s_sink, s], axis=-1), axis=-1)[..., 1:]
    else:
        p = jax.nn.softmax(s, axis=-1)
    return jnp.einsum('bhgqk,bkhd->bqhgd', p.astype(v.dtype), v)


def window_attention_sink(q, k, v, kc, vc, sink):
    bsz, seq = q.shape[0], q.shape[1]
    nb = seq // BLOCK
    grp = A_HQ // A_HKV
    n_ctx = kc.shape[1]
    scale = A_HD ** -0.5
    qb = q.reshape(bsz, nb, BLOCK, A_HKV, grp, A_HD)

    def band(t):
        tp = jnp.pad(t, ((0, 0), (BLOCK, BLOCK), (0, 0), (0, 0))).reshape(bsz, nb + 2, BLOCK, A_HKV, A_HD)
        return jnp.concatenate([tp[:, :-2], tp[:, 1:-1], tp[:, 2:]], axis=2)

    kb, vb = band(k), band(v)
    blk = jnp.arange(nb)[:, None, None] * BLOCK
    qpos = blk + jnp.arange(BLOCK)[None, :, None]
    kpos = blk - BLOCK + jnp.arange(3 * BLOCK)[None, None, :]
    ok = (jnp.abs(qpos - kpos) <= WINDOW) & (kpos >= 0) & (kpos < seq)
    s_loc = jnp.einsum('bnqhgd,bnkhd->bnhgqk', qb, kb).astype(jnp.float32) * scale
    s_loc = jnp.where(ok[None, :, None, None], s_loc, -jnp.inf)
    s_ctx = jnp.einsum('bnqhgd,bchd->bnhgqc', qb, kc).astype(jnp.float32) * scale
    s_sink = jnp.broadcast_to(sink.astype(jnp.float32).reshape(1, 1, A_HKV, grp, 1, 1), s_ctx.shape[:-1] + (1,))
    p = jax.nn.softmax(jnp.concatenate([s_sink, s_ctx, s_loc], axis=-1), axis=-1)
    p_ctx = p[..., 1:1 + n_ctx].astype(v.dtype)
    p_loc = p[..., 1 + n_ctx:].astype(v.dtype)
    o = jnp.einsum('bnhgqc,bchd->bnqhgd', p_ctx, vc) + jnp.einsum('bnhgqk,bnkhd->bnqhgd', p_loc, vb)
    return o.reshape(bsz, seq, A_HQ * A_HD)


def diff_core(q, k, v, lam):
    s = jnp.einsum('bqhmd,bkhmd->bhmqk', q, k).astype(jnp.float32) * (B_HD ** -0.5)
    p = jax.nn.softmax(s, axis=-1)
    a = p[:, :, 0] - lam * p[:, :, 1]
    return jnp.einsum('bhqk,bkhe->bqhe', a.astype(v.dtype), v)


def diff_post(o, subln_g, lam_init):
    o = rms_norm(o, subln_g, SUBLN_EPS) * (1.0 - lam_init)
    return o.reshape(o.shape[0], o.shape[1], B_H * B_VD)


def even_mixer(h, hc, w_in, w_out, sink, lq1, lk1, lq2, lk2, subln_g, lam_init, ctx_out):
    bsz, seq = h.shape[0], h.shape[1]
    n_ctx = hc.shape[1]
    nb = seq // BLOCK
    grp = A_HQ // A_HKV
    tabs = axial_rope_tables(seq, A_HD)

    def proj(t):
        aq, ak, av, bq, bk, bv = _split(t @ w_in, EVEN_SPLITS)
        lead = t.shape[:2]
        return (aq.reshape(lead + (A_HQ, A_HD)), ak.reshape(lead + (A_HKV, A_HD)),
                av.reshape(lead + (A_HKV, A_HD)), bq.reshape(lead + (B_H * 2, B_HD)),
                bk.reshape(lead + (B_H * 2, B_HD)), bv.reshape(lead + (B_H, B_VD)))

    aq, ak, av, bq, bk, bv = proj(h)
    aq, ak, bq, bk = [apply_axial_rope(t, tabs) for t in (aq, ak, bq, bk)]
    aqc, akc, avc, bqc, bkc, bvc = proj(hc)
    lam = (jnp.exp(jnp.sum(lq1 * lk1)) - jnp.exp(jnp.sum(lq2 * lk2))).astype(jnp.float32) + lam_init

    oa = window_attention_sink(aq, ak, av, akc, avc, sink)
    kb_all = jnp.concatenate([bkc, bk], axis=1).reshape(bsz, n_ctx + seq, B_H, 2, B_HD)
    vb_all = jnp.concatenate([bvc, bv], axis=1)
    qblocks = bq.reshape(bsz, nb, BLOCK, B_H, 2, B_HD).swapaxes(0, 1)
    ob = lax.map(lambda qblk: diff_core(qblk, kb_all, vb_all, lam), qblocks)
    ob = diff_post(ob.swapaxes(0, 1).reshape(bsz, seq, B_H, B_VD), subln_g, lam_init)
    y = jnp.concatenate([oa, ob], axis=-1) @ w_out
    if not ctx_out:
        return y, None
    oac = gqa_softmax(aqc.reshape(bsz, n_ctx, A_HKV, grp, A_HD), akc, avc, sink).reshape(bsz, n_ctx, A_HQ * A_HD)
    obc = diff_post(diff_core(bqc.reshape(bsz, n_ctx, B_H, 2, B_HD), bkc.reshape(bsz, n_ctx, B_H, 2, B_HD), bvc, lam), subln_g, lam_init)
    yc = jnp.concatenate([oac, obc], axis=-1) @ w_out
    return y, yc


def odd_mixer(h, hc, w_in, w_out, qn_g, kn_g, ctx_out):
    bsz, seq = h.shape[0], h.shape[1]
    n_ctx = hc.shape[1]
    nb = seq // BLOCK
    grp = C_HQ // C_HKV
    tabs = axial_rope_tables(seq, C_HD)

    def proj(t):
        q, k, v = _split(t @ w_in, ODD_SPLITS)
        lead = t.shape[:2]
        q = rms_norm(q.reshape(lead + (C_HQ, C_HD)), qn_g, QK_EPS)
        k = rms_norm(k.reshape(lead + (C_HKV, C_HD)), kn_g, QK_EPS)
        return q, k, v.reshape(lead + (C_HKV, C_HD))

    q, k, v = proj(h)
    q, k = apply_axial_rope(q, tabs), apply_axial_rope(k, tabs)
    qc, kc, vc = proj(hc)
    k_all = jnp.concatenate([kc, k], axis=1)
    v_all = jnp.concatenate([vc, v], axis=1)
    qblocks = q.reshape(bsz, nb, BLOCK, C_HKV, grp, C_HD).swapaxes(0, 1)
    o = lax.map(lambda qblk: gqa_softmax(qblk, k_all, v_all), qblocks)
    y = o.swapaxes(0, 1).reshape(bsz, seq, C_HQ * C_HD) @ w_out
    if not ctx_out:
        return y, None
    oc = gqa_softmax(qc.reshape(bsz, n_ctx, C_HKV, grp, C_HD), kc, vc).reshape(bsz, n_ctx, C_HQ * C_HD)
    return y, oc @ w_out


def moe(h, w_router, b_router, w_gate, w_up, w_down):
    logits = jnp.einsum('...d,de->...e', h, w_router).astype(jnp.float32)
    scores = jax.nn.sigmoid(logits)
    sel = scores + b_router.astype(jnp.float32)
    grouped = sel.reshape(sel.shape[:-1] + (N_GROUPS, EXPERTS_PER_GROUP))
    group_score = jnp.sum(lax.top_k(grouped, TOP_K)[0], axis=-1)
    best = jnp.argmax(group_score, axis=-1)
    in_group = (jnp.arange(N_EXPERTS) // EXPERTS_PER_GROUP) == best[..., None]
    _, idx = lax.top_k(jnp.where(in_group, sel, -jnp.inf), TOP_K)
    wts = jnp.take_along_axis(scores, idx, axis=-1)
    wts = wts / jnp.sum(wts, axis=-1, keepdims=True)
    gates = jnp.sum(jax.nn.one_hot(idx, N_EXPERTS, dtype=jnp.float32) * wts[..., None], axis=-2)
    g = jnp.einsum('...d,edf->...ef', h, w_gate)
    u = jnp.einsum('...d,edf->...ef', h, w_up)
    a = jax.nn.silu(g) * u * gates[..., None].astype(h.dtype)
    return jnp.einsum('...ef,efd->...d', a, w_down)


def setup_inputs(seed: int = 0) -> dict:
    key = jax.random.key(seed)
    ks = jax.random.split(key, 25)

    def nrm(k, shape, s):
        return jax.random.normal(k, shape, jnp.float32) * s

    return {
        'x': nrm(ks[0], (BATCH, SEQ, D_MODEL), 1.0),
        'c': nrm(ks[1], (BATCH, D_MODEL), 1.0),
        'ctx': nrm(ks[2], (BATCH, CTX_LEN, D_MODEL), 1.0),
        'c_ctx': nrm(ks[3], (D_MODEL,), 1.0),
        'w_mod': nrm(ks[4], (DEPTH, D_MODEL, 6 * D_MODEL), 0.5 * D_MODEL ** -0.5),
        'b_mod': nrm(ks[5], (DEPTH, 6 * D_MODEL), 0.02),
        'ln_g': 1.0 + nrm(ks[6], (DEPTH, 2, D_MODEL), 0.02),
        'ln_b': nrm(ks[7], (DEPTH, 2, D_MODEL), 0.02),
        'w_in_even': nrm(ks[8], (N_EVEN, D_MODEL, EVEN_IN), D_MODEL ** -0.5),
        'w_out_even': nrm(ks[9], (N_EVEN, EVEN_OUT, D_MODEL), BETA * EVEN_OUT ** -0.5),
        'sink_logits': nrm(ks[10], (N_EVEN, A_HQ), 0.5),
        'lam_q1': nrm(ks[11], (N_EVEN, B_HD), 0.1),
        'lam_k1': nrm(ks[12], (N_EVEN, B_HD), 0.1),
        'lam_q2': nrm(ks[13], (N_EVEN, B_HD), 0.1),
        'lam_k2': nrm(ks[14], (N_EVEN, B_HD), 0.1),
        'subln_g': 1.0 + nrm(ks[15], (N_EVEN, B_VD), 0.02),
        'w_in_odd': nrm(ks[16], (N_ODD, D_MODEL, ODD_IN), D_MODEL ** -0.5),
        'w_out_odd': nrm(ks[17], (N_ODD, ODD_OUT, D_MODEL), BETA * ODD_OUT ** -0.5),
        'q_norm_g': 1.0 + nrm(ks[18], (N_ODD, C_HD), 0.02),
        'k_norm_g': 1.0 + nrm(ks[19], (N_ODD, C_HD), 0.02),
        'w_router': nrm(ks[20], (D_MODEL, N_EXPERTS), D_MODEL ** -0.5),
        'b_router': nrm(ks[21], (N_EXPERTS,), 0.01),
        'w_gate': nrm(ks[22], (DEPTH, N_EXPERTS, D_MODEL, D_EXPERT), D_MODEL ** -0.5),
        'w_up': nrm(ks[23], (DEPTH, N_EXPERTS, D_MODEL, D_EXPERT), D_MODEL ** -0.5),
        'w_down': nrm(ks[24], (DEPTH, N_EXPERTS, D_EXPERT, D_MODEL), BETA * D_EXPERT ** -0.5),
    }


def reference(x, c, ctx, c_ctx, w_mod, b_mod, ln_g, ln_b, w_in_even, w_out_even, sink_logits,
              lam_q1, lam_k1, lam_q2, lam_k2, subln_g, w_in_odd, w_out_odd, q_norm_g, k_norm_g,
              w_router, b_router, w_gate, w_up, w_down):
    xc = ctx
    cond = jax.nn.silu(c)
    cond_ctx = jax.nn.silu(c_ctx)
    for l in range(DEPTH):
        ctx_out = l < DEPTH - 1
        sh1, sc1, g1, sh2, sc2, g2 = [m[:, None, :] for m in jnp.split(cond @ w_mod[l] + b_mod[l], 6, axis=-1)]
        shc1, scc1, gc1, shc2, scc2, gc2 = jnp.split(cond_ctx @ w_mod[l] + b_mod[l], 6, axis=-1)
        h = x * (1.0 + sc1) + sh1
        hc = xc * (1.0 + scc1) + shc1
        i = l // 2
        if l % 2 == 0:
            lam_init = 0.8 - 0.6 * math.exp(-0.3 * l)
            y, yc = even_mixer(h, hc, w_in_even[i], w_out_even[i], sink_logits[i], lam_q1[i], lam_k1[i],
                               lam_q2[i], lam_k2[i], subln_g[i], lam_init, ctx_out)
        else:
            y, yc = odd_mixer(h, hc, w_in_odd[i], w_out_odd[i], q_norm_g[i], k_norm_g[i], ctx_out)
        x = layer_norm(ALPHA * x + g1 * y, ln_g[l, 0], ln_b[l, 0])
        h2 = x * (1.0 + sc2) + sh2
        x = layer_norm(ALPHA * x + g2 * moe(h2, w_router, b_router, w_gate[l], w_up[l], w_down[l]), ln_g[l, 1], ln_b[l, 1])
        if ctx_out:
            xc = layer_norm(ALPHA * xc + gc1 * yc, ln_g[l, 0], ln_b[l, 0])
            hc2 = xc * (1.0 + scc2) + shc2
            xc = layer_norm(ALPHA * xc + gc2 * moe(hc2, w_router, b_router, w_gate[l], w_up[l], w_down[l]), ln_g[l, 1], ln_b[l, 1])
    return x
```

```python
import functools
import math

import jax
import jax.numpy as jnp
from jax import lax
from jax.experimental import pallas as pl
from jax.experimental.pallas import tpu as pltpu

D_MODEL = 1024
SEQ = 16384
DEPTH = 4
GRID_W = 64
CTX_LEN = 256
WINDOW = 128
ROPE_THETA = 10000.0
A_HQ, A_HKV, A_HD = 8, 2, 64
B_H, B_HD = 4, 64
B_VD = 2 * B_HD
C_HQ, C_HKV, C_HD = 8, 2, 128
N_EXPERTS = 16
N_GROUPS = 4
EXPERTS_PER_GROUP = N_EXPERTS // N_GROUPS
D_EXPERT = 512
ALPHA = (2 * DEPTH) ** 0.25
LN_EPS = 1e-5
QK_EPS = 1e-6
SUBLN_EPS = 1e-5

LANES = 128
TQ = CTX_LEN
TK = 512
MOE_TM = 1024
VMEM_LIMIT = 56 * 1024 * 1024
LOG2E = 1.4426950408889634
BF16 = jnp.bfloat16
F32 = jnp.float32


def _cparams(*sem):
    return pltpu.CompilerParams(dimension_semantics=sem, vmem_limit_bytes=VMEM_LIMIT)


def _dot(a, b):
    return jnp.dot(a, b, preferred_element_type=F32)


def _dot_nt(a, b):
    return lax.dot_general(a, b, (((1,), (1,)), ((), ())), preferred_element_type=F32)


def _lane_tile(x, width):
    reps = width // x.shape[-1]
    return x if reps == 1 else jnp.concatenate([x] * reps, axis=-1)


def _mod_kernel(c_ref, w_ref, b_ref, o_ref):
    c = c_ref[...]
    s = c * jax.nn.sigmoid(c)
    o_ref[0] = _dot(s.astype(BF16), w_ref[0].astype(BF16)) + b_ref[0]


def _modulation(c8, w_mod, b_mod):
    depth, d, n6 = w_mod.shape
    tn = 1536
    out = pl.pallas_call(
        _mod_kernel,
        grid=(depth, n6 // tn),
        in_specs=[
            pl.BlockSpec((8, d), lambda l, j: (0, 0)),
            pl.BlockSpec((1, d, tn), lambda l, j: (l, 0, j)),
            pl.BlockSpec((1, 1, tn), lambda l, j: (l, 0, j)),
        ],
        out_specs=pl.BlockSpec((1, 8, tn), lambda l, j: (l, 0, j)),
        out_shape=jax.ShapeDtypeStruct((depth, 8, n6), F32),
        compiler_params=_cparams("arbitrary", "arbitrary"),
        name="modulation",
    )(c8, w_mod, b_mod.reshape(depth, 1, n6))
    m = out[:, :2].reshape(depth, 2, 6, d)
    m = jnp.pad(m, ((0, 0), (0, 0), (0, 2), (0, 0)))
    return m.reshape(depth * 2, 8, d)


def _mod_spec(layer):
    return pl.BlockSpec((1, 8, D_MODEL), lambda i, *_: (2 * layer + jnp.minimum(i, 1), 0, 0))


def _rope_chunk(z, cos, sin, quarter):
    lane = lax.broadcasted_iota(jnp.int32, z.shape, 1)
    first = (lane % (2 * quarter)) < quarter
    nxt = pltpu.roll(z, LANES - quarter, axis=1)
    prv = pltpu.roll(z, quarter, axis=1)
    return z * cos + jnp.where(first, nxt, prv) * sin


def _inproj_kernel(segs, quarter, n_gain, x_ref, mod_ref, w_ref, cos_ref, sin_ref, *rest):
    gains = rest[:n_gain]
    outs = rest[n_gain:]
    sh = mod_ref[0, 0:1, :]
    sc = mod_ref[0, 1:2, :]
    hb = (x_ref[...] * (1.0 + sc) + sh).astype(BF16)
    cos = cos_ref[...]
    sin = sin_ref[...]
    off = 0
    for (width, rope, gain_idx), o_ref in zip(segs, outs):
        z = _dot(hb, w_ref[:, off:off + width])
        for c in range(width // LANES):
            zc = z[:, c * LANES:(c + 1) * LANES]
            if gain_idx is not None:
                ms = jnp.mean(zc * zc, axis=-1, keepdims=True)
                zc = zc * lax.rsqrt(ms + QK_EPS) * gains[gain_idx][...]
            if rope:
                zc = _rope_chunk(zc, cos, sin, quarter)
            o_ref[:, c * LANES:(c + 1) * LANES] = zc.astype(BF16)
        off += width


def _inproj(x, mods, layer, w, cos, sin, segs, quarter, gains=()):
    n, d = x.shape
    wtot = w.shape[1]
    row = lambda i: (i, 0)
    fixed = lambda i: (0, 0)
    in_specs = [
        pl.BlockSpec((TQ, d), row),
        _mod_spec(layer),
        pl.BlockSpec((d, wtot), fixed),
        pl.BlockSpec((TQ, LANES), row),
        pl.BlockSpec((TQ, LANES), row),
    ] + [pl.BlockSpec((1, LANES), fixed) for _ in gains]
    return pl.pallas_call(
        functools.partial(_inproj_kernel, segs, quarter, len(gains)),
        grid=(n // TQ,),
        in_specs=in_specs,
        out_specs=[pl.BlockSpec((TQ, s[0]), row) for s in segs],
        out_shape=[jax.ShapeDtypeStruct((n, s[0]), BF16) for s in segs],
        compiler_params=_cparams("arbitrary"),
        name="inproj",
    )(x, mods, w, cos, sin, *gains)


def _softmax_step(s, vc, m_ref, l_ref, acc_ref, idx, c):
    tk = s.shape[1]
    m_prev = m_ref[idx]
    m_new = jnp.maximum(m_prev, jnp.max(s, axis=1, keepdims=True))
    p = jnp.exp2((s - _lane_tile(m_new, tk)) * c)
    alpha = jnp.exp2((m_prev - m_new) * c)
    l_ref[idx] = alpha * l_ref[idx] + jnp.sum(p, axis=1, keepdims=True)
    acc_ref[idx] = acc_ref[idx] * alpha + _dot(p.astype(BF16), vc)
    m_ref[idx] = m_new


def _chunk_loop(is_ctx_block, n_keys, body):
    body(0, CTX_LEN)
    n_chunks = (n_keys - CTX_LEN) // TK

    def step(j, carry):
        body(pl.multiple_of(CTX_LEN + j * TK, TK // 2), TK)
        return carry

    lax.fori_loop(0, jnp.where(is_ctx_block, 0, n_chunks), step, 0)


def _attn_c_kernel(q_ref, k_ref, v_ref, o_ref, m_ref, l_ref, acc_ref):
    grp = C_HQ // C_HKV
    c = (C_HD ** -0.5) * LOG2E
    m_ref[...] = jnp.full(m_ref.shape, -jnp.inf, F32)
    l_ref[...] = jnp.zeros(l_ref.shape, F32)
    acc_ref[...] = jnp.zeros(acc_ref.shape, F32)

    def body(start, size):
        kc = k_ref[pl.ds(start, size), :]
        vc = v_ref[pl.ds(start, size), :]
        for h in range(grp):
            s = _dot_nt(q_ref[:, h * C_HD:(h + 1) * C_HD], kc)
            _softmax_step(s, vc, m_ref, l_ref, acc_ref, h, c)

    _chunk_loop(pl.program_id(1) == 0, k_ref.shape[0], body)
    for h in range(grp):
        o_ref[:, h * C_HD:(h + 1) * C_HD] = (acc_ref[h] / l_ref[h]).astype(BF16)


def _attn_c(q, k, v):
    n = q.shape[0]
    grp = C_HQ // C_HKV
    wq = grp * C_HD
    return pl.pallas_call(
        _attn_c_kernel,
        grid=(C_HKV, n // TQ),
        in_specs=[
            pl.BlockSpec((TQ, wq), lambda g, i: (i, g)),
            pl.BlockSpec((n, C_HD), lambda g, i: (0, g)),
            pl.BlockSpec((n, C_HD), lambda g, i: (0, g)),
        ],
        out_specs=pl.BlockSpec((TQ, wq), lambda g, i: (i, g)),
        out_shape=jax.ShapeDtypeStruct((n, C_HQ * C_HD), BF16),
        scratch_shapes=[pltpu.VMEM((grp, TQ, LANES), F32)] * 3,
        compiler_params=_cparams("arbitrary", "arbitrary"),
        name="attn_c",
    )(q, k, v)


def _attn_b_kernel(lam_init, q_ref, k_ref, v_ref, lq1_ref, lk1_ref, lq2_ref, lk2_ref, g_ref, o_ref,
                   m_ref, l_ref, acc_ref):
    c = (B_HD ** -0.5) * LOG2E
    m_ref[...] = jnp.full(m_ref.shape, -jnp.inf, F32)
    l_ref[...] = jnp.zeros(l_ref.shape, F32)
    acc_ref[...] = jnp.zeros(acc_ref.shape, F32)
    q = q_ref[...]
    lane = lax.broadcasted_iota(jnp.int32, q.shape, 1)
    zero = jnp.zeros_like(q)
    q_maps = (jnp.where(lane < B_HD, q, zero), jnp.where(lane >= B_HD, q, zero))

    def body(start, size):
        kc = k_ref[pl.ds(start, size), :]
        vc = v_ref[pl.ds(start, size), :]
        for mp in range(2):
            _softmax_step(_dot_nt(q_maps[mp], kc), vc, m_ref, l_ref, acc_ref, mp, c)

    _chunk_loop(pl.program_id(1) == 0, k_ref.shape[0], body)
    lam = (jnp.exp(jnp.sum(lq1_ref[...] * lk1_ref[...], axis=-1, keepdims=True))
           - jnp.exp(jnp.sum(lq2_ref[...] * lk2_ref[...], axis=-1, keepdims=True))) + lam_init
    o = acc_ref[0] / l_ref[0] - lam * (acc_ref[1] / l_ref[1])
    ms = jnp.mean(o * o, axis=-1, keepdims=True)
    o = o * lax.rsqrt(ms + SUBLN_EPS) * g_ref[...] * (1.0 - lam_init)
    o_ref[...] = o.astype(BF16)


def _attn_b(q, k, v, lq1, lk1, lq2, lk2, subln_g, lam_init):
    n = q.shape[0]
    blk = lambda h, i: (i, h)
    col = lambda h, i: (0, h)
    fixed = lambda h, i: (0, 0)
    small = pl.BlockSpec((1, B_HD), fixed)
    return pl.pallas_call(
        functools.partial(_attn_b_kernel, lam_init),
        grid=(B_H, n // TQ),
        in_specs=[
            pl.BlockSpec((TQ, B_VD), blk),
            pl.BlockSpec((n, B_VD), col),
            pl.BlockSpec((n, B_VD), col),
            small, small, small, small,
            pl.BlockSpec((1, B_VD), fixed),
        ],
        out_specs=pl.BlockSpec((TQ, B_VD), blk),
        out_shape=jax.ShapeDtypeStruct((n, B_H * B_VD), BF16),
        scratch_shapes=[pltpu.VMEM((2, TQ, LANES), F32)] * 3,
        compiler_params=_cparams("arbitrary", "arbitrary"),
        name="attn_b",
    )(q, k, v, lq1, lk1, lq2, lk2, subln_g)


A_WIN_KEYS = TQ + 2 * WINDOW


def _attn_a_kernel(q_ref, k_ref, v_ref, sink_ref, o_ref):
    i = pl.program_id(0)
    c = (A_HD ** -0.5) * LOG2E
    n_lat = k_ref.shape[0] - CTX_LEN
    kc = k_ref[0:CTX_LEN, :]
    vc = v_ref[0:CTX_LEN, :]
    n_blk = A_HQ * A_HD // LANES

    def heads(local):
        for j in range(n_blk):
            q = q_ref[:, j * LANES:(j + 1) * LANES]
            lane = lax.broadcasted_iota(jnp.int32, q.shape, 1)
            zero = jnp.zeros_like(q)
            halves = []
            for half in range(2):
                qh = jnp.where(lane >= A_HD if half else lane < A_HD, q, zero)
                sink = sink_ref[2 * j + half:2 * j + half + 1, 0:1]
                s_c = _dot_nt(qh, kc)
                m = jnp.maximum(jnp.max(s_c, axis=1, keepdims=True), sink)
                if local is not None:
                    kl, vl, ok = local
                    s_l = jnp.where(ok, _dot_nt(qh, kl), -jnp.inf)
                    m = jnp.maximum(m, jnp.max(s_l, axis=1, keepdims=True))
                p_c = jnp.exp2((s_c - m) * c)
                l = jnp.exp2((sink - m) * c) + jnp.sum(p_c, axis=1, keepdims=True)
                acc = _dot(p_c.astype(BF16), vc)
                if local is not None:
                    p_l = jnp.exp2((s_l - m) * c)
                    l = l + jnp.sum(p_l, axis=1, keepdims=True)
                    acc = acc + _dot(p_l.astype(BF16), vl)
                halves.append(acc / l)
            o_ref[:, j * LANES:(j + 1) * LANES] = jnp.where(lane < A_HD, halves[0], halves[1]).astype(BF16)

    @pl.when(i == 0)
    def _():
        heads(None)

    @pl.when(i > 0)
    def _():
        q0 = (i - 1) * TQ
        ws = jnp.clip(q0 - WINDOW, 0, n_lat - A_WIN_KEYS)
        ws = pl.multiple_of(ws, WINDOW)
        kl = k_ref[pl.ds(CTX_LEN + ws, A_WIN_KEYS), :]
        vl = v_ref[pl.ds(CTX_LEN + ws, A_WIN_KEYS), :]
        qpos = q0 + lax.broadcasted_iota(jnp.int32, (TQ, A_WIN_KEYS), 0)
        kpos = ws + lax.broadcasted_iota(jnp.int32, (TQ, A_WIN_KEYS), 1)
        ok = jnp.abs(qpos - kpos) <= WINDOW
        heads((kl, vl, ok))


def _attn_a(q, k, v, sink_rows):
    n = q.shape[0]
    wq = A_HQ * A_HD
    wkv = A_HKV * A_HD
    return pl.pallas_call(
        _attn_a_kernel,
        grid=(n // TQ,),
        in_specs=[
            pl.BlockSpec((TQ, wq), lambda i: (i, 0)),
            pl.BlockSpec((n, wkv), lambda i: (0, 0)),
            pl.BlockSpec((n, wkv), lambda i: (0, 0)),
            pl.BlockSpec((A_HQ, LANES), lambda i: (0, 0)),
        ],
        out_specs=pl.BlockSpec((TQ, wq), lambda i: (i, 0)),
        out_shape=jax.ShapeDtypeStruct((n, wq), BF16),
        compiler_params=_cparams("arbitrary"),
        name="attn_a",
    )(q, k, v, sink_rows)


def _layer_norm(z, g, b):
    mu = jnp.mean(z, axis=-1, keepdims=True)
    zc = z - mu
    var = jnp.mean(zc * zc, axis=-1, keepdims=True)
    return zc * lax.rsqrt(var + LN_EPS) * g + b


def _route(logits_t, bias_t):
    t = logits_t.shape[1]
    score = [jax.nn.sigmoid(logits_t[e:e + 1, :]) for e in range(N_EXPERTS)]
    sel = [score[e] + bias_t[e:e + 1, :] for e in range(N_EXPERTS)]
    best_g = jnp.zeros((1, t), jnp.int32)
    best_s = None
    for g in range(N_GROUPS):
        a = sel[g * EXPERTS_PER_GROUP:(g + 1) * EXPERTS_PER_GROUP]
        gs = None
        for x in range(EXPERTS_PER_GROUP):
            for y in range(x + 1, EXPERTS_PER_GROUP):
                pair = a[x] + a[y]
                gs = pair if gs is None else jnp.maximum(gs, pair)
        if best_s is None:
            best_s = gs
        else:
            upd = gs > best_s
            best_g = jnp.where(upd, g, best_g)
            best_s = jnp.where(upd, gs, best_s)
    neg = jnp.full((1, t), -jnp.inf, F32)
    masked = [jnp.where(best_g == (e // EXPERTS_PER_GROUP), sel[e], neg) for e in range(N_EXPERTS)]
    i1 = jnp.zeros((1, t), jnp.int32)
    v1 = masked[0]
    for e in range(1, N_EXPERTS):
        upd = masked[e] > v1
        i1 = jnp.where(upd, e, i1)
        v1 = jnp.where(upd, masked[e], v1)
    i2 = jnp.full((1, t), -1, jnp.int32)
    v2 = neg
    for e in range(N_EXPERTS):
        upd = (masked[e] > v2) & (i1 != e)
        i2 = jnp.where(upd, e, i2)
        v2 = jnp.where(upd, masked[e], v2)
    zero = jnp.zeros((1, t), F32)
    w1 = zero
    w2 = zero
    for e in range(N_EXPERTS):
        w1 = w1 + jnp.where(i1 == e, score[e], zero)
        w2 = w2 + jnp.where(i2 == e, score[e], zero)
    den = w1 + w2
    w1 = w1 / den
    w2 = w2 / den
    row = lax.broadcasted_iota(jnp.int32, logits_t.shape, 0)
    return jnp.where(row == i1, w1, 0.0) + jnp.where(row == i2, w2, 0.0)


def _post_kernel(o1_ref, o2_ref, w1_ref, w2_ref, x_ref, mod_ref, lng_ref, lnb_ref, wr_ref, br_ref,
                 x1_ref, h2_ref, gates_ref):
    g1 = mod_ref[0, 2:3, :]
    sh2 = mod_ref[0, 3:4, :]
    sc2 = mod_ref[0, 4:5, :]
    y = _dot(o1_ref[...], w1_ref[...]) + _dot(o2_ref[...], w2_ref[...])
    x1 = _layer_norm(ALPHA * x_ref[...] + g1 * y, lng_ref[...], lnb_ref[...])
    x1_ref[...] = x1
    h2 = x1 * (1.0 + sc2) + sh2
    h2_ref[...] = h2.astype(BF16)
    logits = jnp.dot(h2, wr_ref[...], preferred_element_type=F32, precision=lax.Precision.HIGHEST)
    gates_t = _route(logits.T, br_ref[...])
    gates_ref[...] = gates_t.T


def _post(o1, o2, o1_col, o2_col, w1, w2, x, mods, layer, ln_g, ln_b, wr, br):
    n, d = x.shape
    half = w1.shape[0]
    row = lambda i: (i, 0)
    fixed = lambda i: (0, 0)
    return pl.pallas_call(
        _post_kernel,
        grid=(n // TQ,),
        in_specs=[
            pl.BlockSpec((TQ, half), lambda i: (i, o1_col)),
            pl.BlockSpec((TQ, half), lambda i: (i, o2_col)),
            pl.BlockSpec((half, d), fixed),
            pl.BlockSpec((half, d), fixed),
            pl.BlockSpec((TQ, d), row),
            _mod_spec(layer),
            pl.BlockSpec((1, d), fixed),
            pl.BlockSpec((1, d), fixed),
            pl.BlockSpec((d, LANES), fixed),
            pl.BlockSpec((LANES, 1), fixed),
        ],
        out_specs=[
            pl.BlockSpec((TQ, d), row),
            pl.BlockSpec((TQ, d), row),
            pl.BlockSpec((TQ, LANES), row),
        ],
        out_shape=[
            jax.ShapeDtypeStruct((n, d), F32),
            jax.ShapeDtypeStruct((n, d), BF16),
            jax.ShapeDtypeStruct((n, LANES), F32),
        ],
        compiler_params=_cparams("arbitrary"),
        name="post_attn",
    )(o1, o2, w1, w2, x, mods, ln_g, ln_b, wr, br)


def _moe_kernel(h_ref, gates_ref, wg_ref, wu_ref, wd_ref, x_ref, mod_ref, lng_ref, lnb_ref, o_ref, acc_ref):
    e = pl.program_id(1)

    @pl.when(e == 0)
    def _():
        acc_ref[...] = jnp.zeros(acc_ref.shape, F32)

    h = h_ref[...]
    g = _dot(h, wg_ref[0])
    u = _dot(h, wu_ref[0])
    gates = gates_ref[...]
    lane = lax.broadcasted_iota(jnp.int32, gates.shape, 1)
    gate = jnp.sum(jnp.where(lane == e, gates, 0.0), axis=1, keepdims=True)
    a = (g * jax.nn.sigmoid(g)) * u * gate
    acc_ref[...] += _dot(a.astype(BF16), wd_ref[0])

    @pl.when(e == pl.num_programs(1) - 1)
    def _():
        g2 = mod_ref[0, 5:6, :]
        o_ref[...] = _layer_norm(ALPHA * x_ref[...] + g2 * acc_ref[...], lng_ref[...], lnb_ref[...])


def _moe(h2, gates, wg, wu, wd, x1, mods, layer, ln_g, ln_b):
    n, d = x1.shape
    ne, _, f = wg.shape
    n_ctx_tiles = -(-CTX_LEN // MOE_TM)
    row = lambda i, e: (i, 0)
    fixed = lambda i, e: (0, 0)
    def call(h2_, gates_, x1_, kind):
        m = h2_.shape[0]
        tm = MOE_TM if m % MOE_TM == 0 else m
        return pl.pallas_call(
            _moe_kernel,
            grid=(m // tm, ne),
            in_specs=[
                pl.BlockSpec((tm, d), row),
                pl.BlockSpec((tm, LANES), row),
                pl.BlockSpec((1, d, f), lambda i, e: (e, 0, 0)),
                pl.BlockSpec((1, d, f), lambda i, e: (e, 0, 0)),
                pl.BlockSpec((1, f, d), lambda i, e: (e, 0, 0)),
                pl.BlockSpec((tm, d), row),
                pl.BlockSpec((1, 8, d), lambda i, e: (2 * layer + kind, 0, 0)),
                pl.BlockSpec((1, d), fixed),
                pl.BlockSpec((1, d), fixed),
            ],
            out_specs=pl.BlockSpec((tm, d), row),
            out_shape=jax.ShapeDtypeStruct((m, d), F32),
            scratch_shapes=[pltpu.VMEM((tm, d), F32)],
            compiler_params=_cparams("arbitrary", "arbitrary"),
            name="moe",
        )(h2_, gates_, wg, wu, wd, x1_, mods, ln_g, ln_b)

    del n_ctx_tiles, n
    out_ctx = call(h2[:CTX_LEN], gates[:CTX_LEN], x1[:CTX_LEN], 0)
    out_lat = call(h2[CTX_LEN:], gates[CTX_LEN:], x1[CTX_LEN:], 1)
    return jnp.concatenate([out_ctx, out_lat], axis=0)


def _rope_tables(n_lat, head_dim):
    quarter = head_dim // 4
    t = jnp.arange(n_lat)
    inv_freq = ROPE_THETA ** (-jnp.arange(quarter, dtype=F32) / quarter)
    ang_r = (t // GRID_W)[:, None].astype(F32) * inv_freq
    ang_c = (t % GRID_W)[:, None].astype(F32) * inv_freq
    cos = jnp.concatenate([jnp.cos(ang_r)] * 2 + [jnp.cos(ang_c)] * 2, axis=1)
    sin = jnp.concatenate([-jnp.sin(ang_r), jnp.sin(ang_r), -jnp.sin(ang_c), jnp.sin(ang_c)], axis=1)
    reps = LANES // head_dim
    cos = jnp.tile(cos, (1, reps))
    sin = jnp.tile(sin, (1, reps))
    cos = jnp.concatenate([jnp.ones((CTX_LEN, LANES), F32), cos], axis=0)
    sin = jnp.concatenate([jnp.zeros((CTX_LEN, LANES), F32), sin], axis=0)
    return cos, sin


def _a_head_order():
    grp = A_HQ // A_HKV
    order = []
    for j in range(grp):
        for g in range(A_HKV):
            order.append(g * grp + j)
    return order


def kernel(x, c, ctx, c_ctx, w_mod, b_mod, ln_g, ln_b, w_in_even, w_out_even, sink_logits, lam_q1, lam_k1,
           lam_q2, lam_k2, subln_g, w_in_odd, w_out_odd, q_norm_g, k_norm_g, w_router, b_router, w_gate,
           w_up, w_down):
    n_lat = x.shape[1]
    assert x.shape == (1, n_lat, D_MODEL) and ctx.shape == (1, CTX_LEN, D_MODEL)
    assert n_lat % max(TK, MOE_TM) == 0 and n_lat % GRID_W == 0
    xs = jnp.concatenate([ctx[0], x[0]], axis=0)

    c8 = jnp.zeros((8, D_MODEL), F32).at[0].set(c_ctx).at[1].set(c[0])
    mods = _modulation(c8, w_mod, b_mod)

    cos64, sin64 = _rope_tables(n_lat, A_HD)
    cos128, sin128 = _rope_tables(n_lat, C_HD)

    order = _a_head_order()
    a_cols = jnp.concatenate([jnp.arange(h * A_HD, (h + 1) * A_HD) for h in order])
    wq_a = A_HQ * A_HD
    w_in_even_p = jnp.concatenate([w_in_even[:, :, :wq_a][:, :, a_cols], w_in_even[:, :, wq_a:]], axis=2).astype(BF16)
    w_out_even_p = jnp.concatenate([w_out_even[:, :wq_a][:, a_cols], w_out_even[:, wq_a:]], axis=1).astype(BF16)
    w_in_odd_b = w_in_odd.astype(BF16)
    w_out_odd_b = w_out_odd.astype(BF16)
    wg = w_gate.astype(BF16)
    wu = w_up.astype(BF16)
    wd = w_down.astype(BF16)
    wr = jnp.pad(w_router, ((0, 0), (0, LANES - N_EXPERTS)))
    br = jnp.pad(b_router.astype(F32), (0, LANES - N_EXPERTS)).reshape(LANES, 1)

    segs_even = ((A_HQ * A_HD, True, None), (A_HKV * A_HD, True, None), (A_HKV * A_HD, False, None),
                 (B_H * 2 * B_HD, True, None), (B_H * 2 * B_HD, True, None), (B_H * B_VD, False, None))
    segs_odd = ((C_HQ * C_HD, True, 0), (C_HKV * C_HD, True, 1), (C_HKV * C_HD, False, None))

    for l in range(DEPTH):
        i = l // 2
        lng = ln_g[l]
        lnb = ln_b[l]
        if l % 2 == 0:
            lam_init = 0.8 - 0.6 * math.exp(-0.3 * l)
            aq, ak, av, bq, bk, bv = _inproj(xs, mods, l, w_in_even_p[i], cos64, sin64, segs_even, A_HD // 4)
            sink_rows = jnp.broadcast_to(
                (sink_logits[i][jnp.array(order)] * (A_HD ** 0.5))[:, None], (A_HQ, LANES)).astype(F32)
            oa = _attn_a(aq, ak, av, sink_rows)
            ob = _attn_b(bq, bk, bv, lam_q1[i][None], lam_k1[i][None], lam_q2[i][None], lam_k2[i][None],
                         subln_g[i][None], lam_init)
            half = wq_a
            x1, h2, gates = _post(oa, ob, 0, 0, w_out_even_p[i][:half], w_out_even_p[i][half:], xs, mods, l,
                                  lng[0][None], lnb[0][None], wr, br)
        else:
            q, k, v = _inproj(xs, mods, l, w_in_odd_b[i], cos128, sin128, segs_odd, C_HD // 4,
                              gains=(q_norm_g[i][None], k_norm_g[i][None]))
            o = _attn_c(q, k, v)
            half = C_HQ * C_HD // 2
            x1, h2, gates = _post(o, o, 0, 1, w_out_odd_b[i][:half], w_out_odd_b[i][half:], xs, mods, l,
                                  lng[0][None], lnb[0][None], wr, br)
        xs = _moe(h2, gates, wg[l], wu[l], wd[l], x1, mods, l, lng[1][None], lnb[1][None])
    return xs[CTX_LEN:][None]
```

```python
import functools
import math

import jax
import jax.numpy as jnp
from jax import lax
from jax.experimental import pallas as pl
from jax.experimental.pallas import tpu as pltpu

D_MODEL = 1024
SEQ = 16384
DEPTH = 4
GRID_W = 64
CTX_LEN = 256
WINDOW = 128
ROPE_THETA = 10000.0
A_HQ, A_HKV, A_HD = 8, 2, 64
B_H, B_HD = 4, 64
B_VD = 2 * B_HD
C_HQ, C_HKV, C_HD = 8, 2, 128
N_EXPERTS = 16
N_GROUPS = 4
EXPERTS_PER_GROUP = N_EXPERTS // N_GROUPS
D_EXPERT = 512
ALPHA = (2 * DEPTH) ** 0.25
LN_EPS = 1e-5
QK_EPS = 1e-6
SUBLN_EPS = 1e-5

LANES = 128
TQ = CTX_LEN
TK = 512
MOE_TM = 1024
VMEM_LIMIT = 56 * 1024 * 1024
LOG2E = 1.4426950408889634
BF16 = jnp.bfloat16
F32 = jnp.float32


def _cparams(*sem):
    return pltpu.CompilerParams(dimension_semantics=sem, vmem_limit_bytes=VMEM_LIMIT)


def _dot(a, b):
    return jnp.dot(a, b, preferred_element_type=F32)


def _dot_nt(a, b):
    return lax.dot_general(a, b, (((1,), (1,)), ((), ())), preferred_element_type=F32)


def _lane_tile(x, width):
    reps = width // x.shape[-1]
    return x if reps == 1 else jnp.concatenate([x] * reps, axis=-1)


def _mod_kernel(c_ref, w_ref, b_ref, o_ref):
    c = c_ref[...]
    s = c * jax.nn.sigmoid(c)
    o_ref[0] = _dot(s.astype(BF16), w_ref[0].astype(BF16)) + b_ref[0]


def _modulation(c8, w_mod, b_mod):
    depth, d, n6 = w_mod.shape
    tn = 1536
    out = pl.pallas_call(
        _mod_kernel,
        grid=(depth, n6 // tn),
        in_specs=[
            pl.BlockSpec((8, d), lambda l, j: (0, 0)),
            pl.BlockSpec((1, d, tn), lambda l, j: (l, 0, j)),
            pl.BlockSpec((1, 1, tn), lambda l, j: (l, 0, j)),
        ],
        out_specs=pl.BlockSpec((1, 8, tn), lambda l, j: (l, 0, j)),
        out_shape=jax.ShapeDtypeStruct((depth, 8, n6), F32),
        compiler_params=_cparams("arbitrary", "arbitrary"),
        name="modulation",
    )(c8, w_mod, b_mod.reshape(depth, 1, n6))
    m = out[:, :2].reshape(depth, 2, 6, d)
    m = jnp.pad(m, ((0, 0), (0, 0), (0, 2), (0, 0)))
    return m.reshape(depth * 2, 8, d)


def _mod_spec(layer):
    return pl.BlockSpec((1, 8, D_MODEL), lambda i, *_: (2 * layer + jnp.minimum(i, 1), 0, 0))


def _rope_chunk(z, cos, sin, quarter):
    lane = lax.broadcasted_iota(jnp.int32, z.shape, 1)
    first = (lane % (2 * quarter)) < quarter
    nxt = pltpu.roll(z, LANES - quarter, axis=1)
    prv = pltpu.roll(z, quarter, axis=1)
    return z * cos + jnp.where(first, nxt, prv) * sin


def _inproj_kernel(segs, quarter, n_gain, x_ref, mod_ref, w_ref, cos_ref, sin_ref, *rest):
    gains = rest[:n_gain]
    outs = rest[n_gain:]
    sh = mod_ref[0, 0:1, :]
    sc = mod_ref[0, 1:2, :]
    hb = (x_ref[...] * (1.0 + sc) + sh).astype(BF16)
    cos = cos_ref[...]
    sin = sin_ref[...]
    off = 0
    for (width, rope, gain_idx, scale, with_ones), o_ref in zip(segs, outs):
        z = _dot(hb, w_ref[:, off:off + width])
        for c in range(width // LANES):
            zc = z[:, c * LANES:(c + 1) * LANES]
            if gain_idx is not None:
                ms = jnp.mean(zc * zc, axis=-1, keepdims=True)
                zc = zc * lax.rsqrt(ms + QK_EPS) * gains[gain_idx][...]
            if rope:
                zc = _rope_chunk(zc, cos, sin, quarter)
            if scale is not None:
                zc = zc * scale
            if with_ones:
                o_ref[:, 2 * c * LANES:(2 * c + 1) * LANES] = zc.astype(BF16)
                o_ref[:, (2 * c + 1) * LANES:(2 * c + 2) * LANES] = jnp.ones(zc.shape, BF16)
            else:
                o_ref[:, c * LANES:(c + 1) * LANES] = zc.astype(BF16)
        off += width


def _inproj(x, mods, layer, w, cos, sin, segs, quarter, gains=()):
    n, d = x.shape
    wtot = w.shape[1]
    row = lambda i: (i, 0)
    fixed = lambda i: (0, 0)
    in_specs = [
        pl.BlockSpec((TQ, d), row),
        _mod_spec(layer),
        pl.BlockSpec((d, wtot), fixed),
        pl.BlockSpec((TQ, LANES), row),
        pl.BlockSpec((TQ, LANES), row),
    ] + [pl.BlockSpec((1, LANES), fixed) for _ in gains]
    widths = [s[0] * (2 if s[4] else 1) for s in segs]
    return pl.pallas_call(
        functools.partial(_inproj_kernel, segs, quarter, len(gains)),
        grid=(n // TQ,),
        in_specs=in_specs,
        out_specs=[pl.BlockSpec((TQ, w_), row) for w_ in widths],
        out_shape=[jax.ShapeDtypeStruct((n, w_), BF16) for w_ in widths],
        compiler_params=_cparams("arbitrary"),
        name="inproj",
    )(x, mods, w, cos, sin, *gains)


def _flash(qs_ref, k_ref, v_ref, m_ref, acc_ref, s_refs, is_ctx_block):
    n_chunks = (k_ref.shape[0] - CTX_LEN) // TK
    m_ref[...] = jnp.full(m_ref.shape, -jnp.inf, F32)
    acc_ref[...] = jnp.zeros(acc_ref.shape, F32)

    def scores(start, size):
        return _dot_nt(qs_ref[...], k_ref[pl.ds(start, size), :])

    def update(s, start, size):
        m_prev = m_ref[...]
        m_new = jnp.maximum(m_prev, jnp.max(s, axis=1, keepdims=True))
        p = jnp.exp2(s - _lane_tile(m_new, size)).astype(BF16)
        alpha = jnp.exp2(m_prev - m_new)
        acc_ref[...] = acc_ref[...] * _lane_tile(alpha, 2 * LANES) + _dot(p, v_ref[pl.ds(start, size), :])
        m_ref[...] = m_new

    def lat(j):
        return pl.multiple_of(CTX_LEN + j * TK, TK // 2)

    update(scores(0, CTX_LEN), 0, CTX_LEN)

    @pl.when(jnp.logical_not(is_ctx_block))
    def _():
        s_refs[0][...] = scores(lat(0), TK)

        def pair(jj, carry):
            j = 2 * jj
            s_refs[1][...] = scores(lat(j + 1), TK)
            update(s_refs[0][...], lat(j), TK)
            s_refs[0][...] = scores(lat(jnp.minimum(j + 2, n_chunks - 1)), TK)
            update(s_refs[1][...], lat(j + 1), TK)
            return carry

        lax.fori_loop(0, n_chunks // 2, pair, 0)


def _flash_out(acc_ref, idx):
    rows = acc_ref[idx * TQ:(idx + 1) * TQ, :]
    return rows[:, :LANES] / rows[:, LANES:]


def _flash_scratch(n_stack):
    rows = n_stack * TQ
    return [
        pltpu.VMEM((rows, LANES), BF16),
        pltpu.VMEM((rows, LANES), F32),
        pltpu.VMEM((rows, 2 * LANES), F32),
        pltpu.VMEM((rows, TK), F32),
        pltpu.VMEM((rows, TK), F32),
    ]


def _attn_c_kernel(q_ref, k_ref, v_ref, o_ref, qs_ref, m_ref, acc_ref, s0_ref, s1_ref):
    grp = C_HQ // C_HKV
    for h in range(grp):
        qs_ref[h * TQ:(h + 1) * TQ, :] = q_ref[:, h * C_HD:(h + 1) * C_HD]
    _flash(qs_ref, k_ref, v_ref, m_ref, acc_ref, (s0_ref, s1_ref), pl.program_id(1) == 0)
    for h in range(grp):
        o_ref[:, h * C_HD:(h + 1) * C_HD] = _flash_out(acc_ref, h).astype(BF16)


def _attn_c(q, k, v_aug):
    n = q.shape[0]
    grp = C_HQ // C_HKV
    wq = grp * C_HD
    return pl.pallas_call(
        _attn_c_kernel,
        grid=(C_HKV, n // TQ),
        in_specs=[
            pl.BlockSpec((TQ, wq), lambda g, i: (i, g)),
            pl.BlockSpec((n, C_HD), lambda g, i: (0, g)),
            pl.BlockSpec((n, 2 * C_HD), lambda g, i: (0, g)),
        ],
        out_specs=pl.BlockSpec((TQ, wq), lambda g, i: (i, g)),
        out_shape=jax.ShapeDtypeStruct((n, C_HQ * C_HD), BF16),
        scratch_shapes=_flash_scratch(grp),
        compiler_params=_cparams("arbitrary", "arbitrary"),
        name="attn_c",
    )(q, k, v_aug)


def _attn_b_kernel(lam_init, q_ref, k_ref, v_ref, lq1_ref, lk1_ref, lq2_ref, lk2_ref, g_ref, o_ref,
                   qs_ref, m_ref, acc_ref, s0_ref, s1_ref):
    q = q_ref[...]
    lane = lax.broadcasted_iota(jnp.int32, q.shape, 1)
    zero = jnp.zeros_like(q)
    qs_ref[0:TQ, :] = jnp.where(lane < B_HD, q, zero)
    qs_ref[TQ:2 * TQ, :] = jnp.where(lane >= B_HD, q, zero)
    _flash(qs_ref, k_ref, v_ref, m_ref, acc_ref, (s0_ref, s1_ref), pl.program_id(1) == 0)
    lam = (jnp.exp(jnp.sum(lq1_ref[...] * lk1_ref[...], axis=-1, keepdims=True))
           - jnp.exp(jnp.sum(lq2_ref[...] * lk2_ref[...], axis=-1, keepdims=True))) + lam_init
    o = _flash_out(acc_ref, 0) - lam * _flash_out(acc_ref, 1)
    ms = jnp.mean(o * o, axis=-1, keepdims=True)
    o = o * lax.rsqrt(ms + SUBLN_EPS) * g_ref[...] * (1.0 - lam_init)
    o_ref[...] = o.astype(BF16)


def _attn_b(q, k, v_aug, lq1, lk1, lq2, lk2, subln_g, lam_init):
    n = q.shape[0]
    blk = lambda h, i: (i, h)
    col = lambda h, i: (0, h)
    fixed = lambda h, i: (0, 0)
    small = pl.BlockSpec((1, B_HD), fixed)
    return pl.pallas_call(
        functools.partial(_attn_b_kernel, lam_init),
        grid=(B_H, n // TQ),
        in_specs=[
            pl.BlockSpec((TQ, B_VD), blk),
            pl.BlockSpec((n, B_VD), col),
            pl.BlockSpec((n, 2 * B_VD), col),
            small, small, small, small,
            pl.BlockSpec((1, B_VD), fixed),
        ],
        out_specs=pl.BlockSpec((TQ, B_VD), blk),
        out_shape=jax.ShapeDtypeStruct((n, B_H * B_VD), BF16),
        scratch_shapes=_flash_scratch(2),
        compiler_params=_cparams("arbitrary", "arbitrary"),
        name="attn_b",
    )(q, k, v_aug, lq1, lk1, lq2, lk2, subln_g)


A_WIN_KEYS = TQ + 2 * WINDOW


def _attn_a_kernel(q_ref, k_ref, v_ref, sink_ref, o_ref):
    i = pl.program_id(0)
    c = (A_HD ** -0.5) * LOG2E
    n_lat = k_ref.shape[0] - CTX_LEN
    kc = k_ref[0:CTX_LEN, :]
    vc = v_ref[0:CTX_LEN, :]
    n_blk = A_HQ * A_HD // LANES

    def heads(local):
        for j in range(n_blk):
            q = q_ref[:, j * LANES:(j + 1) * LANES]
            lane = lax.broadcasted_iota(jnp.int32, q.shape, 1)
            zero = jnp.zeros_like(q)
            halves = []
            for half in range(2):
                qh = jnp.where(lane >= A_HD if half else lane < A_HD, q, zero)
                sink = sink_ref[2 * j + half:2 * j + half + 1, 0:1]
                s_c = _dot_nt(qh, kc)
                m = jnp.maximum(jnp.max(s_c, axis=1, keepdims=True), sink)
                if local is not None:
                    kl, vl, ok = local
                    s_l = jnp.where(ok, _dot_nt(qh, kl), -jnp.inf)
                    m = jnp.maximum(m, jnp.max(s_l, axis=1, keepdims=True))
                p_c = jnp.exp2((s_c - m) * c)
                l = jnp.exp2((sink - m) * c) + jnp.sum(p_c, axis=1, keepdims=True)
                acc = _dot(p_c.astype(BF16), vc)
                if local is not None:
                    p_l = jnp.exp2((s_l - m) * c)
                    l = l + jnp.sum(p_l, axis=1, keepdims=True)
                    acc = acc + _dot(p_l.astype(BF16), vl)
                halves.append(acc / l)
            o_ref[:, j * LANES:(j + 1) * LANES] = jnp.where(lane < A_HD, halves[0], halves[1]).astype(BF16)

    @pl.when(i == 0)
    def _():
        heads(None)

    @pl.when(i > 0)
    def _():
        q0 = (i - 1) * TQ
        ws = jnp.clip(q0 - WINDOW, 0, n_lat - A_WIN_KEYS)
        ws = pl.multiple_of(ws, WINDOW)
        kl = k_ref[pl.ds(CTX_LEN + ws, A_WIN_KEYS), :]
        vl = v_ref[pl.ds(CTX_LEN + ws, A_WIN_KEYS), :]
        qpos = q0 + lax.broadcasted_iota(jnp.int32, (TQ, A_WIN_KEYS), 0)
        kpos = ws + lax.broadcasted_iota(jnp.int32, (TQ, A_WIN_KEYS), 1)
        ok = jnp.abs(qpos - kpos) <= WINDOW
        heads((kl, vl, ok))


def _attn_a(q, k, v, sink_rows):
    n = q.shape[0]
    wq = A_HQ * A_HD
    wkv = A_HKV * A_HD
    return pl.pallas_call(
        _attn_a_kernel,
        grid=(n // TQ,),
        in_specs=[
            pl.BlockSpec((TQ, wq), lambda i: (i, 0)),
            pl.BlockSpec((n, wkv), lambda i: (0, 0)),
            pl.BlockSpec((n, wkv), lambda i: (0, 0)),
            pl.BlockSpec((A_HQ, LANES), lambda i: (0, 0)),
        ],
        out_specs=pl.BlockSpec((TQ, wq), lambda i: (i, 0)),
        out_shape=jax.ShapeDtypeStruct((n, wq), BF16),
        compiler_params=_cparams("arbitrary"),
        name="attn_a",
    )(q, k, v, sink_rows)


def _layer_norm(z, g, b):
    mu = jnp.mean(z, axis=-1, keepdims=True)
    zc = z - mu
    var = jnp.mean(zc * zc, axis=-1, keepdims=True)
    return zc * lax.rsqrt(var + LN_EPS) * g + b


def _route(logits_t, bias_t):
    t = logits_t.shape[1]
    score = [jax.nn.sigmoid(logits_t[e:e + 1, :]) for e in range(N_EXPERTS)]
    sel = [score[e] + bias_t[e:e + 1, :] for e in range(N_EXPERTS)]
    best_g = jnp.zeros((1, t), jnp.int32)
    best_s = None
    for g in range(N_GROUPS):
        a = sel[g * EXPERTS_PER_GROUP:(g + 1) * EXPERTS_PER_GROUP]
        gs = None
        for x in range(EXPERTS_PER_GROUP):
            for y in range(x + 1, EXPERTS_PER_GROUP):
                pair = a[x] + a[y]
                gs = pair if gs is None else jnp.maximum(gs, pair)
        if best_s is None:
            best_s = gs
        else:
            upd = gs > best_s
            best_g = jnp.where(upd, g, best_g)
            best_s = jnp.where(upd, gs, best_s)
    neg = jnp.full((1, t), -jnp.inf, F32)
    masked = [jnp.where(best_g == (e // EXPERTS_PER_GROUP), sel[e], neg) for e in range(N_EXPERTS)]
    i1 = jnp.zeros((1, t), jnp.int32)
    v1 = masked[0]
    for e in range(1, N_EXPERTS):
        upd = masked[e] > v1
        i1 = jnp.where(upd, e, i1)
        v1 = jnp.where(upd, masked[e], v1)
    i2 = jnp.full((1, t), -1, jnp.int32)
    v2 = neg
    for e in range(N_EXPERTS):
        upd = (masked[e] > v2) & (i1 != e)
        i2 = jnp.where(upd, e, i2)
        v2 = jnp.where(upd, masked[e], v2)
    zero = jnp.zeros((1, t), F32)
    w1 = zero
    w2 = zero
    for e in range(N_EXPERTS):
        w1 = w1 + jnp.where(i1 == e, score[e], zero)
        w2 = w2 + jnp.where(i2 == e, score[e], zero)
    den = w1 + w2
    w1 = w1 / den
    w2 = w2 / den
    row = lax.broadcasted_iota(jnp.int32, logits_t.shape, 0)
    return jnp.where(row == i1, w1, 0.0) + jnp.where(row == i2, w2, 0.0)


def _post_kernel(o1_ref, o2_ref, w1_ref, w2_ref, x_ref, mod_ref, lng_ref, lnb_ref, wr_ref, br_ref,
                 x1_ref, h2_ref, gates_ref):
    g1 = mod_ref[0, 2:3, :]
    sh2 = mod_ref[0, 3:4, :]
    sc2 = mod_ref[0, 4:5, :]
    y = _dot(o1_ref[...], w1_ref[...]) + _dot(o2_ref[...], w2_ref[...])
    x1 = _layer_norm(ALPHA * x_ref[...] + g1 * y, lng_ref[...], lnb_ref[...])
    x1_ref[...] = x1
    h2 = x1 * (1.0 + sc2) + sh2
    h2_ref[...] = h2.astype(BF16)
    logits = jnp.dot(h2, wr_ref[...], preferred_element_type=F32, precision=lax.Precision.HIGHEST)
    gates_t = _route(logits.T, br_ref[...])
    gates_ref[...] = gates_t.T


def _post(o1, o2, o1_col, o2_col, w1, w2, x, mods, layer, ln_g, ln_b, wr, br):
    n, d = x.shape
    half = w1.shape[0]
    row = lambda i: (i, 0)
    fixed = lambda i: (0, 0)
    return pl.pallas_call(
        _post_kernel,
        grid=(n // TQ,),
        in_specs=[
            pl.BlockSpec((TQ, half), lambda i: (i, o1_col)),
            pl.BlockSpec((TQ, half), lambda i: (i, o2_col)),
            pl.BlockSpec((half, d), fixed),
            pl.BlockSpec((half, d), fixed),
            pl.BlockSpec((TQ, d), row),
            _mod_spec(layer),
            pl.BlockSpec((1, d), fixed),
            pl.BlockSpec((1, d), fixed),
            pl.BlockSpec((d, LANES), fixed),
            pl.BlockSpec((LANES, 1), fixed),
        ],
        out_specs=[
            pl.BlockSpec((TQ, d), row),
            pl.BlockSpec((TQ, d), row),
            pl.BlockSpec((TQ, LANES), row),
        ],
        out_shape=[
            jax.ShapeDtypeStruct((n, d), F32),
            jax.ShapeDtypeStruct((n, d), BF16),
            jax.ShapeDtypeStruct((n, LANES), F32),
        ],
        compiler_params=_cparams("arbitrary"),
        name="post_attn",
    )(o1, o2, w1, w2, x, mods, ln_g, ln_b, wr, br)


def _moe_kernel(h_ref, gates_ref, wg_ref, wu_ref, wd_ref, x_ref, mod_ref, lng_ref, lnb_ref, o_ref, acc_ref):
    e = pl.program_id(1)

    @pl.when(e == 0)
    def _():
        acc_ref[...] = jnp.zeros(acc_ref.shape, F32)

    h = h_ref[...]
    g = _dot(h, wg_ref[0])
    u = _dot(h, wu_ref[0])
    gates = gates_ref[...]
    lane = lax.broadcasted_iota(jnp.int32, gates.shape, 1)
    gate = jnp.sum(jnp.where(lane == e, gates, 0.0), axis=1, keepdims=True)
    a = (g * jax.nn.sigmoid(g)) * u * gate
    acc_ref[...] += _dot(a.astype(BF16), wd_ref[0])

    @pl.when(e == pl.num_programs(1) - 1)
    def _():
        g2 = mod_ref[0, 5:6, :]
        o_ref[...] = _layer_norm(ALPHA * x_ref[...] + g2 * acc_ref[...], lng_ref[...], lnb_ref[...])


def _moe(h2, gates, wg, wu, wd, x1, mods, layer, ln_g, ln_b):
    n, d = x1.shape
    ne, _, f = wg.shape
    n_ctx_tiles = -(-CTX_LEN // MOE_TM)
    row = lambda i, e: (i, 0)
    fixed = lambda i, e: (0, 0)
    def call(h2_, gates_, x1_, kind):
        m = h2_.shape[0]
        tm = MOE_TM if m % MOE_TM == 0 else m
        return pl.pallas_call(
            _moe_kernel,
            grid=(m // tm, ne),
            in_specs=[
                pl.BlockSpec((tm, d), row),
                pl.BlockSpec((tm, LANES), row),
                pl.BlockSpec((1, d, f), lambda i, e: (e, 0, 0)),
                pl.BlockSpec((1, d, f), lambda i, e: (e, 0, 0)),
                pl.BlockSpec((1, f, d), lambda i, e: (e, 0, 0)),
                pl.BlockSpec((tm, d), row),
                pl.BlockSpec((1, 8, d), lambda i, e: (2 * layer + kind, 0, 0)),
                pl.BlockSpec((1, d), fixed),
                pl.BlockSpec((1, d), fixed),
            ],
            out_specs=pl.BlockSpec((tm, d), row),
            out_shape=jax.ShapeDtypeStruct((m, d), F32),
            scratch_shapes=[pltpu.VMEM((tm, d), F32)],
            compiler_params=_cparams("arbitrary", "arbitrary"),
            name="moe",
        )(h2_, gates_, wg, wu, wd, x1_, mods, ln_g, ln_b)

    del n_ctx_tiles, n
    out_ctx = call(h2[:CTX_LEN], gates[:CTX_LEN], x1[:CTX_LEN], 0)
    out_lat = call(h2[CTX_LEN:], gates[CTX_LEN:], x1[CTX_LEN:], 1)
    return jnp.concatenate([out_ctx, out_lat], axis=0)


def _rope_tables(n_lat, head_dim):
    quarter = head_dim // 4
    t = jnp.arange(n_lat)
    inv_freq = ROPE_THETA ** (-jnp.arange(quarter, dtype=F32) / quarter)
    ang_r = (t // GRID_W)[:, None].astype(F32) * inv_freq
    ang_c = (t % GRID_W)[:, None].astype(F32) * inv_freq
    cos = jnp.concatenate([jnp.cos(ang_r)] * 2 + [jnp.cos(ang_c)] * 2, axis=1)
    sin = jnp.concatenate([-jnp.sin(ang_r), jnp.sin(ang_r), -jnp.sin(ang_c), jnp.sin(ang_c)], axis=1)
    reps = LANES // head_dim
    cos = jnp.tile(cos, (1, reps))
    sin = jnp.tile(sin, (1, reps))
    cos = jnp.concatenate([jnp.ones((CTX_LEN, LANES), F32), cos], axis=0)
    sin = jnp.concatenate([jnp.zeros((CTX_LEN, LANES), F32), sin], axis=0)
    return cos, sin


def _a_head_order():
    grp = A_HQ // A_HKV
    order = []
    for j in range(grp):
        for g in range(A_HKV):
            order.append(g * grp + j)
    return order


def kernel(x, c, ctx, c_ctx, w_mod, b_mod, ln_g, ln_b, w_in_even, w_out_even, sink_logits, lam_q1, lam_k1,
           lam_q2, lam_k2, subln_g, w_in_odd, w_out_odd, q_norm_g, k_norm_g, w_router, b_router, w_gate,
           w_up, w_down):
    n_lat = x.shape[1]
    assert x.shape == (1, n_lat, D_MODEL) and ctx.shape == (1, CTX_LEN, D_MODEL)
    assert n_lat % max(TK, MOE_TM) == 0 and n_lat % GRID_W == 0
    xs = jnp.concatenate([ctx[0], x[0]], axis=0)

    c8 = jnp.zeros((8, D_MODEL), F32).at[0].set(c_ctx).at[1].set(c[0])
    mods = _modulation(c8, w_mod, b_mod)

    cos64, sin64 = _rope_tables(n_lat, A_HD)
    cos128, sin128 = _rope_tables(n_lat, C_HD)

    order = _a_head_order()
    a_cols = jnp.concatenate([jnp.arange(h * A_HD, (h + 1) * A_HD) for h in order])
    wq_a = A_HQ * A_HD
    w_in_even_p = jnp.concatenate([w_in_even[:, :, :wq_a][:, :, a_cols], w_in_even[:, :, wq_a:]], axis=2).astype(BF16)
    w_out_even_p = jnp.concatenate([w_out_even[:, :wq_a][:, a_cols], w_out_even[:, wq_a:]], axis=1).astype(BF16)
    w_in_odd_b = w_in_odd.astype(BF16)
    w_out_odd_b = w_out_odd.astype(BF16)
    wg = w_gate.astype(BF16)
    wu = w_up.astype(BF16)
    wd = w_down.astype(BF16)
    wr = jnp.pad(w_router, ((0, 0), (0, LANES - N_EXPERTS)))
    br = jnp.pad(b_router.astype(F32), (0, LANES - N_EXPERTS)).reshape(LANES, 1)

    segs_even = ((A_HQ * A_HD, True, None, None, False), (A_HKV * A_HD, True, None, None, False),
                 (A_HKV * A_HD, False, None, None, False),
                 (B_H * 2 * B_HD, True, None, (B_HD ** -0.5) * LOG2E, False),
                 (B_H * 2 * B_HD, True, None, None, False), (B_H * B_VD, False, None, None, True))
    segs_odd = ((C_HQ * C_HD, True, 0, (C_HD ** -0.5) * LOG2E, False), (C_HKV * C_HD, True, 1, None, False),
                (C_HKV * C_HD, False, None, None, True))

    for l in range(DEPTH):
        i = l // 2
        lng = ln_g[l]
        lnb = ln_b[l]
        if l % 2 == 0:
            lam_init = 0.8 - 0.6 * math.exp(-0.3 * l)
            aq, ak, av, bq, bk, bv = _inproj(xs, mods, l, w_in_even_p[i], cos64, sin64, segs_even, A_HD // 4)
            sink_rows = jnp.broadcast_to(
                (sink_logits[i][jnp.array(order)] * (A_HD ** 0.5))[:, None], (A_HQ, LANES)).astype(F32)
            oa = _attn_a(aq, ak, av, sink_rows)
            ob = _attn_b(bq, bk, bv, lam_q1[i][None], lam_k1[i][None], lam_q2[i][None], lam_k2[i][None],
                         subln_g[i][None], lam_init)
            half = wq_a
            x1, h2, gates = _post(oa, ob, 0, 0, w_out_even_p[i][:half], w_out_even_p[i][half:], xs, mods, l,
                                  lng[0][None], lnb[0][None], wr, br)
        else:
            q, k, v = _inproj(xs, mods, l, w_in_odd_b[i], cos128, sin128, segs_odd, C_HD // 4,
                              gains=(q_norm_g[i][None], k_norm_g[i][None]))
            o = _attn_c(q, k, v)
            half = C_HQ * C_HD // 2
            x1, h2, gates = _post(o, o, 0, 1, w_out_odd_b[i][:half], w_out_odd_b[i][half:], xs, mods, l,
                                  lng[0][None], lnb[0][None], wr, br)
        xs = _moe(h2, gates, wg[l], wu[l], wd[l], x1, mods, l, lng[1][None], lnb[1][None])
    return xs[CTX_LEN:][None]
```

```python
import functools
import math

import jax
import jax.numpy as jnp
from jax import lax
from jax.experimental import pallas as pl
from jax.experimental.pallas import tpu as pltpu

D_MODEL = 1024
SEQ = 16384
DEPTH = 4
GRID_W = 64
CTX_LEN = 256
WINDOW = 128
ROPE_THETA = 10000.0
A_HQ, A_HKV, A_HD = 8, 2, 64
B_H, B_HD = 4, 64
B_VD = 2 * B_HD
C_HQ, C_HKV, C_HD = 8, 2, 128
N_EXPERTS = 16
N_GROUPS = 4
EXPERTS_PER_GROUP = N_EXPERTS // N_GROUPS
D_EXPERT = 512
ALPHA = (2 * DEPTH) ** 0.25
LN_EPS = 1e-5
QK_EPS = 1e-6
SUBLN_EPS = 1e-5

LANES = 128
TQ = CTX_LEN
TK = 512
UNROLL = 4
MOE_TM = 256
VMEM_LIMIT = 56 * 1024 * 1024
LOG2E = 1.4426950408889634
BF16 = jnp.bfloat16
F32 = jnp.float32


def _cparams(*sem):
    return pltpu.CompilerParams(dimension_semantics=sem, vmem_limit_bytes=VMEM_LIMIT)


def _dot(a, b):
    return jnp.dot(a, b, preferred_element_type=F32)


def _dot_nt(a, b):
    return lax.dot_general(a, b, (((1,), (1,)), ((), ())), preferred_element_type=F32)


def _lane_tile(x, width):
    reps = width // x.shape[-1]
    return x if reps == 1 else jnp.concatenate([x] * reps, axis=-1)


def _mod_kernel(c_ref, w_ref, b_ref, o_ref):
    c = c_ref[...]
    s = c * jax.nn.sigmoid(c)
    o_ref[0] = _dot(s.astype(BF16), w_ref[0].astype(BF16)) + b_ref[0]


def _modulation(c8, w_mod, b_mod):
    depth, d, n6 = w_mod.shape
    tn = 1536
    out = pl.pallas_call(
        _mod_kernel,
        grid=(depth, n6 // tn),
        in_specs=[
            pl.BlockSpec((8, d), lambda l, j: (0, 0)),
            pl.BlockSpec((1, d, tn), lambda l, j: (l, 0, j)),
            pl.BlockSpec((1, 1, tn), lambda l, j: (l, 0, j)),
        ],
        out_specs=pl.BlockSpec((1, 8, tn), lambda l, j: (l, 0, j)),
        out_shape=jax.ShapeDtypeStruct((depth, 8, n6), F32),
        compiler_params=_cparams("arbitrary", "arbitrary"),
        name="modulation",
    )(c8, w_mod, b_mod.reshape(depth, 1, n6))
    m = out[:, :2].reshape(depth, 2, 6, d)
    m = jnp.pad(m, ((0, 0), (0, 0), (0, 2), (0, 0)))
    return m.reshape(depth * 2, 8, d)


def _mod_spec(layer):
    return pl.BlockSpec((1, 8, D_MODEL), lambda i, *_: (2 * layer + jnp.minimum(i, 1), 0, 0))


def _rope_chunk(z, cos, sin, quarter):
    lane = lax.broadcasted_iota(jnp.int32, z.shape, 1)
    first = (lane % (2 * quarter)) < quarter
    nxt = pltpu.roll(z, LANES - quarter, axis=1)
    prv = pltpu.roll(z, quarter, axis=1)
    return z * cos + jnp.where(first, nxt, prv) * sin


def _inproj_kernel(segs, quarter, n_gain, x_ref, mod_ref, w_ref, cos_ref, sin_ref, *rest):
    gains = rest[:n_gain]
    outs = rest[n_gain:]
    sh = mod_ref[0, 0:1, :]
    sc = mod_ref[0, 1:2, :]
    hb = (x_ref[...] * (1.0 + sc) + sh).astype(BF16)
    cos = cos_ref[...]
    sin = sin_ref[...]
    off = 0
    for (width, rope, gain_idx, scale, with_ones), o_ref in zip(segs, outs):
        z = _dot(hb, w_ref[:, off:off + width])
        for c in range(width // LANES):
            zc = z[:, c * LANES:(c + 1) * LANES]
            if gain_idx is not None:
                ms = jnp.mean(zc * zc, axis=-1, keepdims=True)
                zc = zc * lax.rsqrt(ms + QK_EPS) * gains[gain_idx][...]
            if rope:
                zc = _rope_chunk(zc, cos, sin, quarter)
            if scale is not None:
                zc = zc * scale
            if with_ones:
                o_ref[:, 2 * c * LANES:(2 * c + 1) * LANES] = zc.astype(BF16)
                o_ref[:, (2 * c + 1) * LANES:(2 * c + 2) * LANES] = jnp.ones(zc.shape, BF16)
            else:
                o_ref[:, c * LANES:(c + 1) * LANES] = zc.astype(BF16)
        off += width


def _inproj(x, mods, layer, w, cos, sin, segs, quarter, gains=()):
    n, d = x.shape
    wtot = w.shape[1]
    row = lambda i: (i, 0)
    fixed = lambda i: (0, 0)
    in_specs = [
        pl.BlockSpec((TQ, d), row),
        _mod_spec(layer),
        pl.BlockSpec((d, wtot), fixed),
        pl.BlockSpec((TQ, LANES), row),
        pl.BlockSpec((TQ, LANES), row),
    ] + [pl.BlockSpec((1, LANES), fixed) for _ in gains]
    widths = [s[0] * (2 if s[4] else 1) for s in segs]
    return pl.pallas_call(
        functools.partial(_inproj_kernel, segs, quarter, len(gains)),
        grid=(n // TQ,),
        in_specs=in_specs,
        out_specs=[pl.BlockSpec((TQ, w_), row) for w_ in widths],
        out_shape=[jax.ShapeDtypeStruct((n, w_), BF16) for w_ in widths],
        compiler_params=_cparams("arbitrary"),
        name="inproj",
    )(x, mods, w, cos, sin, *gains)


def _flash(qs_ref, k_ref, v_ref, m_ref, acc_ref, s_refs, is_ctx_block):
    n_chunks = (k_ref.shape[0] - CTX_LEN) // TK
    m_ref[...] = jnp.full(m_ref.shape, -jnp.inf, F32)
    acc_ref[...] = jnp.zeros(acc_ref.shape, F32)

    def scores(start, size):
        return _dot_nt(qs_ref[...], k_ref[pl.ds(start, size), :])

    def update(s, start, size):
        m_prev = m_ref[...]
        m_new = jnp.maximum(m_prev, jnp.max(s, axis=1, keepdims=True))
        p = jnp.exp2(s - _lane_tile(m_new, size)).astype(BF16)
        alpha = jnp.exp2(m_prev - m_new)
        acc_ref[...] = acc_ref[...] * _lane_tile(alpha, 2 * LANES) + _dot(p, v_ref[pl.ds(start, size), :])
        m_ref[...] = m_new

    def lat(j):
        return pl.multiple_of(CTX_LEN + j * TK, CTX_LEN)

    update(scores(0, CTX_LEN), 0, CTX_LEN)

    @pl.when(jnp.logical_not(is_ctx_block))
    def _():
        s_refs[0][...] = scores(lat(0), TK)

        def trip(jj, carry):
            j = UNROLL * jj
            for u in range(UNROLL):
                s_refs[(u + 1) % 2][...] = scores(lat(jnp.minimum(j + u + 1, n_chunks - 1)), TK)
                update(s_refs[u % 2][...], lat(j + u), TK)
            return carry

        lax.fori_loop(0, n_chunks // UNROLL, trip, 0)


def _flash_out(acc_ref, idx):
    rows = acc_ref[idx * TQ:(idx + 1) * TQ, :]
    return rows[:, :LANES] / rows[:, LANES:]


def _flash_scratch(n_stack):
    rows = n_stack * TQ
    return [
        pltpu.VMEM((rows, LANES), BF16),
        pltpu.VMEM((rows, LANES), F32),
        pltpu.VMEM((rows, 2 * LANES), F32),
        pltpu.VMEM((rows, TK), F32),
        pltpu.VMEM((rows, TK), F32),
    ]


def _attn_c_kernel(q_ref, k_ref, v_ref, o_ref, qs_ref, m_ref, acc_ref, s0_ref, s1_ref):
    grp = C_HQ // C_HKV
    for h in range(grp):
        qs_ref[h * TQ:(h + 1) * TQ, :] = q_ref[:, h * C_HD:(h + 1) * C_HD]
    _flash(qs_ref, k_ref, v_ref, m_ref, acc_ref, (s0_ref, s1_ref), pl.program_id(1) == 0)
    for h in range(grp):
        o_ref[:, h * C_HD:(h + 1) * C_HD] = _flash_out(acc_ref, h).astype(BF16)


def _attn_c(q, k, v_aug):
    n = q.shape[0]
    grp = C_HQ // C_HKV
    wq = grp * C_HD
    return pl.pallas_call(
        _attn_c_kernel,
        grid=(C_HKV, n // TQ),
        in_specs=[
            pl.BlockSpec((TQ, wq), lambda g, i: (i, g)),
            pl.BlockSpec((n, C_HD), lambda g, i: (0, g)),
            pl.BlockSpec((n, 2 * C_HD), lambda g, i: (0, g)),
        ],
        out_specs=pl.BlockSpec((TQ, wq), lambda g, i: (i, g)),
        out_shape=jax.ShapeDtypeStruct((n, C_HQ * C_HD), BF16),
        scratch_shapes=_flash_scratch(grp),
        compiler_params=_cparams("arbitrary", "arbitrary"),
        name="attn_c",
    )(q, k, v_aug)


def _attn_b_kernel(lam_init, q_ref, k_ref, v_ref, lq1_ref, lk1_ref, lq2_ref, lk2_ref, g_ref, o_ref,
                   qs_ref, m_ref, acc_ref, s0_ref, s1_ref):
    q = q_ref[...]
    lane = lax.broadcasted_iota(jnp.int32, q.shape, 1)
    zero = jnp.zeros_like(q)
    qs_ref[0:TQ, :] = jnp.where(lane < B_HD, q, zero)
    qs_ref[TQ:2 * TQ, :] = jnp.where(lane >= B_HD, q, zero)
    _flash(qs_ref, k_ref, v_ref, m_ref, acc_ref, (s0_ref, s1_ref), pl.program_id(1) == 0)
    lam = (jnp.exp(jnp.sum(lq1_ref[...] * lk1_ref[...], axis=-1, keepdims=True))
           - jnp.exp(jnp.sum(lq2_ref[...] * lk2_ref[...], axis=-1, keepdims=True))) + lam_init
    o = _flash_out(acc_ref, 0) - lam * _flash_out(acc_ref, 1)
    ms = jnp.mean(o * o, axis=-1, keepdims=True)
    o = o * lax.rsqrt(ms + SUBLN_EPS) * g_ref[...] * (1.0 - lam_init)
    o_ref[...] = o.astype(BF16)


def _attn_b(q, k, v_aug, lq1, lk1, lq2, lk2, subln_g, lam_init):
    n = q.shape[0]
    blk = lambda h, i: (i, h)
    col = lambda h, i: (0, h)
    fixed = lambda h, i: (0, 0)
    small = pl.BlockSpec((1, B_HD), fixed)
    return pl.pallas_call(
        functools.partial(_attn_b_kernel, lam_init),
        grid=(B_H, n // TQ),
        in_specs=[
            pl.BlockSpec((TQ, B_VD), blk),
            pl.BlockSpec((n, B_VD), col),
            pl.BlockSpec((n, 2 * B_VD), col),
            small, small, small, small,
            pl.BlockSpec((1, B_VD), fixed),
        ],
        out_specs=pl.BlockSpec((TQ, B_VD), blk),
        out_shape=jax.ShapeDtypeStruct((n, B_H * B_VD), BF16),
        scratch_shapes=_flash_scratch(2),
        compiler_params=_cparams("arbitrary", "arbitrary"),
        name="attn_b",
    )(q, k, v_aug, lq1, lk1, lq2, lk2, subln_g)


A_WIN_KEYS = TQ + 2 * WINDOW


def _attn_a_kernel(q_ref, k_ref, v_ref, sink_ref, o_ref):
    i = pl.program_id(0)
    c = (A_HD ** -0.5) * LOG2E
    n_lat = k_ref.shape[0] - CTX_LEN
    kc = k_ref[0:CTX_LEN, :]
    vc = v_ref[0:CTX_LEN, :]
    n_blk = A_HQ * A_HD // LANES

    def heads(local):
        for j in range(n_blk):
            q = q_ref[:, j * LANES:(j + 1) * LANES]
            lane = lax.broadcasted_iota(jnp.int32, q.shape, 1)
            zero = jnp.zeros_like(q)
            halves = []
            for half in range(2):
                qh = jnp.where(lane >= A_HD if half else lane < A_HD, q, zero)
                sink = sink_ref[2 * j + half:2 * j + half + 1, 0:1]
                s_c = _dot_nt(qh, kc)
                m = jnp.maximum(jnp.max(s_c, axis=1, keepdims=True), sink)
                if local is not None:
                    kl, vl, ok = local
                    s_l = jnp.where(ok, _dot_nt(qh, kl), -jnp.inf)
                    m = jnp.maximum(m, jnp.max(s_l, axis=1, keepdims=True))
                p_c = jnp.exp2((s_c - m) * c)
                l = jnp.exp2((sink - m) * c) + jnp.sum(p_c, axis=1, keepdims=True)
                acc = _dot(p_c.astype(BF16), vc)
                if local is not None:
                    p_l = jnp.exp2((s_l - m) * c)
                    l = l + jnp.sum(p_l, axis=1, keepdims=True)
                    acc = acc + _dot(p_l.astype(BF16), vl)
                halves.append(acc / l)
            o_ref[:, j * LANES:(j + 1) * LANES] = jnp.where(lane < A_HD, halves[0], halves[1]).astype(BF16)

    @pl.when(i == 0)
    def _():
        heads(None)

    @pl.when(i > 0)
    def _():
        q0 = (i - 1) * TQ
        ws = jnp.clip(q0 - WINDOW, 0, n_lat - A_WIN_KEYS)
        ws = pl.multiple_of(ws, WINDOW)
        kl = k_ref[pl.ds(CTX_LEN + ws, A_WIN_KEYS), :]
        vl = v_ref[pl.ds(CTX_LEN + ws, A_WIN_KEYS), :]
        qpos = q0 + lax.broadcasted_iota(jnp.int32, (TQ, A_WIN_KEYS), 0)
        kpos = ws + lax.broadcasted_iota(jnp.int32, (TQ, A_WIN_KEYS), 1)
        ok = jnp.abs(qpos - kpos) <= WINDOW
        heads((kl, vl, ok))


def _attn_a(q, k, v, sink_rows):
    n = q.shape[0]
    wq = A_HQ * A_HD
    wkv = A_HKV * A_HD
    return pl.pallas_call(
        _attn_a_kernel,
        grid=(n // TQ,),
        in_specs=[
            pl.BlockSpec((TQ, wq), lambda i: (i, 0)),
            pl.BlockSpec((n, wkv), lambda i: (0, 0)),
            pl.BlockSpec((n, wkv), lambda i: (0, 0)),
            pl.BlockSpec((A_HQ, LANES), lambda i: (0, 0)),
        ],
        out_specs=pl.BlockSpec((TQ, wq), lambda i: (i, 0)),
        out_shape=jax.ShapeDtypeStruct((n, wq), BF16),
        compiler_params=_cparams("arbitrary"),
        name="attn_a",
    )(q, k, v, sink_rows)


def _layer_norm(z, g, b):
    mu = jnp.mean(z, axis=-1, keepdims=True)
    zc = z - mu
    var = jnp.mean(zc * zc, axis=-1, keepdims=True)
    return zc * lax.rsqrt(var + LN_EPS) * g + b


def _route(logits_t, bias_t):
    t = logits_t.shape[1]
    score = [jax.nn.sigmoid(logits_t[e:e + 1, :]) for e in range(N_EXPERTS)]
    sel = [score[e] + bias_t[e:e + 1, :] for e in range(N_EXPERTS)]
    best_g = jnp.zeros((1, t), jnp.int32)
    best_s = None
    for g in range(N_GROUPS):
        a = sel[g * EXPERTS_PER_GROUP:(g + 1) * EXPERTS_PER_GROUP]
        gs = None
        for x in range(EXPERTS_PER_GROUP):
            for y in range(x + 1, EXPERTS_PER_GROUP):
                pair = a[x] + a[y]
                gs = pair if gs is None else jnp.maximum(gs, pair)
        if best_s is None:
            best_s = gs
        else:
            upd = gs > best_s
            best_g = jnp.where(upd, g, best_g)
            best_s = jnp.where(upd, gs, best_s)
    neg = jnp.full((1, t), -jnp.inf, F32)
    masked = [jnp.where(best_g == (e // EXPERTS_PER_GROUP), sel[e], neg) for e in range(N_EXPERTS)]
    i1 = jnp.zeros((1, t), jnp.int32)
    v1 = masked[0]
    for e in range(1, N_EXPERTS):
        upd = masked[e] > v1
        i1 = jnp.where(upd, e, i1)
        v1 = jnp.where(upd, masked[e], v1)
    i2 = jnp.full((1, t), -1, jnp.int32)
    v2 = neg
    for e in range(N_EXPERTS):
        upd = (masked[e] > v2) & (i1 != e)
        i2 = jnp.where(upd, e, i2)
        v2 = jnp.where(upd, masked[e], v2)
    zero = jnp.zeros((1, t), F32)
    w1 = zero
    w2 = zero
    for e in range(N_EXPERTS):
        w1 = w1 + jnp.where(i1 == e, score[e], zero)
        w2 = w2 + jnp.where(i2 == e, score[e], zero)
    den = w1 + w2
    return i1, i2, w1 / den, w2 / den


def _post_kernel(o1_ref, o2_ref, w1_ref, w2_ref, x_ref, mod_ref, lng_ref, lnb_ref, wr_ref, br_ref,
                 x1_ref, h2_ref, route_ref, wcol_ref, cnt_out_ref, cnt_ref):
    @pl.when(pl.program_id(0) == 0)
    def _():
        cnt_ref[...] = jnp.zeros(cnt_ref.shape, F32)

    g1 = mod_ref[0, 2:3, :]
    sh2 = mod_ref[0, 3:4, :]
    sc2 = mod_ref[0, 4:5, :]
    y = _dot(o1_ref[...], w1_ref[...]) + _dot(o2_ref[...], w2_ref[...])
    x1 = _layer_norm(ALPHA * x_ref[...] + g1 * y, lng_ref[...], lnb_ref[...])
    x1_ref[...] = x1
    h2 = x1 * (1.0 + sc2) + sh2
    h2_ref[...] = h2
    logits = jnp.dot(h2, wr_ref[...], preferred_element_type=F32, precision=lax.Precision.HIGHEST)
    i1, i2, w1, w2 = _route(logits.T, br_ref[...])

    row = lax.broadcasted_iota(jnp.int32, (LANES, TQ), 0)
    hit1 = row == i1
    hit2 = row == i2
    assigned = jnp.where(hit1 | hit2, 1.0, 0.0)
    t_from = lax.broadcasted_iota(jnp.int32, (TQ, TQ), 0)
    t_to = lax.broadcasted_iota(jnp.int32, (TQ, TQ), 1)
    earlier = jnp.where(t_from < t_to, 1.0, 0.0).astype(BF16)
    rank = cnt_ref[:, 0:1] + _dot(assigned.astype(BF16), earlier)
    r1 = jnp.sum(jnp.where(hit1, rank, 0.0), axis=0, keepdims=True).astype(jnp.int32)
    r2 = jnp.sum(jnp.where(hit2, rank, 0.0), axis=0, keepdims=True).astype(jnp.int32)
    cnt_ref[...] = cnt_ref[...] + jnp.sum(assigned, axis=1, keepdims=True)
    cnt_out_ref[...] = cnt_ref[...]

    row8 = lax.broadcasted_iota(jnp.int32, (8, TQ), 0)
    route_ref[0] = jnp.where(row8 == 0, i1, jnp.where(row8 == 1, i2, jnp.where(row8 == 2, r1,
                             jnp.where(row8 == 3, r2, 0))))
    wcol_ref[...] = jnp.where(row == 0, w1, jnp.where(row == 1, w2, 0.0)).T


def _post(o1, o2, o1_col, o2_col, w1, w2, x, mods, layer, ln_g, ln_b, wr, br):
    n, d = x.shape
    half = w1.shape[0]
    row = lambda i: (i, 0)
    fixed = lambda i: (0, 0)
    return pl.pallas_call(
        _post_kernel,
        grid=(n // TQ,),
        in_specs=[
            pl.BlockSpec((TQ, half), lambda i: (i, o1_col)),
            pl.BlockSpec((TQ, half), lambda i: (i, o2_col)),
            pl.BlockSpec((half, d), fixed),
            pl.BlockSpec((half, d), fixed),
            pl.BlockSpec((TQ, d), row),
            _mod_spec(layer),
            pl.BlockSpec((1, d), fixed),
            pl.BlockSpec((1, d), fixed),
            pl.BlockSpec((d, LANES), fixed),
            pl.BlockSpec((LANES, 1), fixed),
        ],
        out_specs=[
            pl.BlockSpec((TQ, d), row),
            pl.BlockSpec((TQ, d), row),
            pl.BlockSpec((1, 8, TQ), lambda i: (i, 0, 0)),
            pl.BlockSpec((TQ, LANES), row),
            pl.BlockSpec((LANES, LANES), fixed),
        ],
        out_shape=[
            jax.ShapeDtypeStruct((n, d), F32),
            jax.ShapeDtypeStruct((n, d), F32),
            jax.ShapeDtypeStruct((n // TQ, 8, TQ), jnp.int32),
            jax.ShapeDtypeStruct((n, LANES), F32),
            jax.ShapeDtypeStruct((LANES, LANES), F32),
        ],
        scratch_shapes=[pltpu.VMEM((LANES, LANES), F32)],
        compiler_params=_cparams("arbitrary"),
        name="post_attn",
    )(o1, o2, w1, w2, x, mods, ln_g, ln_b, wr, br)


def _moe_layout(route, counts, n):
    n_tiles_max = (2 * n) // MOE_TM + N_EXPERTS
    cnt = counts[:N_EXPERTS, 0].astype(jnp.int32)
    tiles = (cnt + MOE_TM - 1) // MOE_TM
    tiles_end = jnp.cumsum(tiles)
    seg_start = (tiles_end - tiles) * MOE_TM
    pos = jnp.stack([jnp.take(seg_start, route[:, 0]) + route[:, 2],
                     jnp.take(seg_start, route[:, 1]) + route[:, 3]], axis=1)
    n_used = tiles_end[-1]
    j = jnp.minimum(jnp.arange(n_tiles_max), n_used - 1)
    tile_expert = jnp.sum(j[:, None] >= tiles_end[None, :], axis=1).astype(jnp.int32)
    return pos.astype(jnp.int32), tile_expert, n_used.reshape(1).astype(jnp.int32), n_tiles_max


def _scatter_kernel(pos_ref, h_ref, xs_in_ref, xs_ref, sem):
    del xs_in_ref

    def row_copy(r, k):
        return pltpu.make_async_copy(h_ref.at[pl.ds(r, 1)], xs_ref.at[pl.ds(pos_ref[0, k, r], 1)], sem)

    def start(r, carry):
        row_copy(r, 0).start()
        row_copy(r, 1).start()
        return carry

    def wait(r, carry):
        row_copy(r, 0).wait()
        row_copy(r, 1).wait()
        return carry

    lax.fori_loop(0, TQ, start, 0, unroll=8)
    lax.fori_loop(0, TQ, wait, 0, unroll=8)


def _scatter(pos, h2, n_rows):
    n, d = h2.shape
    return pl.pallas_call(
        _scatter_kernel,
        grid=(n // TQ,),
        in_specs=[
            pl.BlockSpec((1, 2, TQ), lambda i: (i, 0, 0), memory_space=pltpu.SMEM),
            pl.BlockSpec((TQ, d), lambda i: (i, 0)),
            pl.BlockSpec(memory_space=pl.ANY),
        ],
        out_specs=pl.BlockSpec(memory_space=pl.ANY),
        out_shape=jax.ShapeDtypeStruct((n_rows, d), F32),
        scratch_shapes=[pltpu.SemaphoreType.DMA(())],
        input_output_aliases={2: 0},
        compiler_params=_cparams("arbitrary"),
        name="moe_scatter",
    )(pos, h2, jnp.zeros((n_rows, d), F32))


def _expert_kernel(te_ref, nt_ref, x_ref, wg_ref, wu_ref, wd_ref, y_ref):
    del te_ref

    @pl.when(pl.program_id(0) < nt_ref[0])
    def _():
        x = x_ref[...].astype(BF16)
        g = _dot(x, wg_ref[0])
        u = _dot(x, wu_ref[0])
        a = (g * jax.nn.sigmoid(g)) * u
        y_ref[...] = _dot(a.astype(BF16), wd_ref[0])

    @pl.when(pl.program_id(0) >= nt_ref[0])
    def _():
        y_ref[...] = jnp.zeros(y_ref.shape, F32)


def _experts(tile_expert, n_used, xs, wg, wu, wd, n_tiles_max):
    _, d = xs.shape
    f = wg.shape[2]
    weight = lambda j, te, nt: (te[j], 0, 0)
    return pl.pallas_call(
        _expert_kernel,
        grid_spec=pltpu.PrefetchScalarGridSpec(
            num_scalar_prefetch=2,
            grid=(n_tiles_max,),
            in_specs=[
                pl.BlockSpec((MOE_TM, d), lambda j, te, nt: (jnp.minimum(j, nt[0] - 1), 0)),
                pl.BlockSpec((1, d, f), weight),
                pl.BlockSpec((1, d, f), weight),
                pl.BlockSpec((1, f, d), weight),
            ],
            out_specs=pl.BlockSpec((MOE_TM, d), lambda j, te, nt: (j, 0)),
        ),
        out_shape=jax.ShapeDtypeStruct(xs.shape, F32),
        compiler_params=_cparams("arbitrary"),
        name="moe_experts",
    )(tile_expert, n_used, xs, wg, wu, wd)


def _combine_kernel(pos_ref, pos_next_ref, ys_ref, x_ref, wcol_ref, mod_ref, lng_ref, lnb_ref, o_ref,
                    buf_ref, sem):
    i = pl.program_id(0)
    n_steps = pl.num_programs(0)
    slot = lax.rem(i, 2)

    def row_copy(p_ref, dst_slot, r, k):
        return pltpu.make_async_copy(ys_ref.at[pl.ds(p_ref[0, k, r], 1)],
                                     buf_ref.at[dst_slot, k, pl.ds(r, 1)], sem.at[dst_slot])

    def gather(p_ref, dst_slot):
        def start(r, carry):
            row_copy(p_ref, dst_slot, r, 0).start()
            row_copy(p_ref, dst_slot, r, 1).start()
            return carry
        lax.fori_loop(0, TQ, start, 0, unroll=8)

    @pl.when(i == 0)
    def _():
        gather(pos_ref, 0)

    @pl.when(i + 1 < n_steps)
    def _():
        gather(pos_next_ref, 1 - slot)

    def wait(r, carry):
        row_copy(pos_ref, slot, r, 0).wait()
        row_copy(pos_ref, slot, r, 1).wait()
        return carry
    lax.fori_loop(0, TQ, wait, 0, unroll=8)

    w = wcol_ref[...]
    moe = w[:, 0:1] * buf_ref[slot, 0] + w[:, 1:2] * buf_ref[slot, 1]
    g2 = mod_ref[0, 5:6, :]
    o_ref[...] = _layer_norm(ALPHA * x_ref[...] + g2 * moe, lng_ref[...], lnb_ref[...])


def _combine(pos, ys, x1, wcol, mods, layer, ln_g, ln_b):
    n, d = x1.shape
    n_steps = n // TQ
    row = lambda i: (i, 0)
    fixed = lambda i: (0, 0)
    return pl.pallas_call(
        _combine_kernel,
        grid=(n_steps,),
        in_specs=[
            pl.BlockSpec((1, 2, TQ), lambda i: (i, 0, 0), memory_space=pltpu.SMEM),
            pl.BlockSpec((1, 2, TQ), lambda i: (jnp.minimum(i + 1, n_steps - 1), 0, 0), memory_space=pltpu.SMEM),
            pl.BlockSpec(memory_space=pl.ANY),
            pl.BlockSpec((TQ, d), row),
            pl.BlockSpec((TQ, LANES), row),
            _mod_spec(layer),
            pl.BlockSpec((1, d), fixed),
            pl.BlockSpec((1, d), fixed),
        ],
        out_specs=pl.BlockSpec((TQ, d), row),
        out_shape=jax.ShapeDtypeStruct((n, d), F32),
        scratch_shapes=[pltpu.VMEM((2, 2, TQ, d), F32), pltpu.SemaphoreType.DMA((2,))],
        compiler_params=_cparams("arbitrary"),
        name="moe_combine",
    )(pos, pos, ys, x1, wcol, mods, ln_g, ln_b)


def _moe(h2, route, wcol, counts, wg, wu, wd, x1, mods, layer, ln_g, ln_b):
    n = h2.shape[0]
    pos, tile_expert, n_used, n_tiles_max = _moe_layout(route, counts, n)
    xs = _scatter(pos, h2, n_tiles_max * MOE_TM)
    ys = _experts(tile_expert, n_used, xs, wg, wu, wd, n_tiles_max)
    return _combine(pos, ys, x1, wcol, mods, layer, ln_g, ln_b)


def _rope_tables(n_lat, head_dim):
    quarter = head_dim // 4
    t = jnp.arange(n_lat)
    inv_freq = ROPE_THETA ** (-jnp.arange(quarter, dtype=F32) / quarter)
    ang_r = (t // GRID_W)[:, None].astype(F32) * inv_freq
    ang_c = (t % GRID_W)[:, None].astype(F32) * inv_freq
    cos = jnp.concatenate([jnp.cos(ang_r)] * 2 + [jnp.cos(ang_c)] * 2, axis=1)
    sin = jnp.concatenate([-jnp.sin(ang_r), jnp.sin(ang_r), -jnp.sin(ang_c), jnp.sin(ang_c)], axis=1)
    reps = LANES // head_dim
    cos = jnp.tile(cos, (1, reps))
    sin = jnp.tile(sin, (1, reps))
    cos = jnp.concatenate([jnp.ones((CTX_LEN, LANES), F32), cos], axis=0)
    sin = jnp.concatenate([jnp.zeros((CTX_LEN, LANES), F32), sin], axis=0)
    return cos, sin


def _a_head_order():
    grp = A_HQ // A_HKV
    order = []
    for j in range(grp):
        for g in range(A_HKV):
            order.append(g * grp + j)
    return order


def kernel(x, c, ctx, c_ctx, w_mod, b_mod, ln_g, ln_b, w_in_even, w_out_even, sink_logits, lam_q1, lam_k1,
           lam_q2, lam_k2, subln_g, w_in_odd, w_out_odd, q_norm_g, k_norm_g, w_router, b_router, w_gate,
           w_up, w_down):
    n_lat = x.shape[1]
    assert x.shape == (1, n_lat, D_MODEL) and ctx.shape == (1, CTX_LEN, D_MODEL)
    assert n_lat % max(UNROLL * TK, MOE_TM) == 0 and n_lat % GRID_W == 0
    xs = jnp.concatenate([ctx[0], x[0]], axis=0)

    c8 = jnp.zeros((8, D_MODEL), F32).at[0].set(c_ctx).at[1].set(c[0])
    mods = _modulation(c8, w_mod, b_mod)

    cos64, sin64 = _rope_tables(n_lat, A_HD)
    cos128, sin128 = _rope_tables(n_lat, C_HD)

    order = _a_head_order()
    a_cols = jnp.concatenate([jnp.arange(h * A_HD, (h + 1) * A_HD) for h in order])
    wq_a = A_HQ * A_HD
    w_in_even_p = jnp.concatenate([w_in_even[:, :, :wq_a][:, :, a_cols], w_in_even[:, :, wq_a:]], axis=2).astype(BF16)
    w_out_even_p = jnp.concatenate([w_out_even[:, :wq_a][:, a_cols], w_out_even[:, wq_a:]], axis=1).astype(BF16)
    w_in_odd_b = w_in_odd.astype(BF16)
    w_out_odd_b = w_out_odd.astype(BF16)
    wg = w_gate.astype(BF16)
    wu = w_up.astype(BF16)
    wd = w_down.astype(BF16)
    wr = jnp.pad(w_router, ((0, 0), (0, LANES - N_EXPERTS)))
    br = jnp.pad(b_router.astype(F32), (0, LANES - N_EXPERTS)).reshape(LANES, 1)

    segs_even = ((A_HQ * A_HD, True, None, None, False), (A_HKV * A_HD, True, None, None, False),
                 (A_HKV * A_HD, False, None, None, False),
                 (B_H * 2 * B_HD, True, None, (B_HD ** -0.5) * LOG2E, False),
                 (B_H * 2 * B_HD, True, None, None, False), (B_H * B_VD, False, None, None, True))
    segs_odd = ((C_HQ * C_HD, True, 0, (C_HD ** -0.5) * LOG2E, False), (C_HKV * C_HD, True, 1, None, False),
                (C_HKV * C_HD, False, None, None, True))

    for l in range(DEPTH):
        i = l // 2
        lng = ln_g[l]
        lnb = ln_b[l]
        if l % 2 == 0:
            lam_init = 0.8 - 0.6 * math.exp(-0.3 * l)
            aq, ak, av, bq, bk, bv = _inproj(xs, mods, l, w_in_even_p[i], cos64, sin64, segs_even, A_HD // 4)
            sink_rows = jnp.broadcast_to(
                (sink_logits[i][jnp.array(order)] * (A_HD ** 0.5))[:, None], (A_HQ, LANES)).astype(F32)
            oa = _attn_a(aq, ak, av, sink_rows)
            ob = _attn_b(bq, bk, bv, lam_q1[i][None], lam_k1[i][None], lam_q2[i][None], lam_k2[i][None],
                         subln_g[i][None], lam_init)
            half = wq_a
            routed = _post(oa, ob, 0, 0, w_out_even_p[i][:half], w_out_even_p[i][half:], xs, mods, l,
                           lng[0][None], lnb[0][None], wr, br)
        else:
            q, k, v = _inproj(xs, mods, l, w_in_odd_b[i], cos128, sin128, segs_odd, C_HD // 4,
                              gains=(q_norm_g[i][None], k_norm_g[i][None]))
            o = _attn_c(q, k, v)
            half = C_HQ * C_HD // 2
            routed = _post(o, o, 0, 1, w_out_odd_b[i][:half], w_out_odd_b[i][half:], xs, mods, l,
                           lng[0][None], lnb[0][None], wr, br)
        x1, h2, route, wcol, counts = routed
        xs = _moe(h2, route, wcol, counts, wg[l], wu[l], wd[l], x1, mods, l, lng[1][None], lnb[1][None])
    return xs[CTX_LEN:][None]
```

```python
import functools
import math

import jax
import jax.numpy as jnp
from jax import lax
from jax.experimental import pallas as pl
from jax.experimental.pallas import tpu as pltpu

D_MODEL = 1024
SEQ = 16384
DEPTH = 4
GRID_W = 64
CTX_LEN = 256
WINDOW = 128
ROPE_THETA = 10000.0
A_HQ, A_HKV, A_HD = 8, 2, 64
B_H, B_HD = 4, 64
B_VD = 2 * B_HD
C_HQ, C_HKV, C_HD = 8, 2, 128
N_EXPERTS = 16
N_GROUPS = 4
EXPERTS_PER_GROUP = N_EXPERTS // N_GROUPS
D_EXPERT = 512
ALPHA = (2 * DEPTH) ** 0.25
LN_EPS = 1e-5
QK_EPS = 1e-6
SUBLN_EPS = 1e-5

LANES = 128
TQ = CTX_LEN
TK = 512
UNROLL = 4
MOE_TM = 256
VMEM_LIMIT = 56 * 1024 * 1024
LOG2E = 1.4426950408889634
BF16 = jnp.bfloat16
F32 = jnp.float32


def _cparams(*sem):
    return pltpu.CompilerParams(dimension_semantics=sem, vmem_limit_bytes=VMEM_LIMIT)


def _dot(a, b):
    return jnp.dot(a, b, preferred_element_type=F32)


def _dot_nt(a, b):
    return lax.dot_general(a, b, (((1,), (1,)), ((), ())), preferred_element_type=F32)


def _lane_tile(x, width):
    reps = width // x.shape[-1]
    return x if reps == 1 else jnp.concatenate([x] * reps, axis=-1)


def _mod_kernel(c_ref, w_ref, b_ref, o_ref):
    c = c_ref[...]
    s = c * jax.nn.sigmoid(c)
    o_ref[0] = _dot(s.astype(BF16), w_ref[0].astype(BF16)) + b_ref[0]


def _modulation(c8, w_mod, b_mod):
    depth, d, n6 = w_mod.shape
    tn = 1536
    out = pl.pallas_call(
        _mod_kernel,
        grid=(depth, n6 // tn),
        in_specs=[
            pl.BlockSpec((8, d), lambda l, j: (0, 0)),
            pl.BlockSpec((1, d, tn), lambda l, j: (l, 0, j)),
            pl.BlockSpec((1, 1, tn), lambda l, j: (l, 0, j)),
        ],
        out_specs=pl.BlockSpec((1, 8, tn), lambda l, j: (l, 0, j)),
        out_shape=jax.ShapeDtypeStruct((depth, 8, n6), F32),
        compiler_params=_cparams("arbitrary", "arbitrary"),
        name="modulation",
    )(c8, w_mod, b_mod.reshape(depth, 1, n6))
    m = out[:, :2].reshape(depth, 2, 6, d)
    m = jnp.pad(m, ((0, 0), (0, 0), (0, 2), (0, 0)))
    return m.reshape(depth * 2, 8, d)


def _mod_spec(layer):
    return pl.BlockSpec((1, 8, D_MODEL), lambda i, *_: (2 * layer + jnp.minimum(i, 1), 0, 0))


def _rope_chunk(z, cos, sin, quarter):
    lane = lax.broadcasted_iota(jnp.int32, z.shape, 1)
    first = (lane % (2 * quarter)) < quarter
    nxt = pltpu.roll(z, LANES - quarter, axis=1)
    prv = pltpu.roll(z, quarter, axis=1)
    return z * cos + jnp.where(first, nxt, prv) * sin


def _inproj_kernel(segs, quarter, n_gain, x_ref, mod_ref, w_ref, cos_ref, sin_ref, *rest):
    gains = rest[:n_gain]
    outs = rest[n_gain:]
    sh = mod_ref[0, 0:1, :]
    sc = mod_ref[0, 1:2, :]
    hb = (x_ref[...] * (1.0 + sc) + sh).astype(BF16)
    cos = cos_ref[...]
    sin = sin_ref[...]
    off = 0
    for (width, rope, gain_idx, scale, with_ones), o_ref in zip(segs, outs):
        z = _dot(hb, w_ref[:, off:off + width])
        for c in range(width // LANES):
            zc = z[:, c * LANES:(c + 1) * LANES]
            if gain_idx is not None:
                ms = jnp.mean(zc * zc, axis=-1, keepdims=True)
                zc = zc * lax.rsqrt(ms + QK_EPS) * gains[gain_idx][...]
            if rope:
                zc = _rope_chunk(zc, cos, sin, quarter)
            if scale is not None:
                zc = zc * scale
            if with_ones:
                o_ref[:, 2 * c * LANES:(2 * c + 1) * LANES] = zc.astype(BF16)
                o_ref[:, (2 * c + 1) * LANES:(2 * c + 2) * LANES] = jnp.ones(zc.shape, BF16)
            else:
                o_ref[:, c * LANES:(c + 1) * LANES] = zc.astype(BF16)
        off += width


def _inproj(x, mods, layer, w, cos, sin, segs, quarter, gains=()):
    n, d = x.shape
    wtot = w.shape[1]
    row = lambda i: (i, 0)
    fixed = lambda i: (0, 0)
    in_specs = [
        pl.BlockSpec((TQ, d), row),
        _mod_spec(layer),
        pl.BlockSpec((d, wtot), fixed),
        pl.BlockSpec((TQ, LANES), row),
        pl.BlockSpec((TQ, LANES), row),
    ] + [pl.BlockSpec((1, LANES), fixed) for _ in gains]
    widths = [s[0] * (2 if s[4] else 1) for s in segs]
    return pl.pallas_call(
        functools.partial(_inproj_kernel, segs, quarter, len(gains)),
        grid=(n // TQ,),
        in_specs=in_specs,
        out_specs=[pl.BlockSpec((TQ, w_), row) for w_ in widths],
        out_shape=[jax.ShapeDtypeStruct((n, w_), BF16) for w_ in widths],
        compiler_params=_cparams("arbitrary"),
        name="inproj",
    )(x, mods, w, cos, sin, *gains)


def _flash(qs_ref, k_ref, v_ref, m_ref, acc_ref, s_refs, is_ctx_block):
    n_chunks = (k_ref.shape[0] - CTX_LEN) // TK
    m_ref[...] = jnp.full(m_ref.shape, -jnp.inf, F32)
    acc_ref[...] = jnp.zeros(acc_ref.shape, F32)

    def scores(start, size):
        return _dot_nt(qs_ref[...], k_ref[pl.ds(start, size), :])

    def update(s, start, size):
        m_prev = m_ref[...]
        m_new = jnp.maximum(m_prev, jnp.max(s, axis=1, keepdims=True))
        p = jnp.exp2(s - _lane_tile(m_new, size)).astype(BF16)
        alpha = jnp.exp2(m_prev - m_new)
        acc_ref[...] = acc_ref[...] * _lane_tile(alpha, 2 * LANES) + _dot(p, v_ref[pl.ds(start, size), :])
        m_ref[...] = m_new

    def lat(j):
        return pl.multiple_of(CTX_LEN + j * TK, CTX_LEN)

    update(scores(0, CTX_LEN), 0, CTX_LEN)

    @pl.when(jnp.logical_not(is_ctx_block))
    def _():
        s_refs[0][...] = scores(lat(0), TK)

        def trip(jj, carry):
            j = UNROLL * jj
            for u in range(UNROLL):
                s_refs[(u + 1) % 2][...] = scores(lat(jnp.minimum(j + u + 1, n_chunks - 1)), TK)
                update(s_refs[u % 2][...], lat(j + u), TK)
            return carry

        lax.fori_loop(0, n_chunks // UNROLL, trip, 0)


def _flash_out(acc_ref, idx):
    rows = acc_ref[idx * TQ:(idx + 1) * TQ, :]
    return rows[:, :LANES] / rows[:, LANES:]


def _flash_scratch(n_stack):
    rows = n_stack * TQ
    return [
        pltpu.VMEM((rows, LANES), BF16),
        pltpu.VMEM((rows, LANES), F32),
        pltpu.VMEM((rows, 2 * LANES), F32),
        pltpu.VMEM((rows, TK), F32),
        pltpu.VMEM((rows, TK), F32),
    ]


def _attn_c_kernel(q_ref, k_ref, v_ref, o_ref, qs_ref, m_ref, acc_ref, s0_ref, s1_ref):
    grp = C_HQ // C_HKV
    for h in range(grp):
        qs_ref[h * TQ:(h + 1) * TQ, :] = q_ref[:, h * C_HD:(h + 1) * C_HD]
    _flash(qs_ref, k_ref, v_ref, m_ref, acc_ref, (s0_ref, s1_ref), pl.program_id(1) == 0)
    for h in range(grp):
        o_ref[:, h * C_HD:(h + 1) * C_HD] = _flash_out(acc_ref, h).astype(BF16)


def _attn_c(q, k, v_aug):
    n = q.shape[0]
    grp = C_HQ // C_HKV
    wq = grp * C_HD
    return pl.pallas_call(
        _attn_c_kernel,
        grid=(C_HKV, n // TQ),
        in_specs=[
            pl.BlockSpec((TQ, wq), lambda g, i: (i, g)),
            pl.BlockSpec((n, C_HD), lambda g, i: (0, g)),
            pl.BlockSpec((n, 2 * C_HD), lambda g, i: (0, g)),
        ],
        out_specs=pl.BlockSpec((TQ, wq), lambda g, i: (i, g)),
        out_shape=jax.ShapeDtypeStruct((n, C_HQ * C_HD), BF16),
        scratch_shapes=_flash_scratch(grp),
        compiler_params=_cparams("arbitrary", "arbitrary"),
        name="attn_c",
    )(q, k, v_aug)


def _attn_b_kernel(lam_init, q_ref, k_ref, v_ref, lq1_ref, lk1_ref, lq2_ref, lk2_ref, g_ref, o_ref,
                   qs_ref, m_ref, acc_ref, s0_ref, s1_ref):
    q = q_ref[...]
    lane = lax.broadcasted_iota(jnp.int32, q.shape, 1)
    zero = jnp.zeros_like(q)
    qs_ref[0:TQ, :] = jnp.where(lane < B_HD, q, zero)
    qs_ref[TQ:2 * TQ, :] = jnp.where(lane >= B_HD, q, zero)
    _flash(qs_ref, k_ref, v_ref, m_ref, acc_ref, (s0_ref, s1_ref), pl.program_id(1) == 0)
    lam = (jnp.exp(jnp.sum(lq1_ref[...] * lk1_ref[...], axis=-1, keepdims=True))
           - jnp.exp(jnp.sum(lq2_ref[...] * lk2_ref[...], axis=-1, keepdims=True))) + lam_init
    o = _flash_out(acc_ref, 0) - lam * _flash_out(acc_ref, 1)
    ms = jnp.mean(o * o, axis=-1, keepdims=True)
    o = o * lax.rsqrt(ms + SUBLN_EPS) * g_ref[...] * (1.0 - lam_init)
    o_ref[...] = o.astype(BF16)


def _attn_b(q, k, v_aug, lq1, lk1, lq2, lk2, subln_g, lam_init):
    n = q.shape[0]
    blk = lambda h, i: (i, h)
    col = lambda h, i: (0, h)
    fixed = lambda h, i: (0, 0)
    small = pl.BlockSpec((1, B_HD), fixed)
    return pl.pallas_call(
        functools.partial(_attn_b_kernel, lam_init),
        grid=(B_H, n // TQ),
        in_specs=[
            pl.BlockSpec((TQ, B_VD), blk),
            pl.BlockSpec((n, B_VD), col),
            pl.BlockSpec((n, 2 * B_VD), col),
            small, small, small, small,
            pl.BlockSpec((1, B_VD), fixed),
        ],
        out_specs=pl.BlockSpec((TQ, B_VD), blk),
        out_shape=jax.ShapeDtypeStruct((n, B_H * B_VD), BF16),
        scratch_shapes=_flash_scratch(2),
        compiler_params=_cparams("arbitrary", "arbitrary"),
        name="attn_b",
    )(q, k, v_aug, lq1, lk1, lq2, lk2, subln_g)


A_WIN_KEYS = TQ + 2 * WINDOW


def _attn_a_kernel(q_ref, k_ref, v_ref, sink_ref, o_ref):
    i = pl.program_id(0)
    c = (A_HD ** -0.5) * LOG2E
    n_lat = k_ref.shape[0] - CTX_LEN
    kc = k_ref[0:CTX_LEN, :]
    vc = v_ref[0:CTX_LEN, :]
    n_blk = A_HQ * A_HD // LANES

    def heads(local):
        for j in range(n_blk):
            q = q_ref[:, j * LANES:(j + 1) * LANES]
            lane = lax.broadcasted_iota(jnp.int32, q.shape, 1)
            zero = jnp.zeros_like(q)
            halves = []
            for half in range(2):
                qh = jnp.where(lane >= A_HD if half else lane < A_HD, q, zero)
                sink = sink_ref[2 * j + half:2 * j + half + 1, 0:1]
                s_c = _dot_nt(qh, kc)
                m = jnp.maximum(jnp.max(s_c, axis=1, keepdims=True), sink)
                if local is not None:
                    kl, vl, ok = local
                    s_l = jnp.where(ok, _dot_nt(qh, kl), -jnp.inf)
                    m = jnp.maximum(m, jnp.max(s_l, axis=1, keepdims=True))
                p_c = jnp.exp2((s_c - m) * c)
                l = jnp.exp2((sink - m) * c) + jnp.sum(p_c, axis=1, keepdims=True)
                acc = _dot(p_c.astype(BF16), vc)
                if local is not None:
                    p_l = jnp.exp2((s_l - m) * c)
                    l = l + jnp.sum(p_l, axis=1, keepdims=True)
                    acc = acc + _dot(p_l.astype(BF16), vl)
                halves.append(acc / l)
            o_ref[:, j * LANES:(j + 1) * LANES] = jnp.where(lane < A_HD, halves[0], halves[1]).astype(BF16)

    @pl.when(i == 0)
    def _():
        heads(None)

    @pl.when(i > 0)
    def _():
        q0 = (i - 1) * TQ
        ws = jnp.clip(q0 - WINDOW, 0, n_lat - A_WIN_KEYS)
        ws = pl.multiple_of(ws, WINDOW)
        kl = k_ref[pl.ds(CTX_LEN + ws, A_WIN_KEYS), :]
        vl = v_ref[pl.ds(CTX_LEN + ws, A_WIN_KEYS), :]
        qpos = q0 + lax.broadcasted_iota(jnp.int32, (TQ, A_WIN_KEYS), 0)
        kpos = ws + lax.broadcasted_iota(jnp.int32, (TQ, A_WIN_KEYS), 1)
        ok = jnp.abs(qpos - kpos) <= WINDOW
        heads((kl, vl, ok))


def _attn_a(q, k, v, sink_rows):
    n = q.shape[0]
    wq = A_HQ * A_HD
    wkv = A_HKV * A_HD
    return pl.pallas_call(
        _attn_a_kernel,
        grid=(n // TQ,),
        in_specs=[
            pl.BlockSpec((TQ, wq), lambda i: (i, 0)),
            pl.BlockSpec((n, wkv), lambda i: (0, 0)),
            pl.BlockSpec((n, wkv), lambda i: (0, 0)),
            pl.BlockSpec((A_HQ, LANES), lambda i: (0, 0)),
        ],
        out_specs=pl.BlockSpec((TQ, wq), lambda i: (i, 0)),
        out_shape=jax.ShapeDtypeStruct((n, wq), BF16),
        compiler_params=_cparams("arbitrary"),
        name="attn_a",
    )(q, k, v, sink_rows)


def _layer_norm(z, g, b):
    mu = jnp.mean(z, axis=-1, keepdims=True)
    zc = z - mu
    var = jnp.mean(zc * zc, axis=-1, keepdims=True)
    return zc * lax.rsqrt(var + LN_EPS) * g + b


def _route(logits_t, bias_t):
    t = logits_t.shape[1]
    score = [jax.nn.sigmoid(logits_t[e:e + 1, :]) for e in range(N_EXPERTS)]
    sel = [score[e] + bias_t[e:e + 1, :] for e in range(N_EXPERTS)]
    best_g = jnp.zeros((1, t), jnp.int32)
    best_s = None
    for g in range(N_GROUPS):
        a = sel[g * EXPERTS_PER_GROUP:(g + 1) * EXPERTS_PER_GROUP]
        gs = None
        for x in range(EXPERTS_PER_GROUP):
            for y in range(x + 1, EXPERTS_PER_GROUP):
                pair = a[x] + a[y]
                gs = pair if gs is None else jnp.maximum(gs, pair)
        if best_s is None:
            best_s = gs
        else:
            upd = gs > best_s
            best_g = jnp.where(upd, g, best_g)
            best_s = jnp.where(upd, gs, best_s)
    neg = jnp.full((1, t), -jnp.inf, F32)
    masked = [jnp.where(best_g == (e // EXPERTS_PER_GROUP), sel[e], neg) for e in range(N_EXPERTS)]
    i1 = jnp.zeros((1, t), jnp.int32)
    v1 = masked[0]
    for e in range(1, N_EXPERTS):
        upd = masked[e] > v1
        i1 = jnp.where(upd, e, i1)
        v1 = jnp.where(upd, masked[e], v1)
    i2 = jnp.full((1, t), -1, jnp.int32)
    v2 = neg
    for e in range(N_EXPERTS):
        upd = (masked[e] > v2) & (i1 != e)
        i2 = jnp.where(upd, e, i2)
        v2 = jnp.where(upd, masked[e], v2)
    zero = jnp.zeros((1, t), F32)
    w1 = zero
    w2 = zero
    for e in range(N_EXPERTS):
        w1 = w1 + jnp.where(i1 == e, score[e], zero)
        w2 = w2 + jnp.where(i2 == e, score[e], zero)
    den = w1 + w2
    return i1, i2, w1 / den, w2 / den


def _post_kernel(o1_ref, o2_ref, w1_ref, w2_ref, x_ref, mod_ref, lng_ref, lnb_ref, wr_ref, br_ref,
                 x1_ref, h2_ref, route_ref, wcol_ref, cnt_out_ref, cnt_ref):
    @pl.when(pl.program_id(0) == 0)
    def _():
        cnt_ref[...] = jnp.zeros(cnt_ref.shape, F32)

    g1 = mod_ref[0, 2:3, :]
    sh2 = mod_ref[0, 3:4, :]
    sc2 = mod_ref[0, 4:5, :]
    y = _dot(o1_ref[...], w1_ref[...]) + _dot(o2_ref[...], w2_ref[...])
    x1 = _layer_norm(ALPHA * x_ref[...] + g1 * y, lng_ref[...], lnb_ref[...])
    x1_ref[...] = x1
    h2 = x1 * (1.0 + sc2) + sh2
    h2_ref[...] = h2
    logits = jnp.dot(h2, wr_ref[...], preferred_element_type=F32, precision=lax.Precision.HIGHEST)
    i1, i2, w1, w2 = _route(logits.T, br_ref[...])

    row = lax.broadcasted_iota(jnp.int32, (LANES, TQ), 0)
    hit1 = row == i1
    hit2 = row == i2
    assigned = jnp.where(hit1 | hit2, 1.0, 0.0)
    t_from = lax.broadcasted_iota(jnp.int32, (TQ, TQ), 0)
    t_to = lax.broadcasted_iota(jnp.int32, (TQ, TQ), 1)
    earlier = jnp.where(t_from < t_to, 1.0, 0.0).astype(BF16)
    rank = cnt_ref[:, 0:1] + _dot(assigned.astype(BF16), earlier)
    r1 = jnp.sum(jnp.where(hit1, rank, 0.0), axis=0, keepdims=True).astype(jnp.int32)
    r2 = jnp.sum(jnp.where(hit2, rank, 0.0), axis=0, keepdims=True).astype(jnp.int32)
    cnt_ref[...] = cnt_ref[...] + jnp.sum(assigned, axis=1, keepdims=True)
    cnt_out_ref[...] = cnt_ref[...]

    row8 = lax.broadcasted_iota(jnp.int32, (8, TQ), 0)
    route_ref[0] = jnp.where(row8 == 0, i1, jnp.where(row8 == 1, i2, jnp.where(row8 == 2, r1,
                             jnp.where(row8 == 3, r2, 0))))
    wcol_ref[...] = jnp.where(row == 0, w1, jnp.where(row == 1, w2, 0.0)).T


def _post(o1, o2, o1_col, o2_col, w1, w2, x, mods, layer, ln_g, ln_b, wr, br):
    n, d = x.shape
    half = w1.shape[0]
    row = lambda i: (i, 0)
    fixed = lambda i: (0, 0)
    return pl.pallas_call(
        _post_kernel,
        grid=(n // TQ,),
        in_specs=[
            pl.BlockSpec((TQ, half), lambda i: (i, o1_col)),
            pl.BlockSpec((TQ, half), lambda i: (i, o2_col)),
            pl.BlockSpec((half, d), fixed),
            pl.BlockSpec((half, d), fixed),
            pl.BlockSpec((TQ, d), row),
            _mod_spec(layer),
            pl.BlockSpec((1, d), fixed),
            pl.BlockSpec((1, d), fixed),
            pl.BlockSpec((d, LANES), fixed),
            pl.BlockSpec((LANES, 1), fixed),
        ],
        out_specs=[
            pl.BlockSpec((TQ, d), row),
            pl.BlockSpec((TQ, d), row),
            pl.BlockSpec((1, 8, TQ), lambda i: (i, 0, 0)),
            pl.BlockSpec((TQ, LANES), row),
            pl.BlockSpec((LANES, LANES), fixed),
        ],
        out_shape=[
            jax.ShapeDtypeStruct((n, d), F32),
            jax.ShapeDtypeStruct((n, d), F32),
            jax.ShapeDtypeStruct((n // TQ, 8, TQ), jnp.int32),
            jax.ShapeDtypeStruct((n, LANES), F32),
            jax.ShapeDtypeStruct((LANES, LANES), F32),
        ],
        scratch_shapes=[pltpu.VMEM((LANES, LANES), F32)],
        compiler_params=_cparams("arbitrary"),
        name="post_attn",
    )(o1, o2, w1, w2, x, mods, ln_g, ln_b, wr, br)


def _moe_layout(route, counts, n):
    n_tiles_max = (2 * n) // MOE_TM + N_EXPERTS
    cnt = counts[:N_EXPERTS, 0].astype(jnp.int32)
    tiles = (cnt + MOE_TM - 1) // MOE_TM
    tiles_end = jnp.cumsum(tiles)
    seg_start = (tiles_end - tiles) * MOE_TM
    experts = jnp.arange(N_EXPERTS)

    def start_of(e):
        return jnp.sum(jnp.where(e[..., None] == experts, seg_start, 0), axis=-1)

    pos = jnp.stack([start_of(route[:, 0]) + route[:, 2],
                     start_of(route[:, 1]) + route[:, 3]], axis=1)
    n_used = tiles_end[-1]
    j = jnp.minimum(jnp.arange(n_tiles_max), n_used - 1)
    tile_expert = jnp.sum(j[:, None] >= tiles_end[None, :], axis=1).astype(jnp.int32)
    last_tiles = jnp.maximum(tiles_end - 1, 0)
    tail_tiles = jnp.minimum(n_used + experts, n_tiles_max - 1)
    live = jnp.concatenate([tiles > 0, n_used + experts < n_tiles_max])
    zero_tiles = jnp.concatenate([last_tiles, tail_tiles, live]).astype(jnp.int32)
    return pos.astype(jnp.int32), tile_expert, n_used.reshape(1).astype(jnp.int32), zero_tiles, n_tiles_max


def _scatter_kernel(zt_ref, pos_ref, h_ref, xs_ref, zero_ref, sem, zsem):
    def zero_copy(t):
        return pltpu.make_async_copy(zero_ref, xs_ref.at[pl.ds(zt_ref[t] * MOE_TM, MOE_TM)], zsem)

    @pl.when(pl.program_id(0) == 0)
    def _():
        zero_ref[...] = jnp.zeros(zero_ref.shape, F32)
        for t in range(2 * N_EXPERTS):
            @pl.when(zt_ref[2 * N_EXPERTS + t] != 0)
            def _():
                zero_copy(t).start()
        for t in range(2 * N_EXPERTS):
            @pl.when(zt_ref[2 * N_EXPERTS + t] != 0)
            def _():
                zero_copy(t).wait()

    def row_copy(r, k):
        return pltpu.make_async_copy(h_ref.at[pl.ds(r, 1)], xs_ref.at[pl.ds(pos_ref[0, k, r], 1)], sem)

    def start(r, carry):
        row_copy(r, 0).start()
        row_copy(r, 1).start()
        return carry

    def wait(r, carry):
        row_copy(r, 0).wait()
        row_copy(r, 1).wait()
        return carry

    lax.fori_loop(0, TQ, start, 0, unroll=8)
    lax.fori_loop(0, TQ, wait, 0, unroll=8)


def _scatter(zero_tiles, pos, h2, n_rows):
    n, d = h2.shape
    return pl.pallas_call(
        _scatter_kernel,
        grid_spec=pltpu.PrefetchScalarGridSpec(
            num_scalar_prefetch=1,
            grid=(n // TQ,),
            in_specs=[
                pl.BlockSpec((1, 2, TQ), lambda i, zt: (i, 0, 0), memory_space=pltpu.SMEM),
                pl.BlockSpec((TQ, d), lambda i, zt: (i, 0)),
            ],
            out_specs=pl.BlockSpec(memory_space=pl.ANY),
            scratch_shapes=[pltpu.VMEM((MOE_TM, d), F32), pltpu.SemaphoreType.DMA(()),
                            pltpu.SemaphoreType.DMA(())],
        ),
        out_shape=jax.ShapeDtypeStruct((n_rows, d), F32),
        compiler_params=_cparams("arbitrary"),
        name="moe_scatter",
    )(zero_tiles, pos, h2)


def _expert_kernel(te_ref, nt_ref, x_ref, wg_ref, wu_ref, wd_ref, y_ref, wgb_ref, wub_ref, wdb_ref):
    j = pl.program_id(0)
    used = j < nt_ref[0]
    new_expert = jnp.logical_or(j == 0, te_ref[j] != te_ref[jnp.maximum(j - 1, 0)])

    @pl.when(jnp.logical_and(used, new_expert))
    def _():
        wgb_ref[...] = wg_ref[0].astype(BF16)
        wub_ref[...] = wu_ref[0].astype(BF16)
        wdb_ref[...] = wd_ref[0].astype(BF16)

    @pl.when(used)
    def _():
        x = x_ref[...].astype(BF16)
        g = _dot(x, wgb_ref[...])
        u = _dot(x, wub_ref[...])
        a = (g * jax.nn.sigmoid(g)) * u
        y_ref[...] = _dot(a.astype(BF16), wdb_ref[...])

    @pl.when(jnp.logical_not(used))
    def _():
        y_ref[...] = jnp.zeros(y_ref.shape, F32)


def _experts(tile_expert, n_used, xs, wg, wu, wd, n_tiles_max):
    _, d = xs.shape
    f = wg.shape[2]
    weight = lambda j, te, nt: (te[j], 0, 0)
    return pl.pallas_call(
        _expert_kernel,
        grid_spec=pltpu.PrefetchScalarGridSpec(
            num_scalar_prefetch=2,
            grid=(n_tiles_max,),
            in_specs=[
                pl.BlockSpec((MOE_TM, d), lambda j, te, nt: (jnp.minimum(j, nt[0] - 1), 0)),
                pl.BlockSpec((1, d, f), weight),
                pl.BlockSpec((1, d, f), weight),
                pl.BlockSpec((1, f, d), weight),
            ],
            out_specs=pl.BlockSpec((MOE_TM, d), lambda j, te, nt: (j, 0)),
            scratch_shapes=[pltpu.VMEM((d, f), BF16), pltpu.VMEM((d, f), BF16), pltpu.VMEM((f, d), BF16)],
        ),
        out_shape=jax.ShapeDtypeStruct(xs.shape, F32),
        compiler_params=_cparams("arbitrary"),
        name="moe_experts",
    )(tile_expert, n_used, xs, wg, wu, wd)


def _combine_kernel(pos_ref, pos_next_ref, ys_ref, x_ref, wcol_ref, mod_ref, lng_ref, lnb_ref, o_ref,
                    buf_ref, sem):
    i = pl.program_id(0)
    n_steps = pl.num_programs(0)
    slot = lax.rem(i, 2)

    def row_copy(p_ref, dst_slot, r, k):
        return pltpu.make_async_copy(ys_ref.at[pl.ds(p_ref[0, k, r], 1)],
                                     buf_ref.at[dst_slot, k, pl.ds(r, 1)], sem.at[dst_slot])

    def gather(p_ref, dst_slot):
        def start(r, carry):
            row_copy(p_ref, dst_slot, r, 0).start()
            row_copy(p_ref, dst_slot, r, 1).start()
            return carry
        lax.fori_loop(0, TQ, start, 0, unroll=8)

    @pl.when(i == 0)
    def _():
        gather(pos_ref, 0)

    @pl.when(i + 1 < n_steps)
    def _():
        gather(pos_next_ref, 1 - slot)

    def wait(r, carry):
        row_copy(pos_ref, slot, r, 0).wait()
        row_copy(pos_ref, slot, r, 1).wait()
        return carry
    lax.fori_loop(0, TQ, wait, 0, unroll=8)

    w = wcol_ref[...]
    moe = w[:, 0:1] * buf_ref[slot, 0] + w[:, 1:2] * buf_ref[slot, 1]
    g2 = mod_ref[0, 5:6, :]
    o_ref[...] = _layer_norm(ALPHA * x_ref[...] + g2 * moe, lng_ref[...], lnb_ref[...])


def _combine(pos, ys, x1, wcol, mods, layer, ln_g, ln_b, latents_only):
    n, d = x1.shape
    n_steps = n // TQ
    row = lambda i: (i, 0)
    fixed = lambda i: (0, 0)
    if latents_only:
        out_rows, out_map = n - CTX_LEN, lambda i: (jnp.maximum(i - CTX_LEN // TQ, 0), 0)
    else:
        out_rows, out_map = n, row
    return pl.pallas_call(
        _combine_kernel,
        grid=(n_steps,),
        in_specs=[
            pl.BlockSpec((1, 2, TQ), lambda i: (i, 0, 0), memory_space=pltpu.SMEM),
            pl.BlockSpec((1, 2, TQ), lambda i: (jnp.minimum(i + 1, n_steps - 1), 0, 0), memory_space=pltpu.SMEM),
            pl.BlockSpec(memory_space=pl.ANY),
            pl.BlockSpec((TQ, d), row),
            pl.BlockSpec((TQ, LANES), row),
            _mod_spec(layer),
            pl.BlockSpec((1, d), fixed),
            pl.BlockSpec((1, d), fixed),
        ],
        out_specs=pl.BlockSpec((TQ, d), out_map),
        out_shape=jax.ShapeDtypeStruct((out_rows, d), F32),
        scratch_shapes=[pltpu.VMEM((2, 2, TQ, d), F32), pltpu.SemaphoreType.DMA((2,))],
        compiler_params=_cparams("arbitrary"),
        name="moe_combine",
    )(pos, pos, ys, x1, wcol, mods, ln_g, ln_b)


def _moe(h2, route, wcol, counts, wg, wu, wd, x1, mods, layer, ln_g, ln_b, latents_only):
    n = h2.shape[0]
    pos, tile_expert, n_used, zero_tiles, n_tiles_max = _moe_layout(route, counts, n)
    xs = _scatter(zero_tiles, pos, h2, n_tiles_max * MOE_TM)
    ys = _experts(tile_expert, n_used, xs, wg, wu, wd, n_tiles_max)
    return _combine(pos, ys, x1, wcol, mods, layer, ln_g, ln_b, latents_only)


def _rope_tables(n_lat, head_dim):
    quarter = head_dim // 4
    t = jnp.arange(n_lat)
    inv_freq = ROPE_THETA ** (-jnp.arange(quarter, dtype=F32) / quarter)
    ang_r = (t // GRID_W)[:, None].astype(F32) * inv_freq
    ang_c = (t % GRID_W)[:, None].astype(F32) * inv_freq
    cos = jnp.concatenate([jnp.cos(ang_r)] * 2 + [jnp.cos(ang_c)] * 2, axis=1)
    sin = jnp.concatenate([-jnp.sin(ang_r), jnp.sin(ang_r), -jnp.sin(ang_c), jnp.sin(ang_c)], axis=1)
    reps = LANES // head_dim
    cos = jnp.tile(cos, (1, reps))
    sin = jnp.tile(sin, (1, reps))
    cos = jnp.concatenate([jnp.ones((CTX_LEN, LANES), F32), cos], axis=0)
    sin = jnp.concatenate([jnp.zeros((CTX_LEN, LANES), F32), sin], axis=0)
    return cos, sin


def _a_head_order():
    grp = A_HQ // A_HKV
    order = []
    for j in range(grp):
        for g in range(A_HKV):
            order.append(g * grp + j)
    return order


def kernel(x, c, ctx, c_ctx, w_mod, b_mod, ln_g, ln_b, w_in_even, w_out_even, sink_logits, lam_q1, lam_k1,
           lam_q2, lam_k2, subln_g, w_in_odd, w_out_odd, q_norm_g, k_norm_g, w_router, b_router, w_gate,
           w_up, w_down):
    n_lat = x.shape[1]
    assert x.shape == (1, n_lat, D_MODEL) and ctx.shape == (1, CTX_LEN, D_MODEL)
    assert n_lat % max(UNROLL * TK, MOE_TM) == 0 and n_lat % GRID_W == 0
    xs = jnp.concatenate([ctx[0], x[0]], axis=0)

    c8 = jnp.zeros((8, D_MODEL), F32).at[0].set(c_ctx).at[1].set(c[0])
    mods = _modulation(c8, w_mod, b_mod)

    cos64, sin64 = _rope_tables(n_lat, A_HD)
    cos128, sin128 = _rope_tables(n_lat, C_HD)

    order = _a_head_order()
    a_cols = jnp.concatenate([jnp.arange(h * A_HD, (h + 1) * A_HD) for h in order])
    wq_a = A_HQ * A_HD
    w_in_even_p = jnp.concatenate([w_in_even[:, :, :wq_a][:, :, a_cols], w_in_even[:, :, wq_a:]], axis=2).astype(BF16)
    w_out_even_p = jnp.concatenate([w_out_even[:, :wq_a][:, a_cols], w_out_even[:, wq_a:]], axis=1).astype(BF16)
    w_in_odd_b = w_in_odd.astype(BF16)
    w_out_odd_b = w_out_odd.astype(BF16)
    wg, wu, wd = w_gate, w_up, w_down
    wr = jnp.pad(w_router, ((0, 0), (0, LANES - N_EXPERTS)))
    br = jnp.pad(b_router.astype(F32), (0, LANES - N_EXPERTS)).reshape(LANES, 1)

    segs_even = ((A_HQ * A_HD, True, None, None, False), (A_HKV * A_HD, True, None, None, False),
                 (A_HKV * A_HD, False, None, None, False),
                 (B_H * 2 * B_HD, True, None, (B_HD ** -0.5) * LOG2E, False),
                 (B_H * 2 * B_HD, True, None, None, False), (B_H * B_VD, False, None, None, True))
    segs_odd = ((C_HQ * C_HD, True, 0, (C_HD ** -0.5) * LOG2E, False), (C_HKV * C_HD, True, 1, None, False),
                (C_HKV * C_HD, False, None, None, True))

    for l in range(DEPTH):
        i = l // 2
        lng = ln_g[l]
        lnb = ln_b[l]
        if l % 2 == 0:
            lam_init = 0.8 - 0.6 * math.exp(-0.3 * l)
            aq, ak, av, bq, bk, bv = _inproj(xs, mods, l, w_in_even_p[i], cos64, sin64, segs_even, A_HD // 4)
            sink_rows = jnp.broadcast_to(
                (sink_logits[i][jnp.array(order)] * (A_HD ** 0.5))[:, None], (A_HQ, LANES)).astype(F32)
            oa = _attn_a(aq, ak, av, sink_rows)
            ob = _attn_b(bq, bk, bv, lam_q1[i][None], lam_k1[i][None], lam_q2[i][None], lam_k2[i][None],
                         subln_g[i][None], lam_init)
            half = wq_a
            routed = _post(oa, ob, 0, 0, w_out_even_p[i][:half], w_out_even_p[i][half:], xs, mods, l,
                           lng[0][None], lnb[0][None], wr, br)
        else:
            q, k, v = _inproj(xs, mods, l, w_in_odd_b[i], cos128, sin128, segs_odd, C_HD // 4,
                              gains=(q_norm_g[i][None], k_norm_g[i][None]))
            o = _attn_c(q, k, v)
            half = C_HQ * C_HD // 2
            routed = _post(o, o, 0, 1, w_out_odd_b[i][:half], w_out_odd_b[i][half:], xs, mods, l,
                           lng[0][None], lnb[0][None], wr, br)
        x1, h2, route, wcol, counts = routed
        xs = _moe(h2, route, wcol, counts, wg[l], wu[l], wd[l], x1, mods, l, lng[1][None], lnb[1][None],
                  latents_only=(l == DEPTH - 1))
    return xs[None]
```

```python
import functools
import math

import jax
import jax.numpy as jnp
from jax import lax
from jax.experimental import pallas as pl
from jax.experimental.pallas import tpu as pltpu

D_MODEL = 1024
SEQ = 16384
DEPTH = 4
GRID_W = 64
CTX_LEN = 256
WINDOW = 128
ROPE_THETA = 10000.0
A_HQ, A_HKV, A_HD = 8, 2, 64
B_H, B_HD = 4, 64
B_VD = 2 * B_HD
C_HQ, C_HKV, C_HD = 8, 2, 128
N_EXPERTS = 16
N_GROUPS = 4
EXPERTS_PER_GROUP = N_EXPERTS // N_GROUPS
D_EXPERT = 512
ALPHA = (2 * DEPTH) ** 0.25
LN_EPS = 1e-5
QK_EPS = 1e-6
SUBLN_EPS = 1e-5

LANES = 128
TQ = CTX_LEN
TK = 512
UNROLL = 4
MOE_TM = 256
VMEM_LIMIT = 56 * 1024 * 1024
LOG2E = 1.4426950408889634
BF16 = jnp.bfloat16
F32 = jnp.float32


def _cparams(*sem):
    return pltpu.CompilerParams(dimension_semantics=sem, vmem_limit_bytes=VMEM_LIMIT)


def _dot(a, b):
    return jnp.dot(a, b, preferred_element_type=F32)


def _dot_nt(a, b):
    return lax.dot_general(a, b, (((1,), (1,)), ((), ())), preferred_element_type=F32)


def _lane_tile(x, width):
    reps = width // x.shape[-1]
    return x if reps == 1 else jnp.concatenate([x] * reps, axis=-1)


def _mod_kernel(c_ref, w_ref, b_ref, o_ref):
    c = c_ref[...]
    s = c * jax.nn.sigmoid(c)
    o_ref[0] = _dot(s.astype(BF16), w_ref[0].astype(BF16)) + b_ref[0]


def _modulation(c8, w_mod, b_mod):
    depth, d, n6 = w_mod.shape
    tn = 1536
    out = pl.pallas_call(
        _mod_kernel,
        grid=(depth, n6 // tn),
        in_specs=[
            pl.BlockSpec((8, d), lambda l, j: (0, 0)),
            pl.BlockSpec((1, d, tn), lambda l, j: (l, 0, j)),
            pl.BlockSpec((1, 1, tn), lambda l, j: (l, 0, j)),
        ],
        out_specs=pl.BlockSpec((1, 8, tn), lambda l, j: (l, 0, j)),
        out_shape=jax.ShapeDtypeStruct((depth, 8, n6), F32),
        compiler_params=_cparams("arbitrary", "arbitrary"),
        name="modulation",
    )(c8, w_mod, b_mod.reshape(depth, 1, n6))
    m = out[:, :2].reshape(depth, 2, 6, d)
    m = jnp.pad(m, ((0, 0), (0, 0), (0, 2), (0, 0)))
    return m.reshape(depth * 2, 8, d)


def _mod_spec(layer):
    return pl.BlockSpec((1, 8, D_MODEL), lambda i, *_: (2 * layer + jnp.minimum(i, 1), 0, 0))


def _rope_chunk(z, cos, sin, quarter):
    lane = lax.broadcasted_iota(jnp.int32, z.shape, 1)
    first = (lane % (2 * quarter)) < quarter
    nxt = pltpu.roll(z, LANES - quarter, axis=1)
    prv = pltpu.roll(z, quarter, axis=1)
    return z * cos + jnp.where(first, nxt, prv) * sin


def _inproj_kernel(segs, quarter, n_gain, x_ref, mod_ref, w_ref, cos_ref, sin_ref, *rest):
    gains = rest[:n_gain]
    outs = rest[n_gain:]
    sh = mod_ref[0, 0:1, :]
    sc = mod_ref[0, 1:2, :]
    hb = (x_ref[...] * (1.0 + sc) + sh).astype(BF16)
    cos = cos_ref[...]
    sin = sin_ref[...]
    off = 0
    for (width, rope, gain_idx, scale, with_ones), o_ref in zip(segs, outs):
        z = _dot(hb, w_ref[:, off:off + width])
        for c in range(width // LANES):
            zc = z[:, c * LANES:(c + 1) * LANES]
            if gain_idx is not None:
                ms = jnp.mean(zc * zc, axis=-1, keepdims=True)
                zc = zc * lax.rsqrt(ms + QK_EPS) * gains[gain_idx][...]
            if rope:
                zc = _rope_chunk(zc, cos, sin, quarter)
            if scale is not None:
                zc = zc * scale
            if with_ones:
                o_ref[:, 2 * c * LANES:(2 * c + 1) * LANES] = zc.astype(BF16)
                o_ref[:, (2 * c + 1) * LANES:(2 * c + 2) * LANES] = jnp.ones(zc.shape, BF16)
            else:
                o_ref[:, c * LANES:(c + 1) * LANES] = zc.astype(BF16)
        off += width


def _inproj(x, mods, layer, w, cos, sin, segs, quarter, gains=()):
    n, d = x.shape
    wtot = w.shape[1]
    row = lambda i: (i, 0)
    fixed = lambda i: (0, 0)
    in_specs = [
        pl.BlockSpec((TQ, d), row),
        _mod_spec(layer),
        pl.BlockSpec((d, wtot), fixed),
        pl.BlockSpec((TQ, LANES), row),
        pl.BlockSpec((TQ, LANES), row),
    ] + [pl.BlockSpec((1, LANES), fixed) for _ in gains]
    widths = [s[0] * (2 if s[4] else 1) for s in segs]
    return pl.pallas_call(
        functools.partial(_inproj_kernel, segs, quarter, len(gains)),
        grid=(n // TQ,),
        in_specs=in_specs,
        out_specs=[pl.BlockSpec((TQ, w_), row) for w_ in widths],
        out_shape=[jax.ShapeDtypeStruct((n, w_), BF16) for w_ in widths],
        compiler_params=_cparams("arbitrary"),
        name="inproj",
    )(x, mods, w, cos, sin, *gains)


def _flash(qs_ref, k_ref, v_ref, m_ref, acc_ref, s_refs, is_ctx_block):
    n_chunks = (k_ref.shape[0] - CTX_LEN) // TK
    n_sets = k_ref.shape[1] // LANES
    rows = qs_ref.shape[0] // n_sets
    m_ref[...] = jnp.full(m_ref.shape, -jnp.inf, F32)
    acc_ref[...] = jnp.zeros(acc_ref.shape, F32)

    def scores(s_ref, start, size):
        for g in range(n_sets):
            rs = slice(g * rows, (g + 1) * rows)
            s_ref[rs, 0:size] = _dot_nt(qs_ref[rs, :], k_ref[pl.ds(start, size), g * LANES:(g + 1) * LANES])

    def update(s_ref, start, size):
        s = s_ref[:, 0:size]
        m_prev = m_ref[...]
        m_new = jnp.maximum(m_prev, jnp.max(s, axis=1, keepdims=True))
        p = jnp.exp2(s - _lane_tile(m_new, size)).astype(BF16)
        alpha = _lane_tile(jnp.exp2(m_prev - m_new), 2 * LANES)
        for g in range(n_sets):
            rs = slice(g * rows, (g + 1) * rows)
            pv = _dot(p[rs, :], v_ref[pl.ds(start, size), 2 * g * LANES:2 * (g + 1) * LANES])
            acc_ref[rs, :] = acc_ref[rs, :] * alpha[rs, :] + pv
        m_ref[...] = m_new

    def lat(j):
        return pl.multiple_of(CTX_LEN + j * TK, CTX_LEN)

    scores(s_refs[1], 0, CTX_LEN)
    update(s_refs[1], 0, CTX_LEN)

    @pl.when(jnp.logical_not(is_ctx_block))
    def _():
        scores(s_refs[0], lat(0), TK)

        def trip(jj, carry):
            j = UNROLL * jj
            for u in range(UNROLL):
                scores(s_refs[(u + 1) % 2], lat(jnp.minimum(j + u + 1, n_chunks - 1)), TK)
                update(s_refs[u % 2], lat(j + u), TK)
            return carry

        lax.fori_loop(0, n_chunks // UNROLL, trip, 0)


def _flash_out(acc_ref, idx):
    rows = acc_ref[idx * TQ:(idx + 1) * TQ, :]
    return rows[:, :LANES] / rows[:, LANES:]


def _flash_scratch(n_stack):
    rows = n_stack * TQ
    return [
        pltpu.VMEM((rows, LANES), BF16),
        pltpu.VMEM((rows, LANES), F32),
        pltpu.VMEM((rows, 2 * LANES), F32),
        pltpu.VMEM((rows, TK), F32),
        pltpu.VMEM((rows, TK), F32),
    ]


def _attn_c_kernel(q_ref, k_ref, v_ref, o_ref, qs_ref, m_ref, acc_ref, s0_ref, s1_ref):
    grp = C_HQ // C_HKV
    for h in range(grp):
        qs_ref[h * TQ:(h + 1) * TQ, :] = q_ref[:, h * C_HD:(h + 1) * C_HD]
    _flash(qs_ref, k_ref, v_ref, m_ref, acc_ref, (s0_ref, s1_ref), pl.program_id(1) == 0)
    for h in range(grp):
        o_ref[:, h * C_HD:(h + 1) * C_HD] = _flash_out(acc_ref, h).astype(BF16)


def _attn_c(q, k, v_aug):
    n = q.shape[0]
    grp = C_HQ // C_HKV
    wq = grp * C_HD
    return pl.pallas_call(
        _attn_c_kernel,
        grid=(C_HKV, n // TQ),
        in_specs=[
            pl.BlockSpec((TQ, wq), lambda g, i: (i, g)),
            pl.BlockSpec((n, C_HD), lambda g, i: (0, g), pipeline_mode=pl.Buffered(1)),
            pl.BlockSpec((n, 2 * C_HD), lambda g, i: (0, g), pipeline_mode=pl.Buffered(1)),
        ],
        out_specs=pl.BlockSpec((TQ, wq), lambda g, i: (i, g)),
        out_shape=jax.ShapeDtypeStruct((n, C_HQ * C_HD), BF16),
        scratch_shapes=_flash_scratch(grp),
        compiler_params=_cparams("arbitrary", "arbitrary"),
        name="attn_c",
    )(q, k, v_aug)


B_HEADS_PER_STEP = 2


def _attn_b_kernel(lam_init, q_ref, k_ref, v_ref, lq1_ref, lk1_ref, lq2_ref, lk2_ref, g_ref, o_ref,
                   qs_ref, m_ref, acc_ref, s0_ref, s1_ref):
    for h in range(B_HEADS_PER_STEP):
        q = q_ref[:, h * B_VD:(h + 1) * B_VD]
        lane = lax.broadcasted_iota(jnp.int32, q.shape, 1)
        zero = jnp.zeros_like(q)
        qs_ref[(2 * h) * TQ:(2 * h + 1) * TQ, :] = jnp.where(lane < B_HD, q, zero)
        qs_ref[(2 * h + 1) * TQ:(2 * h + 2) * TQ, :] = jnp.where(lane >= B_HD, q, zero)
    _flash(qs_ref, k_ref, v_ref, m_ref, acc_ref, (s0_ref, s1_ref), pl.program_id(1) == 0)
    lam = (jnp.exp(jnp.sum(lq1_ref[...] * lk1_ref[...], axis=-1, keepdims=True))
           - jnp.exp(jnp.sum(lq2_ref[...] * lk2_ref[...], axis=-1, keepdims=True))) + lam_init
    for h in range(B_HEADS_PER_STEP):
        o = _flash_out(acc_ref, 2 * h) - lam * _flash_out(acc_ref, 2 * h + 1)
        ms = jnp.mean(o * o, axis=-1, keepdims=True)
        o = o * lax.rsqrt(ms + SUBLN_EPS) * g_ref[...] * (1.0 - lam_init)
        o_ref[:, h * B_VD:(h + 1) * B_VD] = o.astype(BF16)


def _attn_b(q, k, v_aug, lq1, lk1, lq2, lk2, subln_g, lam_init):
    n = q.shape[0]
    hps = B_HEADS_PER_STEP
    blk = lambda h, i: (i, h)
    col = lambda h, i: (0, h)
    fixed = lambda h, i: (0, 0)
    small = pl.BlockSpec((1, B_HD), fixed)
    return pl.pallas_call(
        functools.partial(_attn_b_kernel, lam_init),
        grid=(B_H // hps, n // TQ),
        in_specs=[
            pl.BlockSpec((TQ, hps * B_VD), blk),
            pl.BlockSpec((n, hps * B_VD), col, pipeline_mode=pl.Buffered(1)),
            pl.BlockSpec((n, hps * 2 * B_VD), col, pipeline_mode=pl.Buffered(1)),
            small, small, small, small,
            pl.BlockSpec((1, B_VD), fixed),
        ],
        out_specs=pl.BlockSpec((TQ, hps * B_VD), blk),
        out_shape=jax.ShapeDtypeStruct((n, B_H * B_VD), BF16),
        scratch_shapes=_flash_scratch(2 * hps),
        compiler_params=_cparams("arbitrary", "arbitrary"),
        name="attn_b",
    )(q, k, v_aug, lq1, lk1, lq2, lk2, subln_g)


A_WIN_KEYS = TQ + 2 * WINDOW


def _attn_a_kernel(q_ref, k_ref, v_ref, sink_ref, o_ref, qs_ref):
    i = pl.program_id(0)
    n_lat = k_ref.shape[0] - CTX_LEN
    n_blk = A_HQ * A_HD // LANES
    for j in range(n_blk):
        q = q_ref[:, j * LANES:(j + 1) * LANES]
        lane = lax.broadcasted_iota(jnp.int32, q.shape, 1)
        zero = jnp.zeros_like(q)
        qs_ref[(2 * j) * TQ:(2 * j + 1) * TQ, :] = jnp.where(lane < A_HD, q, zero)
        qs_ref[(2 * j + 1) * TQ:(2 * j + 2) * TQ, :] = jnp.where(lane >= A_HD, q, zero)
    sink = jnp.concatenate([jnp.broadcast_to(sink_ref[r:r + 1, :], (TQ, LANES)) for r in range(A_HQ)], axis=0)
    s_c = _dot_nt(qs_ref[...], k_ref[0:CTX_LEN, :])
    vc = v_ref[0:CTX_LEN, :]

    def finish(m, acc):
        o = acc[:, :LANES] / (acc[:, LANES:] + jnp.exp2(sink - m))
        lane = lax.broadcasted_iota(jnp.int32, (TQ, LANES), 1)
        for j in range(n_blk):
            lo = o[(2 * j) * TQ:(2 * j + 1) * TQ, :]
            hi = o[(2 * j + 1) * TQ:(2 * j + 2) * TQ, :]
            o_ref[:, j * LANES:(j + 1) * LANES] = jnp.where(lane < A_HD, lo, hi).astype(BF16)

    @pl.when(i == 0)
    def _():
        m = jnp.maximum(sink, jnp.max(s_c, axis=1, keepdims=True))
        p_c = jnp.exp2(s_c - _lane_tile(m, CTX_LEN)).astype(BF16)
        finish(m, _dot(p_c, vc))

    @pl.when(i > 0)
    def _():
        q0 = (i - 1) * TQ
        ws = jnp.clip(q0 - WINDOW, 0, n_lat - A_WIN_KEYS)
        ws = pl.multiple_of(ws, WINDOW)
        qpos = q0 + lax.broadcasted_iota(jnp.int32, (TQ, A_WIN_KEYS), 0)
        kpos = ws + lax.broadcasted_iota(jnp.int32, (TQ, A_WIN_KEYS), 1)
        ok = jnp.abs(qpos - kpos) <= WINDOW
        s_l = _dot_nt(qs_ref[...], k_ref[pl.ds(CTX_LEN + ws, A_WIN_KEYS), :])
        s_l = jnp.where(ok[None], s_l.reshape(A_HQ, TQ, A_WIN_KEYS), -jnp.inf).reshape(A_HQ * TQ, A_WIN_KEYS)
        m = jnp.maximum(sink, jnp.maximum(jnp.max(s_c, axis=1, keepdims=True),
                                          jnp.max(s_l, axis=1, keepdims=True)))
        p_c = jnp.exp2(s_c - _lane_tile(m, CTX_LEN)).astype(BF16)
        p_l = jnp.exp2(s_l - _lane_tile(m, A_WIN_KEYS)).astype(BF16)
        finish(m, _dot(p_c, vc) + _dot(p_l, v_ref[pl.ds(CTX_LEN + ws, A_WIN_KEYS), :]))


def _attn_a(q, k, v_aug, sink_rows):
    n = q.shape[0]
    wq = A_HQ * A_HD
    wkv = A_HKV * A_HD
    return pl.pallas_call(
        _attn_a_kernel,
        grid=(n // TQ,),
        in_specs=[
            pl.BlockSpec((TQ, wq), lambda i: (i, 0)),
            pl.BlockSpec((n, wkv), lambda i: (0, 0)),
            pl.BlockSpec((n, 2 * wkv), lambda i: (0, 0)),
            pl.BlockSpec((A_HQ, LANES), lambda i: (0, 0)),
        ],
        out_specs=pl.BlockSpec((TQ, wq), lambda i: (i, 0)),
        out_shape=jax.ShapeDtypeStruct((n, wq), BF16),
        scratch_shapes=[pltpu.VMEM((A_HQ * TQ, LANES), BF16)],
        compiler_params=_cparams("arbitrary"),
        name="attn_a",
    )(q, k, v_aug, sink_rows)


def _layer_norm(z, g, b):
    mu = jnp.mean(z, axis=-1, keepdims=True)
    zc = z - mu
    var = jnp.mean(zc * zc, axis=-1, keepdims=True)
    return zc * lax.rsqrt(var + LN_EPS) * g + b


def _route(logits_t, bias_t):
    t = logits_t.shape[1]
    score = [jax.nn.sigmoid(logits_t[e:e + 1, :]) for e in range(N_EXPERTS)]
    sel = [score[e] + bias_t[e:e + 1, :] for e in range(N_EXPERTS)]
    best_g = jnp.zeros((1, t), jnp.int32)
    best_s = None
    for g in range(N_GROUPS):
        a = sel[g * EXPERTS_PER_GROUP:(g + 1) * EXPERTS_PER_GROUP]
        gs = None
        for x in range(EXPERTS_PER_GROUP):
            for y in range(x + 1, EXPERTS_PER_GROUP):
                pair = a[x] + a[y]
                gs = pair if gs is None else jnp.maximum(gs, pair)
        if best_s is None:
            best_s = gs
        else:
            upd = gs > best_s
            best_g = jnp.where(upd, g, best_g)
            best_s = jnp.where(upd, gs, best_s)
    neg = jnp.full((1, t), -jnp.inf, F32)
    masked = [jnp.where(best_g == (e // EXPERTS_PER_GROUP), sel[e], neg) for e in range(N_EXPERTS)]
    i1 = jnp.zeros((1, t), jnp.int32)
    v1 = masked[0]
    for e in range(1, N_EXPERTS):
        upd = masked[e] > v1
        i1 = jnp.where(upd, e, i1)
        v1 = jnp.where(upd, masked[e], v1)
    i2 = jnp.full((1, t), -1, jnp.int32)
    v2 = neg
    for e in range(N_EXPERTS):
        upd = (masked[e] > v2) & (i1 != e)
        i2 = jnp.where(upd, e, i2)
        v2 = jnp.where(upd, masked[e], v2)
    zero = jnp.zeros((1, t), F32)
    w1 = zero
    w2 = zero
    for e in range(N_EXPERTS):
        w1 = w1 + jnp.where(i1 == e, score[e], zero)
        w2 = w2 + jnp.where(i2 == e, score[e], zero)
    den = w1 + w2
    return i1, i2, w1 / den, w2 / den


def _post_kernel(o1_ref, o2_ref, w1_ref, w2_ref, x_ref, mod_ref, lng_ref, lnb_ref, wr_ref, br_ref,
                 x1_ref, h2_ref, route_ref, wcol_ref, cnt_out_ref, cnt_ref):
    @pl.when(pl.program_id(0) == 0)
    def _():
        cnt_ref[...] = jnp.zeros(cnt_ref.shape, F32)

    g1 = mod_ref[0, 2:3, :]
    sh2 = mod_ref[0, 3:4, :]
    sc2 = mod_ref[0, 4:5, :]
    y = _dot(o1_ref[...], w1_ref[...]) + _dot(o2_ref[...], w2_ref[...])
    x1 = _layer_norm(ALPHA * x_ref[...] + g1 * y, lng_ref[...], lnb_ref[...])
    x1_ref[...] = x1
    h2 = x1 * (1.0 + sc2) + sh2
    h2_ref[...] = h2
    h_hi = h2.astype(BF16)
    h_lo = (h2 - h_hi.astype(F32)).astype(BF16)
    wr = wr_ref[...]
    w_hi = wr.astype(BF16)
    w_lo = (wr - w_hi.astype(F32)).astype(BF16)
    logits = _dot(h_hi, w_hi) + (_dot(h_hi, w_lo) + _dot(h_lo, w_hi))
    i1, i2, w1, w2 = _route(logits.T, br_ref[...])

    row = lax.broadcasted_iota(jnp.int32, (LANES, TQ), 0)
    hit1 = row == i1
    hit2 = row == i2
    assigned = jnp.where(hit1 | hit2, 1.0, 0.0)
    t_from = lax.broadcasted_iota(jnp.int32, (TQ, TQ), 0)
    t_to = lax.broadcasted_iota(jnp.int32, (TQ, TQ), 1)
    earlier = jnp.where(t_from < t_to, 1.0, 0.0).astype(BF16)
    rank = cnt_ref[:, 0:1] + _dot(assigned.astype(BF16), earlier)
    r1 = jnp.sum(jnp.where(hit1, rank, 0.0), axis=0, keepdims=True).astype(jnp.int32)
    r2 = jnp.sum(jnp.where(hit2, rank, 0.0), axis=0, keepdims=True).astype(jnp.int32)
    cnt_ref[...] = cnt_ref[...] + jnp.sum(assigned, axis=1, keepdims=True)
    cnt_out_ref[...] = cnt_ref[...]

    row8 = lax.broadcasted_iota(jnp.int32, (8, TQ), 0)
    route_ref[0] = jnp.where(row8 == 0, i1, jnp.where(row8 == 1, i2, jnp.where(row8 == 2, r1,
                             jnp.where(row8 == 3, r2, 0))))
    wcol_ref[...] = jnp.where(row == 0, w1, jnp.where(row == 1, w2, 0.0)).T


def _post(o1, o2, o1_col, o2_col, w1, w2, x, mods, layer, ln_g, ln_b, wr, br):
    n, d = x.shape
    half = w1.shape[0]
    row = lambda i: (i, 0)
    fixed = lambda i: (0, 0)
    return pl.pallas_call(
        _post_kernel,
        grid=(n // TQ,),
        in_specs=[
            pl.BlockSpec((TQ, half), lambda i: (i, o1_col)),
            pl.BlockSpec((TQ, half), lambda i: (i, o2_col)),
            pl.BlockSpec((half, d), fixed),
            pl.BlockSpec((half, d), fixed),
            pl.BlockSpec((TQ, d), row),
            _mod_spec(layer),
            pl.BlockSpec((1, d), fixed),
            pl.BlockSpec((1, d), fixed),
            pl.BlockSpec((d, LANES), fixed),
            pl.BlockSpec((LANES, 1), fixed),
        ],
        out_specs=[
            pl.BlockSpec((TQ, d), row),
            pl.BlockSpec((TQ, d), row),
            pl.BlockSpec((1, 8, TQ), lambda i: (i, 0, 0)),
            pl.BlockSpec((TQ, LANES), row),
            pl.BlockSpec((LANES, LANES), fixed),
        ],
        out_shape=[
            jax.ShapeDtypeStruct((n, d), F32),
            jax.ShapeDtypeStruct((n, d), F32),
            jax.ShapeDtypeStruct((n // TQ, 8, TQ), jnp.int32),
            jax.ShapeDtypeStruct((n, LANES), F32),
            jax.ShapeDtypeStruct((LANES, LANES), F32),
        ],
        scratch_shapes=[pltpu.VMEM((LANES, LANES), F32)],
        compiler_params=_cparams("arbitrary"),
        name="post_attn",
    )(o1, o2, w1, w2, x, mods, ln_g, ln_b, wr, br)


def _moe_layout(route, counts, n):
    n_tiles_max = (2 * n) // MOE_TM + N_EXPERTS
    cnt = counts[:N_EXPERTS, 0].astype(jnp.int32)
    tiles = (cnt + MOE_TM - 1) // MOE_TM
    tiles_end = jnp.cumsum(tiles)
    seg_start = (tiles_end - tiles) * MOE_TM
    experts = jnp.arange(N_EXPERTS)

    def start_of(e):
        return jnp.sum(jnp.where(e[..., None] == experts, seg_start, 0), axis=-1)

    pos = jnp.stack([start_of(route[:, 0]) + route[:, 2],
                     start_of(route[:, 1]) + route[:, 3]], axis=1)
    n_used = tiles_end[-1]
    j = jnp.minimum(jnp.arange(n_tiles_max), n_used - 1)
    tile_expert = jnp.sum(j[:, None] >= tiles_end[None, :], axis=1).astype(jnp.int32)
    last_tiles = jnp.maximum(tiles_end - 1, 0)
    tail_tiles = jnp.minimum(n_used + experts, n_tiles_max - 1)
    live = jnp.concatenate([tiles > 0, n_used + experts < n_tiles_max])
    zero_tiles = jnp.concatenate([last_tiles, tail_tiles, live]).astype(jnp.int32)
    return pos.astype(jnp.int32), tile_expert, n_used.reshape(1).astype(jnp.int32), zero_tiles, n_tiles_max


def _scatter_kernel(zt_ref, pos_ref, h_ref, xs_ref, zero_ref, sem, zsem):
    def zero_copy(t):
        return pltpu.make_async_copy(zero_ref, xs_ref.at[pl.ds(zt_ref[t] * MOE_TM, MOE_TM)], zsem)

    @pl.when(pl.program_id(0) == 0)
    def _():
        zero_ref[...] = jnp.zeros(zero_ref.shape, F32)
        for t in range(2 * N_EXPERTS):
            @pl.when(zt_ref[2 * N_EXPERTS + t] != 0)
            def _():
                zero_copy(t).start()
        for t in range(2 * N_EXPERTS):
            @pl.when(zt_ref[2 * N_EXPERTS + t] != 0)
            def _():
                zero_copy(t).wait()

    def row_copy(r, k):
        return pltpu.make_async_copy(h_ref.at[pl.ds(r, 1)], xs_ref.at[pl.ds(pos_ref[0, k, r], 1)], sem)

    def start(r, carry):
        row_copy(r, 0).start()
        row_copy(r, 1).start()
        return carry

    def wait(r, carry):
        row_copy(r, 0).wait()
        row_copy(r, 1).wait()
        return carry

    lax.fori_loop(0, TQ, start, 0, unroll=8)
    lax.fori_loop(0, TQ, wait, 0, unroll=8)


def _scatter(zero_tiles, pos, h2, n_rows):
    n, d = h2.shape
    return pl.pallas_call(
        _scatter_kernel,
        grid_spec=pltpu.PrefetchScalarGridSpec(
            num_scalar_prefetch=1,
            grid=(n // TQ,),
            in_specs=[
                pl.BlockSpec((1, 2, TQ), lambda i, zt: (i, 0, 0), memory_space=pltpu.SMEM),
                pl.BlockSpec((TQ, d), lambda i, zt: (i, 0)),
            ],
            out_specs=pl.BlockSpec(memory_space=pl.ANY),
            scratch_shapes=[pltpu.VMEM((MOE_TM, d), F32), pltpu.SemaphoreType.DMA(()),
                            pltpu.SemaphoreType.DMA(())],
        ),
        out_shape=jax.ShapeDtypeStruct((n_rows, d), F32),
        compiler_params=_cparams("arbitrary"),
        name="moe_scatter",
    )(zero_tiles, pos, h2)


def _expert_kernel(te_ref, nt_ref, x_ref, wg_ref, wu_ref, wd_ref, y_ref, wgb_ref, wub_ref, wdb_ref):
    j = pl.program_id(0)
    used = j < nt_ref[0]
    new_expert = jnp.logical_or(j == 0, te_ref[j] != te_ref[jnp.maximum(j - 1, 0)])

    @pl.when(jnp.logical_and(used, new_expert))
    def _():
        wgb_ref[...] = wg_ref[0].astype(BF16)
        wub_ref[...] = wu_ref[0].astype(BF16)
        wdb_ref[...] = wd_ref[0].astype(BF16)

    @pl.when(used)
    def _():
        x = x_ref[...].astype(BF16)
        g = _dot(x, wgb_ref[...])
        u = _dot(x, wub_ref[...])
        a = (g * jax.nn.sigmoid(g)) * u
        y_ref[...] = _dot(a.astype(BF16), wdb_ref[...])

    @pl.when(jnp.logical_not(used))
    def _():
        y_ref[...] = jnp.zeros(y_ref.shape, F32)


def _experts(tile_expert, n_used, xs, wg, wu, wd, layer, n_tiles_max):
    _, d = xs.shape
    f = wg.shape[3]
    weight = lambda j, te, nt: (layer, te[j], 0, 0)
    return pl.pallas_call(
        _expert_kernel,
        grid_spec=pltpu.PrefetchScalarGridSpec(
            num_scalar_prefetch=2,
            grid=(n_tiles_max,),
            in_specs=[
                pl.BlockSpec((MOE_TM, d), lambda j, te, nt: (jnp.minimum(j, nt[0] - 1), 0)),
                pl.BlockSpec((None, 1, d, f), weight),
                pl.BlockSpec((None, 1, d, f), weight),
                pl.BlockSpec((None, 1, f, d), weight),
            ],
            out_specs=pl.BlockSpec((MOE_TM, d), lambda j, te, nt: (j, 0)),
            scratch_shapes=[pltpu.VMEM((d, f), BF16), pltpu.VMEM((d, f), BF16), pltpu.VMEM((f, d), BF16)],
        ),
        out_shape=jax.ShapeDtypeStruct(xs.shape, F32),
        compiler_params=_cparams("arbitrary"),
        name="moe_experts",
    )(tile_expert, n_used, xs, wg, wu, wd)


def _combine_kernel(pos_ref, pos_next_ref, ys_ref, x_ref, wcol_ref, mod_ref, lng_ref, lnb_ref, o_ref,
                    buf_ref, sem):
    i = pl.program_id(0)
    n_steps = pl.num_programs(0)
    slot = lax.rem(i, 2)

    def row_copy(p_ref, dst_slot, r, k):
        return pltpu.make_async_copy(ys_ref.at[pl.ds(p_ref[0, k, r], 1)],
                                     buf_ref.at[dst_slot, k, pl.ds(r, 1)], sem.at[dst_slot])

    def gather(p_ref, dst_slot):
        def start(r, carry):
            row_copy(p_ref, dst_slot, r, 0).start()
            row_copy(p_ref, dst_slot, r, 1).start()
            return carry
        lax.fori_loop(0, TQ, start, 0, unroll=8)

    @pl.when(i == 0)
    def _():
        gather(pos_ref, 0)

    @pl.when(i + 1 < n_steps)
    def _():
        gather(pos_next_ref, 1 - slot)

    def wait(r, carry):
        row_copy(pos_ref, slot, r, 0).wait()
        row_copy(pos_ref, slot, r, 1).wait()
        return carry
    lax.fori_loop(0, TQ, wait, 0, unroll=8)

    w = wcol_ref[...]
    moe = w[:, 0:1] * buf_ref[slot, 0] + w[:, 1:2] * buf_ref[slot, 1]
    g2 = mod_ref[0, 5:6, :]
    o_ref[...] = _layer_norm(ALPHA * x_ref[...] + g2 * moe, lng_ref[...], lnb_ref[...])


def _combine(pos, ys, x1, wcol, mods, layer, ln_g, ln_b, latents_only):
    n, d = x1.shape
    n_steps = n // TQ
    row = lambda i: (i, 0)
    fixed = lambda i: (0, 0)
    if latents_only:
        out_rows, out_map = n - CTX_LEN, lambda i: (jnp.maximum(i - CTX_LEN // TQ, 0), 0)
    else:
        out_rows, out_map = n, row
    return pl.pallas_call(
        _combine_kernel,
        grid=(n_steps,),
        in_specs=[
            pl.BlockSpec((1, 2, TQ), lambda i: (i, 0, 0), memory_space=pltpu.SMEM),
            pl.BlockSpec((1, 2, TQ), lambda i: (jnp.minimum(i + 1, n_steps - 1), 0, 0), memory_space=pltpu.SMEM),
            pl.BlockSpec(memory_space=pl.ANY),
            pl.BlockSpec((TQ, d), row),
            pl.BlockSpec((TQ, LANES), row),
            _mod_spec(layer),
            pl.BlockSpec((1, d), fixed),
            pl.BlockSpec((1, d), fixed),
        ],
        out_specs=pl.BlockSpec((TQ, d), out_map),
        out_shape=jax.ShapeDtypeStruct((out_rows, d), F32),
        scratch_shapes=[pltpu.VMEM((2, 2, TQ, d), F32), pltpu.SemaphoreType.DMA((2,))],
        compiler_params=_cparams("arbitrary"),
        name="moe_combine",
    )(pos, pos, ys, x1, wcol, mods, ln_g, ln_b)


def _moe(h2, route, wcol, counts, wg, wu, wd, x1, mods, layer, ln_g, ln_b, latents_only):
    n = h2.shape[0]
    pos, tile_expert, n_used, zero_tiles, n_tiles_max = _moe_layout(route, counts, n)
    xs = _scatter(zero_tiles, pos, h2, n_tiles_max * MOE_TM)
    ys = _experts(tile_expert, n_used, xs, wg, wu, wd, layer, n_tiles_max)
    return _combine(pos, ys, x1, wcol, mods, layer, ln_g, ln_b, latents_only)


def _rope_tables(n_lat, head_dim):
    quarter = head_dim // 4
    t = jnp.arange(n_lat)
    inv_freq = ROPE_THETA ** (-jnp.arange(quarter, dtype=F32) / quarter)
    ang_r = (t // GRID_W)[:, None].astype(F32) * inv_freq
    ang_c = (t % GRID_W)[:, None].astype(F32) * inv_freq
    cos = jnp.concatenate([jnp.cos(ang_r)] * 2 + [jnp.cos(ang_c)] * 2, axis=1)
    sin = jnp.concatenate([-jnp.sin(ang_r), jnp.sin(ang_r), -jnp.sin(ang_c), jnp.sin(ang_c)], axis=1)
    reps = LANES // head_dim
    cos = jnp.tile(cos, (1, reps))
    sin = jnp.tile(sin, (1, reps))
    cos = jnp.concatenate([jnp.ones((CTX_LEN, LANES), F32), cos], axis=0)
    sin = jnp.concatenate([jnp.zeros((CTX_LEN, LANES), F32), sin], axis=0)
    return cos, sin


def _a_head_order():
    grp = A_HQ // A_HKV
    order = []
    for j in range(grp):
        for g in range(A_HKV):
            order.append(g * grp + j)
    return order


def kernel(x, c, ctx, c_ctx, w_mod, b_mod, ln_g, ln_b, w_in_even, w_out_even, sink_logits, lam_q1, lam_k1,
           lam_q2, lam_k2, subln_g, w_in_odd, w_out_odd, q_norm_g, k_norm_g, w_router, b_router, w_gate,
           w_up, w_down):
    n_lat = x.shape[1]
    assert x.shape == (1, n_lat, D_MODEL) and ctx.shape == (1, CTX_LEN, D_MODEL)
    assert n_lat % max(UNROLL * TK, MOE_TM) == 0 and n_lat % GRID_W == 0
    xs = jnp.concatenate([ctx[0], x[0]], axis=0)

    c8 = jnp.zeros((8, D_MODEL), F32).at[0].set(c_ctx).at[1].set(c[0])
    mods = _modulation(c8, w_mod, b_mod)

    cos64, sin64 = _rope_tables(n_lat, A_HD)
    cos128, sin128 = _rope_tables(n_lat, C_HD)

    order = _a_head_order()
    a_cols = jnp.concatenate([jnp.arange(h * A_HD, (h + 1) * A_HD) for h in order])
    wq_a = A_HQ * A_HD
    w_in_even_p = jnp.concatenate([w_in_even[:, :, :wq_a][:, :, a_cols], w_in_even[:, :, wq_a:]], axis=2).astype(BF16)
    w_out_even_p = jnp.concatenate([w_out_even[:, :wq_a][:, a_cols], w_out_even[:, wq_a:]], axis=1).astype(BF16)
    w_in_odd_b = w_in_odd.astype(BF16)
    w_out_odd_b = w_out_odd.astype(BF16)
    wg, wu, wd = w_gate, w_up, w_down
    wr = jnp.pad(w_router, ((0, 0), (0, LANES - N_EXPERTS)))
    br = jnp.pad(b_router.astype(F32), (0, LANES - N_EXPERTS)).reshape(LANES, 1)

    segs_even = ((A_HQ * A_HD, True, None, (A_HD ** -0.5) * LOG2E, False), (A_HKV * A_HD, True, None, None, False),
                 (A_HKV * A_HD, False, None, None, True),
                 (B_H * 2 * B_HD, True, None, (B_HD ** -0.5) * LOG2E, False),
                 (B_H * 2 * B_HD, True, None, None, False), (B_H * B_VD, False, None, None, True))
    segs_odd = ((C_HQ * C_HD, True, 0, (C_HD ** -0.5) * LOG2E, False), (C_HKV * C_HD, True, 1, None, False),
                (C_HKV * C_HD, False, None, None, True))

    for l in range(DEPTH):
        i = l // 2
        lng = ln_g[l]
        lnb = ln_b[l]
        if l % 2 == 0:
            lam_init = 0.8 - 0.6 * math.exp(-0.3 * l)
            aq, ak, av, bq, bk, bv = _inproj(xs, mods, l, w_in_even_p[i], cos64, sin64, segs_even, A_HD // 4)
            sink_rows = jnp.broadcast_to(
                (sink_logits[i][jnp.array(order)] * LOG2E)[:, None], (A_HQ, LANES)).astype(F32)
            oa = _attn_a(aq, ak, av, sink_rows)
            ob = _attn_b(bq, bk, bv, lam_q1[i][None], lam_k1[i][None], lam_q2[i][None], lam_k2[i][None],
                         subln_g[i][None], lam_init)
            half = wq_a
            routed = _post(oa, ob, 0, 0, w_out_even_p[i][:half], w_out_even_p[i][half:], xs, mods, l,
                           lng[0][None], lnb[0][None], wr, br)
        else:
            q, k, v = _inproj(xs, mods, l, w_in_odd_b[i], cos128, sin128, segs_odd, C_HD // 4,
                              gains=(q_norm_g[i][None], k_norm_g[i][None]))
            o = _attn_c(q, k, v)
            half = C_HQ * C_HD // 2
            routed = _post(o, o, 0, 1, w_out_odd_b[i][:half], w_out_odd_b[i][half:], xs, mods, l,
                           lng[0][None], lnb[0][None], wr, br)
        x1, h2, route, wcol, counts = routed
        xs = _moe(h2, route, wcol, counts, wg, wu, wd, x1, mods, l, lng[1][None], lnb[1][None],
                  latents_only=(l == DEPTH - 1))
    return xs[None]
```

```python
import functools
import math

import jax
import jax.numpy as jnp
from jax import lax
from jax.experimental import pallas as pl
from jax.experimental.pallas import tpu as pltpu

D_MODEL = 1024
SEQ = 16384
DEPTH = 4
GRID_W = 64
CTX_LEN = 256
WINDOW = 128
ROPE_THETA = 10000.0
A_HQ, A_HKV, A_HD = 8, 2, 64
B_H, B_HD = 4, 64
B_VD = 2 * B_HD
C_HQ, C_HKV, C_HD = 8, 2, 128
N_EXPERTS = 16
N_GROUPS = 4
EXPERTS_PER_GROUP = N_EXPERTS // N_GROUPS
D_EXPERT = 512
ALPHA = (2 * DEPTH) ** 0.25
LN_EPS = 1e-5
QK_EPS = 1e-6
SUBLN_EPS = 1e-5

LANES = 128
TQ = CTX_LEN
TK = 512
UNROLL = 4
MOE_TM = 256
VMEM_LIMIT = 56 * 1024 * 1024
LOG2E = 1.4426950408889634
BF16 = jnp.bfloat16
F32 = jnp.float32


def _cparams(*sem):
    return pltpu.CompilerParams(dimension_semantics=sem, vmem_limit_bytes=VMEM_LIMIT)


def _dot(a, b):
    return jnp.dot(a, b, preferred_element_type=F32)


def _dot_nt(a, b):
    return lax.dot_general(a, b, (((1,), (1,)), ((), ())), preferred_element_type=F32)


def _lane_tile(x, width):
    reps = width // x.shape[-1]
    return x if reps == 1 else jnp.concatenate([x] * reps, axis=-1)


def _mod_kernel(c_ref, w_ref, b_ref, o_ref):
    c = c_ref[...]
    s = c * jax.nn.sigmoid(c)
    o_ref[0] = _dot(s.astype(BF16), w_ref[0].astype(BF16)) + b_ref[0]


def _modulation(c8, w_mod, b_mod):
    depth, d, n6 = w_mod.shape
    tn = 1536
    out = pl.pallas_call(
        _mod_kernel,
        grid=(depth, n6 // tn),
        in_specs=[
            pl.BlockSpec((8, d), lambda l, j: (0, 0)),
            pl.BlockSpec((1, d, tn), lambda l, j: (l, 0, j)),
            pl.BlockSpec((1, 1, tn), lambda l, j: (l, 0, j)),
        ],
        out_specs=pl.BlockSpec((1, 8, tn), lambda l, j: (l, 0, j)),
        out_shape=jax.ShapeDtypeStruct((depth, 8, n6), F32),
        compiler_params=_cparams("arbitrary", "arbitrary"),
        name="modulation",
    )(c8, w_mod, b_mod.reshape(depth, 1, n6))
    m = out[:, :2].reshape(depth, 2, 6, d)
    m = jnp.pad(m, ((0, 0), (0, 0), (0, 2), (0, 0)))
    return m.reshape(depth * 2, 8, d)


def _mod_spec(layer):
    return pl.BlockSpec((1, 8, D_MODEL), lambda i, *_: (2 * layer + jnp.minimum(i, 1), 0, 0))


def _rope_chunk(z, cos, sin, quarter):
    lane = lax.broadcasted_iota(jnp.int32, z.shape, 1)
    first = (lane % (2 * quarter)) < quarter
    nxt = pltpu.roll(z, LANES - quarter, axis=1)
    prv = pltpu.roll(z, quarter, axis=1)
    return z * cos + jnp.where(first, nxt, prv) * sin


def _inproj_kernel(segs, quarter, n_gain, x_ref, mod_ref, w_ref, cos_ref, sin_ref, *rest):
    gains = rest[:n_gain]
    outs = rest[n_gain:]
    sh = mod_ref[0, 0:1, :]
    sc = mod_ref[0, 1:2, :]
    hb = (x_ref[...] * (1.0 + sc) + sh).astype(BF16)
    cos = cos_ref[...]
    sin = sin_ref[...]
    off = 0
    for (width, rope, gain_idx, scale, form), o_ref in zip(segs, outs):
        z = _dot(hb, w_ref[:, off:off + width])
        for c in range(width // LANES):
            zc = z[:, c * LANES:(c + 1) * LANES]
            if gain_idx is not None:
                ms = jnp.mean(zc * zc, axis=-1, keepdims=True)
                zc = zc * lax.rsqrt(ms + QK_EPS) * gains[gain_idx][...]
            if rope:
                zc = _rope_chunk(zc, cos, sin, quarter)
            if scale is not None:
                zc = zc * scale
            if form == "ones":
                o_ref[:, 2 * c * LANES:(2 * c + 1) * LANES] = zc.astype(BF16)
                o_ref[:, (2 * c + 1) * LANES:(2 * c + 2) * LANES] = jnp.ones(zc.shape, BF16)
            elif form == "transposed":
                o_ref[c, 0] = zc.T.astype(BF16)
            else:
                o_ref[:, c * LANES:(c + 1) * LANES] = zc.astype(BF16)
        off += width


def _inproj(x, mods, layer, w, cos, sin, segs, quarter, gains=()):
    n, d = x.shape
    wtot = w.shape[1]
    row = lambda i: (i, 0)
    fixed = lambda i: (0, 0)
    in_specs = [
        pl.BlockSpec((TQ, d), row),
        _mod_spec(layer),
        pl.BlockSpec((d, wtot), fixed),
        pl.BlockSpec((TQ, LANES), row),
        pl.BlockSpec((TQ, LANES), row),
    ] + [pl.BlockSpec((1, LANES), fixed) for _ in gains]
    out_specs, out_shape = [], []
    for width, _, _, _, form in segs:
        if form == "transposed":
            heads = width // LANES
            out_specs.append(pl.BlockSpec((heads, 1, LANES, TQ), lambda i: (0, i, 0, 0)))
            out_shape.append(jax.ShapeDtypeStruct((heads, n // TQ, LANES, TQ), BF16))
        else:
            w_ = width * (2 if form == "ones" else 1)
            out_specs.append(pl.BlockSpec((TQ, w_), row))
            out_shape.append(jax.ShapeDtypeStruct((n, w_), BF16))
    return pl.pallas_call(
        functools.partial(_inproj_kernel, segs, quarter, len(gains)),
        grid=(n // TQ,),
        in_specs=in_specs,
        out_specs=out_specs,
        out_shape=out_shape,
        compiler_params=_cparams("arbitrary"),
        name="inproj",
    )(x, mods, w, cos, sin, *gains)


def _flash(qs_ref, k_ref, vt_ref, m_ref, l_ref, acc_ref, s_refs, is_ctx_block):
    n_chunks = (k_ref.shape[0] - CTX_LEN) // TK
    n_sets = k_ref.shape[1] // LANES
    cols = qs_ref.shape[0] // n_sets
    m_ref[...] = jnp.full(m_ref.shape, -jnp.inf, F32)
    l_ref[...] = jnp.zeros(l_ref.shape, F32)
    acc_ref[...] = jnp.zeros(acc_ref.shape, F32)

    def scores(slot, start, size):
        for g in range(n_sets):
            cs = slice(g * cols, (g + 1) * cols)
            s_refs[slot][0:size, cs] = _dot_nt(k_ref[pl.ds(start, size), g * LANES:(g + 1) * LANES], qs_ref[cs, :])

    def update(slot, first_block, n_blocks):
        s = s_refs[slot][0:n_blocks * TQ, :]
        m_prev = m_ref[0:1, :]
        m_new = jnp.maximum(m_prev, jnp.max(s, axis=0, keepdims=True))
        p = jnp.exp2(s - m_new)
        alpha = jnp.exp2(m_prev - m_new)
        l_ref[...] = jnp.broadcast_to(alpha * l_ref[0:1, :] + jnp.sum(p, axis=0, keepdims=True), l_ref.shape)
        pb = p.astype(BF16)
        for g in range(n_sets):
            cs = slice(g * cols, (g + 1) * cols)
            pv = _dot(vt_ref[g, first_block], pb[0:TQ, cs])
            for b in range(1, n_blocks):
                pv = pv + _dot(vt_ref[g, first_block + b], pb[b * TQ:(b + 1) * TQ, cs])
            acc_ref[:, cs] = acc_ref[:, cs] * alpha[:, cs] + pv
        m_ref[...] = jnp.broadcast_to(m_new, m_ref.shape)

    def lat(j):
        return pl.multiple_of(CTX_LEN + j * TK, CTX_LEN)

    blocks_per_chunk = TK // TQ
    ctx_blocks = CTX_LEN // TQ
    scores(1, 0, CTX_LEN)
    update(1, 0, ctx_blocks)

    @pl.when(jnp.logical_not(is_ctx_block))
    def _():
        scores(0, lat(0), TK)

        def trip(jj, carry):
            j = UNROLL * jj
            for u in range(UNROLL):
                scores((u + 1) % 2, lat(jnp.minimum(j + u + 1, n_chunks - 1)), TK)
                update(u % 2, ctx_blocks + (j + u) * blocks_per_chunk, blocks_per_chunk)
            return carry

        lax.fori_loop(0, n_chunks // UNROLL, trip, 0)


def _flash_out(l_ref, acc_ref, idx):
    cs = slice(idx * TQ, (idx + 1) * TQ)
    return (acc_ref[:, cs] / l_ref[0:1, cs]).T


def _flash_scratch(n_stack):
    rows = n_stack * TQ
    return [
        pltpu.VMEM((rows, LANES), BF16),
        pltpu.VMEM((8, rows), F32),
        pltpu.VMEM((8, rows), F32),
        pltpu.VMEM((LANES, rows), F32),
        pltpu.VMEM((TK, rows), F32),
        pltpu.VMEM((TK, rows), F32),
    ]


def _attn_c_kernel(q_ref, k_ref, vt_ref, o_ref, qs_ref, m_ref, l_ref, acc_ref, s0_ref, s1_ref):
    grp = C_HQ // C_HKV
    for h in range(grp):
        qs_ref[h * TQ:(h + 1) * TQ, :] = q_ref[:, h * C_HD:(h + 1) * C_HD]
    _flash(qs_ref, k_ref, vt_ref, m_ref, l_ref, acc_ref, (s0_ref, s1_ref), pl.program_id(1) == 0)
    for h in range(grp):
        o_ref[:, h * C_HD:(h + 1) * C_HD] = _flash_out(l_ref, acc_ref, h).astype(BF16)


def _attn_c(q, k, vt):
    n = q.shape[0]
    grp = C_HQ // C_HKV
    wq = grp * C_HD
    return pl.pallas_call(
        _attn_c_kernel,
        grid=(C_HKV, n // TQ),
        in_specs=[
            pl.BlockSpec((TQ, wq), lambda g, i: (i, g)),
            pl.BlockSpec((n, C_HD), lambda g, i: (0, g), pipeline_mode=pl.Buffered(1)),
            pl.BlockSpec((1, n // TQ, C_HD, TQ), lambda g, i: (g, 0, 0, 0), pipeline_mode=pl.Buffered(1)),
        ],
        out_specs=pl.BlockSpec((TQ, wq), lambda g, i: (i, g)),
        out_shape=jax.ShapeDtypeStruct((n, C_HQ * C_HD), BF16),
        scratch_shapes=_flash_scratch(grp),
        compiler_params=_cparams("arbitrary", "arbitrary"),
        name="attn_c",
    )(q, k, vt)


B_HEADS_PER_STEP = 2


def _attn_b_kernel(lam_init, q_ref, k_ref, vt_ref, lq1_ref, lk1_ref, lq2_ref, lk2_ref, g_ref, o_ref,
                   qs_ref, m_ref, l_ref, acc_ref, s0_ref, s1_ref):
    for h in range(B_HEADS_PER_STEP):
        q = q_ref[:, h * B_VD:(h + 1) * B_VD]
        lane = lax.broadcasted_iota(jnp.int32, q.shape, 1)
        zero = jnp.zeros_like(q)
        qs_ref[(2 * h) * TQ:(2 * h + 1) * TQ, :] = jnp.where(lane < B_HD, q, zero)
        qs_ref[(2 * h + 1) * TQ:(2 * h + 2) * TQ, :] = jnp.where(lane >= B_HD, q, zero)
    _flash(qs_ref, k_ref, vt_ref, m_ref, l_ref, acc_ref, (s0_ref, s1_ref), pl.program_id(1) == 0)
    lam = (jnp.exp(jnp.sum(lq1_ref[...] * lk1_ref[...], axis=-1, keepdims=True))
           - jnp.exp(jnp.sum(lq2_ref[...] * lk2_ref[...], axis=-1, keepdims=True))) + lam_init
    for h in range(B_HEADS_PER_STEP):
        o = _flash_out(l_ref, acc_ref, 2 * h) - lam * _flash_out(l_ref, acc_ref, 2 * h + 1)
        ms = jnp.mean(o * o, axis=-1, keepdims=True)
        o = o * lax.rsqrt(ms + SUBLN_EPS) * g_ref[...] * (1.0 - lam_init)
        o_ref[:, h * B_VD:(h + 1) * B_VD] = o.astype(BF16)


def _attn_b(q, k, vt, lq1, lk1, lq2, lk2, subln_g, lam_init):
    n = q.shape[0]
    hps = B_HEADS_PER_STEP
    blk = lambda h, i: (i, h)
    col = lambda h, i: (0, h)
    fixed = lambda h, i: (0, 0)
    small = pl.BlockSpec((1, B_HD), fixed)
    return pl.pallas_call(
        functools.partial(_attn_b_kernel, lam_init),
        grid=(B_H // hps, n // TQ),
        in_specs=[
            pl.BlockSpec((TQ, hps * B_VD), blk),
            pl.BlockSpec((n, hps * B_VD), col, pipeline_mode=pl.Buffered(1)),
            pl.BlockSpec((hps, n // TQ, B_VD, TQ), lambda h, i: (h, 0, 0, 0), pipeline_mode=pl.Buffered(1)),
            small, small, small, small,
            pl.BlockSpec((1, B_VD), fixed),
        ],
        out_specs=pl.BlockSpec((TQ, hps * B_VD), blk),
        out_shape=jax.ShapeDtypeStruct((n, B_H * B_VD), BF16),
        scratch_shapes=_flash_scratch(2 * hps),
        compiler_params=_cparams("arbitrary", "arbitrary"),
        name="attn_b",
    )(q, k, vt, lq1, lk1, lq2, lk2, subln_g)


A_WIN_KEYS = TQ + 2 * WINDOW


def _attn_a_kernel(q_ref, k_ref, v_ref, sink_ref, o_ref, qs_ref):
    i = pl.program_id(0)
    n_lat = k_ref.shape[0] - CTX_LEN
    n_blk = A_HQ * A_HD // LANES
    for j in range(n_blk):
        q = q_ref[:, j * LANES:(j + 1) * LANES]
        lane = lax.broadcasted_iota(jnp.int32, q.shape, 1)
        zero = jnp.zeros_like(q)
        qs_ref[(2 * j) * TQ:(2 * j + 1) * TQ, :] = jnp.where(lane < A_HD, q, zero)
        qs_ref[(2 * j + 1) * TQ:(2 * j + 2) * TQ, :] = jnp.where(lane >= A_HD, q, zero)
    sink = jnp.concatenate([jnp.broadcast_to(sink_ref[r:r + 1, :], (TQ, LANES)) for r in range(A_HQ)], axis=0)
    s_c = _dot_nt(qs_ref[...], k_ref[0:CTX_LEN, :])
    vc = v_ref[0:CTX_LEN, :]

    def finish(m, acc):
        o = acc[:, :LANES] / (acc[:, LANES:] + jnp.exp2(sink - m))
        lane = lax.broadcasted_iota(jnp.int32, (TQ, LANES), 1)
        for j in range(n_blk):
            lo = o[(2 * j) * TQ:(2 * j + 1) * TQ, :]
            hi = o[(2 * j + 1) * TQ:(2 * j + 2) * TQ, :]
            o_ref[:, j * LANES:(j + 1) * LANES] = jnp.where(lane < A_HD, lo, hi).astype(BF16)

    @pl.when(i == 0)
    def _():
        m = jnp.maximum(sink, jnp.max(s_c, axis=1, keepdims=True))
        p_c = jnp.exp2(s_c - _lane_tile(m, CTX_LEN)).astype(BF16)
        finish(m, _dot(p_c, vc))

    @pl.when(i > 0)
    def _():
        q0 = (i - 1) * TQ
        ws = jnp.clip(q0 - WINDOW, 0, n_lat - A_WIN_KEYS)
        ws = pl.multiple_of(ws, WINDOW)
        qpos = q0 + lax.broadcasted_iota(jnp.int32, (TQ, A_WIN_KEYS), 0)
        kpos = ws + lax.broadcasted_iota(jnp.int32, (TQ, A_WIN_KEYS), 1)
        ok = jnp.abs(qpos - kpos) <= WINDOW
        s_l = _dot_nt(qs_ref[...], k_ref[pl.ds(CTX_LEN + ws, A_WIN_KEYS), :])
        s_l = jnp.where(ok[None], s_l.reshape(A_HQ, TQ, A_WIN_KEYS), -jnp.inf).reshape(A_HQ * TQ, A_WIN_KEYS)
        m = jnp.maximum(sink, jnp.maximum(jnp.max(s_c, axis=1, keepdims=True),
                                          jnp.max(s_l, axis=1, keepdims=True)))
        p_c = jnp.exp2(s_c - _lane_tile(m, CTX_LEN)).astype(BF16)
        p_l = jnp.exp2(s_l - _lane_tile(m, A_WIN_KEYS)).astype(BF16)
        finish(m, _dot(p_c, vc) + _dot(p_l, v_ref[pl.ds(CTX_LEN + ws, A_WIN_KEYS), :]))


def _attn_a(q, k, v_aug, sink_rows):
    n = q.shape[0]
    wq = A_HQ * A_HD
    wkv = A_HKV * A_HD
    return pl.pallas_call(
        _attn_a_kernel,
        grid=(n // TQ,),
        in_specs=[
            pl.BlockSpec((TQ, wq), lambda i: (i, 0)),
            pl.BlockSpec((n, wkv), lambda i: (0, 0)),
            pl.BlockSpec((n, 2 * wkv), lambda i: (0, 0)),
            pl.BlockSpec((A_HQ, LANES), lambda i: (0, 0)),
        ],
        out_specs=pl.BlockSpec((TQ, wq), lambda i: (i, 0)),
        out_shape=jax.ShapeDtypeStruct((n, wq), BF16),
        scratch_shapes=[pltpu.VMEM((A_HQ * TQ, LANES), BF16)],
        compiler_params=_cparams("arbitrary"),
        name="attn_a",
    )(q, k, v_aug, sink_rows)


def _layer_norm(z, g, b):
    mu = jnp.mean(z, axis=-1, keepdims=True)
    zc = z - mu
    var = jnp.mean(zc * zc, axis=-1, keepdims=True)
    return zc * lax.rsqrt(var + LN_EPS) * g + b


def _route(logits_t, bias_t):
    t = logits_t.shape[1]
    score = [jax.nn.sigmoid(logits_t[e:e + 1, :]) for e in range(N_EXPERTS)]
    sel = [score[e] + bias_t[e:e + 1, :] for e in range(N_EXPERTS)]
    best_g = jnp.zeros((1, t), jnp.int32)
    best_s = None
    for g in range(N_GROUPS):
        a = sel[g * EXPERTS_PER_GROUP:(g + 1) * EXPERTS_PER_GROUP]
        gs = None
        for x in range(EXPERTS_PER_GROUP):
            for y in range(x + 1, EXPERTS_PER_GROUP):
                pair = a[x] + a[y]
                gs = pair if gs is None else jnp.maximum(gs, pair)
        if best_s is None:
            best_s = gs
        else:
            upd = gs > best_s
            best_g = jnp.where(upd, g, best_g)
            best_s = jnp.where(upd, gs, best_s)
    neg = jnp.full((1, t), -jnp.inf, F32)
    masked = [jnp.where(best_g == (e // EXPERTS_PER_GROUP), sel[e], neg) for e in range(N_EXPERTS)]
    i1 = jnp.zeros((1, t), jnp.int32)
    v1 = masked[0]
    for e in range(1, N_EXPERTS):
        upd = masked[e] > v1
        i1 = jnp.where(upd, e, i1)
        v1 = jnp.where(upd, masked[e], v1)
    i2 = jnp.full((1, t), -1, jnp.int32)
    v2 = neg
    for e in range(N_EXPERTS):
        upd = (masked[e] > v2) & (i1 != e)
        i2 = jnp.where(upd, e, i2)
        v2 = jnp.where(upd, masked[e], v2)
    zero = jnp.zeros((1, t), F32)
    w1 = zero
    w2 = zero
    for e in range(N_EXPERTS):
        w1 = w1 + jnp.where(i1 == e, score[e], zero)
        w2 = w2 + jnp.where(i2 == e, score[e], zero)
    den = w1 + w2
    return i1, i2, w1 / den, w2 / den


def _post_kernel(o1_ref, o2_ref, w1_ref, w2_ref, x_ref, mod_ref, lng_ref, lnb_ref, wr_ref, br_ref,
                 x1_ref, h2_ref, route_ref, wcol_ref, cnt_out_ref, cnt_ref):
    @pl.when(pl.program_id(0) == 0)
    def _():
        cnt_ref[...] = jnp.zeros(cnt_ref.shape, F32)

    g1 = mod_ref[0, 2:3, :]
    sh2 = mod_ref[0, 3:4, :]
    sc2 = mod_ref[0, 4:5, :]
    y = _dot(o1_ref[...], w1_ref[...]) + _dot(o2_ref[...], w2_ref[...])
    x1 = _layer_norm(ALPHA * x_ref[...] + g1 * y, lng_ref[...], lnb_ref[...])
    x1_ref[...] = x1
    h2 = x1 * (1.0 + sc2) + sh2
    h2_ref[...] = h2
    h_hi = h2.astype(BF16)
    h_lo = (h2 - h_hi.astype(F32)).astype(BF16)
    wr = wr_ref[...]
    w_hi = wr.astype(BF16)
    w_lo = (wr - w_hi.astype(F32)).astype(BF16)
    logits = _dot(h_hi, w_hi) + (_dot(h_hi, w_lo) + _dot(h_lo, w_hi))
    i1, i2, w1, w2 = _route(logits.T, br_ref[...])

    row = lax.broadcasted_iota(jnp.int32, (LANES, TQ), 0)
    hit1 = row == i1
    hit2 = row == i2
    assigned = jnp.where(hit1 | hit2, 1.0, 0.0)
    t_from = lax.broadcasted_iota(jnp.int32, (TQ, TQ), 0)
    t_to = lax.broadcasted_iota(jnp.int32, (TQ, TQ), 1)
    earlier = jnp.where(t_from < t_to, 1.0, 0.0).astype(BF16)
    rank = cnt_ref[:, 0:1] + _dot(assigned.astype(BF16), earlier)
    r1 = jnp.sum(jnp.where(hit1, rank, 0.0), axis=0, keepdims=True).astype(jnp.int32)
    r2 = jnp.sum(jnp.where(hit2, rank, 0.0), axis=0, keepdims=True).astype(jnp.int32)
    cnt_ref[...] = cnt_ref[...] + jnp.sum(assigned, axis=1, keepdims=True)
    cnt_out_ref[...] = cnt_ref[...]

    row8 = lax.broadcasted_iota(jnp.int32, (8, TQ), 0)
    route_ref[0] = jnp.where(row8 == 0, i1, jnp.where(row8 == 1, i2, jnp.where(row8 == 2, r1,
                             jnp.where(row8 == 3, r2, 0))))
    wcol_ref[...] = jnp.where(row == 0, w1, jnp.where(row == 1, w2, 0.0)).T


def _post(o1, o2, o1_col, o2_col, w1, w2, x, mods, layer, ln_g, ln_b, wr, br):
    n, d = x.shape
    half = w1.shape[0]
    row = lambda i: (i, 0)
    fixed = lambda i: (0, 0)
    return pl.pallas_call(
        _post_kernel,
        grid=(n // TQ,),
        in_specs=[
            pl.BlockSpec((TQ, half), lambda i: (i, o1_col)),
            pl.BlockSpec((TQ, half), lambda i: (i, o2_col)),
            pl.BlockSpec((half, d), fixed),
            pl.BlockSpec((half, d), fixed),
            pl.BlockSpec((TQ, d), row),
            _mod_spec(layer),
            pl.BlockSpec((1, d), fixed),
            pl.BlockSpec((1, d), fixed),
            pl.BlockSpec((d, LANES), fixed),
            pl.BlockSpec((LANES, 1), fixed),
        ],
        out_specs=[
            pl.BlockSpec((TQ, d), row),
            pl.BlockSpec((TQ, d), row),
            pl.BlockSpec((1, 8, TQ), lambda i: (i, 0, 0)),
            pl.BlockSpec((TQ, LANES), row),
            pl.BlockSpec((LANES, LANES), fixed),
        ],
        out_shape=[
            jax.ShapeDtypeStruct((n, d), F32),
            jax.ShapeDtypeStruct((n, d), F32),
            jax.ShapeDtypeStruct((n // TQ, 8, TQ), jnp.int32),
            jax.ShapeDtypeStruct((n, LANES), F32),
            jax.ShapeDtypeStruct((LANES, LANES), F32),
        ],
        scratch_shapes=[pltpu.VMEM((LANES, LANES), F32)],
        compiler_params=_cparams("arbitrary"),
        name="post_attn",
    )(o1, o2, w1, w2, x, mods, ln_g, ln_b, wr, br)


def _moe_layout(route, counts, n):
    n_tiles_max = (2 * n) // MOE_TM + N_EXPERTS
    cnt = counts[:N_EXPERTS, 0].astype(jnp.int32)
    tiles = (cnt + MOE_TM - 1) // MOE_TM
    tiles_end = jnp.cumsum(tiles)
    seg_start = (tiles_end - tiles) * MOE_TM
    experts = jnp.arange(N_EXPERTS)

    def start_of(e):
        return jnp.sum(jnp.where(e[..., None] == experts, seg_start, 0), axis=-1)

    pos = jnp.stack([start_of(route[:, 0]) + route[:, 2],
                     start_of(route[:, 1]) + route[:, 3]], axis=1)
    n_used = tiles_end[-1]
    j = jnp.minimum(jnp.arange(n_tiles_max), n_used - 1)
    tile_expert = jnp.sum(j[:, None] >= tiles_end[None, :], axis=1).astype(jnp.int32)
    last_tiles = jnp.maximum(tiles_end - 1, 0)
    tail_tiles = jnp.minimum(n_used + experts, n_tiles_max - 1)
    live = jnp.concatenate([tiles > 0, n_used + experts < n_tiles_max])
    zero_tiles = jnp.concatenate([last_tiles, tail_tiles, live]).astype(jnp.int32)
    return pos.astype(jnp.int32), tile_expert, n_used.reshape(1).astype(jnp.int32), zero_tiles, n_tiles_max


def _scatter_kernel(zt_ref, pos_ref, h_ref, xs_ref, zero_ref, sem, zsem):
    def zero_copy(t):
        return pltpu.make_async_copy(zero_ref, xs_ref.at[pl.ds(zt_ref[t] * MOE_TM, MOE_TM)], zsem)

    @pl.when(pl.program_id(0) == 0)
    def _():
        zero_ref[...] = jnp.zeros(zero_ref.shape, F32)
        for t in range(2 * N_EXPERTS):
            @pl.when(zt_ref[2 * N_EXPERTS + t] != 0)
            def _():
                zero_copy(t).start()
        for t in range(2 * N_EXPERTS):
            @pl.when(zt_ref[2 * N_EXPERTS + t] != 0)
            def _():
                zero_copy(t).wait()

    def row_copy(r, k):
        return pltpu.make_async_copy(h_ref.at[pl.ds(r, 1)], xs_ref.at[pl.ds(pos_ref[0, k, r], 1)], sem)

    def start(r, carry):
        row_copy(r, 0).start()
        row_copy(r, 1).start()
        return carry

    def wait(r, carry):
        row_copy(r, 0).wait()
        row_copy(r, 1).wait()
        return carry

    lax.fori_loop(0, TQ, start, 0, unroll=8)
    lax.fori_loop(0, TQ, wait, 0, unroll=8)


def _scatter(zero_tiles, pos, h2, n_rows):
    n, d = h2.shape
    return pl.pallas_call(
        _scatter_kernel,
        grid_spec=pltpu.PrefetchScalarGridSpec(
            num_scalar_prefetch=1,
            grid=(n // TQ,),
            in_specs=[
                pl.BlockSpec((1, 2, TQ), lambda i, zt: (i, 0, 0), memory_space=pltpu.SMEM),
                pl.BlockSpec((TQ, d), lambda i, zt: (i, 0)),
            ],
            out_specs=pl.BlockSpec(memory_space=pl.ANY),
            scratch_shapes=[pltpu.VMEM((MOE_TM, d), F32), pltpu.SemaphoreType.DMA(()),
                            pltpu.SemaphoreType.DMA(())],
        ),
        out_shape=jax.ShapeDtypeStruct((n_rows, d), F32),
        compiler_params=_cparams("arbitrary"),
        name="moe_scatter",
    )(zero_tiles, pos, h2)


def _expert_kernel(te_ref, nt_ref, x_ref, wg_ref, wu_ref, wd_ref, y_ref, wgb_ref, wub_ref, wdb_ref):
    j = pl.program_id(0)
    used = j < nt_ref[0]
    new_expert = jnp.logical_or(j == 0, te_ref[j] != te_ref[jnp.maximum(j - 1, 0)])

    @pl.when(jnp.logical_and(used, new_expert))
    def _():
        wgb_ref[...] = wg_ref[0].astype(BF16)
        wub_ref[...] = wu_ref[0].astype(BF16)
        wdb_ref[...] = wd_ref[0].astype(BF16)

    @pl.when(used)
    def _():
        x = x_ref[...].astype(BF16)
        g = _dot(x, wgb_ref[...])
        u = _dot(x, wub_ref[...])
        a = (g * jax.nn.sigmoid(g)) * u
        y_ref[...] = _dot(a.astype(BF16), wdb_ref[...])

    @pl.when(jnp.logical_not(used))
    def _():
        y_ref[...] = jnp.zeros(y_ref.shape, F32)


def _experts(tile_expert, n_used, xs, wg, wu, wd, layer, n_tiles_max):
    _, d = xs.shape
    f = wg.shape[3]
    weight = lambda j, te, nt: (layer, te[j], 0, 0)
    return pl.pallas_call(
        _expert_kernel,
        grid_spec=pltpu.PrefetchScalarGridSpec(
            num_scalar_prefetch=2,
            grid=(n_tiles_max,),
            in_specs=[
                pl.BlockSpec((MOE_TM, d), lambda j, te, nt: (jnp.minimum(j, nt[0] - 1), 0)),
                pl.BlockSpec((None, 1, d, f), weight),
                pl.BlockSpec((None, 1, d, f), weight),
                pl.BlockSpec((None, 1, f, d), weight),
            ],
            out_specs=pl.BlockSpec((MOE_TM, d), lambda j, te, nt: (j, 0)),
            scratch_shapes=[pltpu.VMEM((d, f), BF16), pltpu.VMEM((d, f), BF16), pltpu.VMEM((f, d), BF16)],
        ),
        out_shape=jax.ShapeDtypeStruct(xs.shape, F32),
        compiler_params=_cparams("arbitrary"),
        name="moe_experts",
    )(tile_expert, n_used, xs, wg, wu, wd)


def _combine_kernel(pos_ref, pos_next_ref, ys_ref, x_ref, wcol_ref, mod_ref, lng_ref, lnb_ref, o_ref,
                    buf_ref, sem):
    i = pl.program_id(0)
    n_steps = pl.num_programs(0)
    slot = lax.rem(i, 2)

    def row_copy(p_ref, dst_slot, r, k):
        return pltpu.make_async_copy(ys_ref.at[pl.ds(p_ref[0, k, r], 1)],
                                     buf_ref.at[dst_slot, k, pl.ds(r, 1)], sem.at[dst_slot])

    def gather(p_ref, dst_slot):
        def start(r, carry):
            row_copy(p_ref, dst_slot, r, 0).start()
            row_copy(p_ref, dst_slot, r, 1).start()
            return carry
        lax.fori_loop(0, TQ, start, 0, unroll=8)

    @pl.when(i == 0)
    def _():
        gather(pos_ref, 0)

    @pl.when(i + 1 < n_steps)
    def _():
        gather(pos_next_ref, 1 - slot)

    def wait(r, carry):
        row_copy(pos_ref, slot, r, 0).wait()
        row_copy(pos_ref, slot, r, 1).wait()
        return carry
    lax.fori_loop(0, TQ, wait, 0, unroll=8)

    w = wcol_ref[...]
    moe = w[:, 0:1] * buf_ref[slot, 0] + w[:, 1:2] * buf_ref[slot, 1]
    g2 = mod_ref[0, 5:6, :]
    o_ref[...] = _layer_norm(ALPHA * x_ref[...] + g2 * moe, lng_ref[...], lnb_ref[...])


def _combine(pos, ys, x1, wcol, mods, layer, ln_g, ln_b, latents_only):
    n, d = x1.shape
    n_steps = n // TQ
    row = lambda i: (i, 0)
    fixed = lambda i: (0, 0)
    if latents_only:
        out_rows, out_map = n - CTX_LEN, lambda i: (jnp.maximum(i - CTX_LEN // TQ, 0), 0)
    else:
        out_rows, out_map = n, row
    return pl.pallas_call(
        _combine_kernel,
        grid=(n_steps,),
        in_specs=[
            pl.BlockSpec((1, 2, TQ), lambda i: (i, 0, 0), memory_space=pltpu.SMEM),
            pl.BlockSpec((1, 2, TQ), lambda i: (jnp.minimum(i + 1, n_steps - 1), 0, 0), memory_space=pltpu.SMEM),
            pl.BlockSpec(memory_space=pl.ANY),
            pl.BlockSpec((TQ, d), row),
            pl.BlockSpec((TQ, LANES), row),
            _mod_spec(layer),
            pl.BlockSpec((1, d), fixed),
            pl.BlockSpec((1, d), fixed),
        ],
        out_specs=pl.BlockSpec((TQ, d), out_map),
        out_shape=jax.ShapeDtypeStruct((out_rows, d), F32),
        scratch_shapes=[pltpu.VMEM((2, 2, TQ, d), F32), pltpu.SemaphoreType.DMA((2,))],
        compiler_params=_cparams("arbitrary"),
        name="moe_combine",
    )(pos, pos, ys, x1, wcol, mods, ln_g, ln_b)


def _moe(h2, route, wcol, counts, wg, wu, wd, x1, mods, layer, ln_g, ln_b, latents_only):
    n = h2.shape[0]
    pos, tile_expert, n_used, zero_tiles, n_tiles_max = _moe_layout(route, counts, n)
    xs = _scatter(zero_tiles, pos, h2, n_tiles_max * MOE_TM)
    ys = _experts(tile_expert, n_used, xs, wg, wu, wd, layer, n_tiles_max)
    return _combine(pos, ys, x1, wcol, mods, layer, ln_g, ln_b, latents_only)


def _rope_tables(n_lat, head_dim):
    quarter = head_dim // 4
    t = jnp.arange(n_lat)
    inv_freq = ROPE_THETA ** (-jnp.arange(quarter, dtype=F32) / quarter)
    ang_r = (t // GRID_W)[:, None].astype(F32) * inv_freq
    ang_c = (t % GRID_W)[:, None].astype(F32) * inv_freq
    cos = jnp.concatenate([jnp.cos(ang_r)] * 2 + [jnp.cos(ang_c)] * 2, axis=1)
    sin = jnp.concatenate([-jnp.sin(ang_r), jnp.sin(ang_r), -jnp.sin(ang_c), jnp.sin(ang_c)], axis=1)
    reps = LANES // head_dim
    cos = jnp.tile(cos, (1, reps))
    sin = jnp.tile(sin, (1, reps))
    cos = jnp.concatenate([jnp.ones((CTX_LEN, LANES), F32), cos], axis=0)
    sin = jnp.concatenate([jnp.zeros((CTX_LEN, LANES), F32), sin], axis=0)
    return cos, sin


def _a_head_order():
    grp = A_HQ // A_HKV
    order = []
    for j in range(grp):
        for g in range(A_HKV):
            order.append(g * grp + j)
    return order


def kernel(x, c, ctx, c_ctx, w_mod, b_mod, ln_g, ln_b, w_in_even, w_out_even, sink_logits, lam_q1, lam_k1,
           lam_q2, lam_k2, subln_g, w_in_odd, w_out_odd, q_norm_g, k_norm_g, w_router, b_router, w_gate,
           w_up, w_down):
    n_lat = x.shape[1]
    assert x.shape == (1, n_lat, D_MODEL) and ctx.shape == (1, CTX_LEN, D_MODEL)
    assert n_lat % max(UNROLL * TK, MOE_TM) == 0 and n_lat % GRID_W == 0
    xs = jnp.concatenate([ctx[0], x[0]], axis=0)

    c8 = jnp.zeros((8, D_MODEL), F32).at[0].set(c_ctx).at[1].set(c[0])
    mods = _modulation(c8, w_mod, b_mod)

    cos64, sin64 = _rope_tables(n_lat, A_HD)
    cos128, sin128 = _rope_tables(n_lat, C_HD)

    order = _a_head_order()
    a_cols = jnp.concatenate([jnp.arange(h * A_HD, (h + 1) * A_HD) for h in order])
    wq_a = A_HQ * A_HD
    w_in_even_p = jnp.concatenate([w_in_even[:, :, :wq_a][:, :, a_cols], w_in_even[:, :, wq_a:]], axis=2).astype(BF16)
    w_out_even_p = jnp.concatenate([w_out_even[:, :wq_a][:, a_cols], w_out_even[:, wq_a:]], axis=1).astype(BF16)
    w_in_odd_b = w_in_odd.astype(BF16)
    w_out_odd_b = w_out_odd.astype(BF16)
    wg, wu, wd = w_gate, w_up, w_down
    wr = jnp.pad(w_router, ((0, 0), (0, LANES - N_EXPERTS)))
    br = jnp.pad(b_router.astype(F32), (0, LANES - N_EXPERTS)).reshape(LANES, 1)

    segs_even = ((A_HQ * A_HD, True, None, (A_HD ** -0.5) * LOG2E, None), (A_HKV * A_HD, True, None, None, None),
                 (A_HKV * A_HD, False, None, None, "ones"),
                 (B_H * 2 * B_HD, True, None, (B_HD ** -0.5) * LOG2E, None),
                 (B_H * 2 * B_HD, True, None, None, None), (B_H * B_VD, False, None, None, "transposed"))
    segs_odd = ((C_HQ * C_HD, True, 0, (C_HD ** -0.5) * LOG2E, None), (C_HKV * C_HD, True, 1, None, None),
                (C_HKV * C_HD, False, None, None, "transposed"))

    for l in range(DEPTH):
        i = l // 2
        lng = ln_g[l]
        lnb = ln_b[l]
        if l % 2 == 0:
            lam_init = 0.8 - 0.6 * math.exp(-0.3 * l)
            aq, ak, av, bq, bk, bv = _inproj(xs, mods, l, w_in_even_p[i], cos64, sin64, segs_even, A_HD // 4)
            sink_rows = jnp.broadcast_to(
                (sink_logits[i][jnp.array(order)] * LOG2E)[:, None], (A_HQ, LANES)).astype(F32)
            oa = _attn_a(aq, ak, av, sink_rows)
            ob = _attn_b(bq, bk, bv, lam_q1[i][None], lam_k1[i][None], lam_q2[i][None], lam_k2[i][None],
                         subln_g[i][None], lam_init)
            half = wq_a
            routed = _post(oa, ob, 0, 0, w_out_even_p[i][:half], w_out_even_p[i][half:], xs, mods, l,
                           lng[0][None], lnb[0][None], wr, br)
        else:
            q, k, v = _inproj(xs, mods, l, w_in_odd_b[i], cos128, sin128, segs_odd, C_HD // 4,
                              gains=(q_norm_g[i][None], k_norm_g[i][None]))
            o = _attn_c(q, k, v)
            half = C_HQ * C_HD // 2
            routed = _post(o, o, 0, 1, w_out_odd_b[i][:half], w_out_odd_b[i][half:], xs, mods, l,
                           lng[0][None], lnb[0][None], wr, br)
        x1, h2, route, wcol, counts = routed
        xs = _moe(h2, route, wcol, counts, wg, wu, wd, x1, mods, l, lng[1][None], lnb[1][None],
                  latents_only=(l == DEPTH - 1))
    return xs[None]
```

```python
import functools
import math

import jax
import jax.numpy as jnp
from jax import lax
from jax.experimental import pallas as pl
from jax.experimental.pallas import tpu as pltpu

D_MODEL = 1024
SEQ = 16384
DEPTH = 4
GRID_W = 64
CTX_LEN = 256
WINDOW = 128
ROPE_THETA = 10000.0
A_HQ, A_HKV, A_HD = 8, 2, 64
B_H, B_HD = 4, 64
B_VD = 2 * B_HD
C_HQ, C_HKV, C_HD = 8, 2, 128
N_EXPERTS = 16
N_GROUPS = 4
EXPERTS_PER_GROUP = N_EXPERTS // N_GROUPS
D_EXPERT = 512
ALPHA = (2 * DEPTH) ** 0.25
LN_EPS = 1e-5
QK_EPS = 1e-6
SUBLN_EPS = 1e-5

LANES = 128
TQ = CTX_LEN
TK = 512
UNROLL = 4
MOE_TM = 256
VMEM_LIMIT = 56 * 1024 * 1024
LOG2E = 1.4426950408889634
BF16 = jnp.bfloat16
F32 = jnp.float32


def _cparams(*sem):
    return pltpu.CompilerParams(dimension_semantics=sem, vmem_limit_bytes=VMEM_LIMIT)


def _dot(a, b):
    return jnp.dot(a, b, preferred_element_type=F32)


def _dot_nt(a, b):
    return lax.dot_general(a, b, (((1,), (1,)), ((), ())), preferred_element_type=F32)


def _lane_tile(x, width):
    reps = width // x.shape[-1]
    return x if reps == 1 else jnp.concatenate([x] * reps, axis=-1)


def _mod_kernel(c_ref, w_ref, b_ref, o_ref):
    c = c_ref[...]
    s = c * jax.nn.sigmoid(c)
    o_ref[0] = _dot(s.astype(BF16), w_ref[0].astype(BF16)) + b_ref[0]


def _modulation(c8, w_mod, b_mod):
    depth, d, n6 = w_mod.shape
    tn = 1536
    out = pl.pallas_call(
        _mod_kernel,
        grid=(depth, n6 // tn),
        in_specs=[
            pl.BlockSpec((8, d), lambda l, j: (0, 0)),
            pl.BlockSpec((1, d, tn), lambda l, j: (l, 0, j)),
            pl.BlockSpec((1, 1, tn), lambda l, j: (l, 0, j)),
        ],
        out_specs=pl.BlockSpec((1, 8, tn), lambda l, j: (l, 0, j)),
        out_shape=jax.ShapeDtypeStruct((depth, 8, n6), F32),
        compiler_params=_cparams("arbitrary", "arbitrary"),
        name="modulation",
    )(c8, w_mod, b_mod.reshape(depth, 1, n6))
    m = out[:, :2].reshape(depth, 2, 6, d)
    m = jnp.pad(m, ((0, 0), (0, 0), (0, 2), (0, 0)))
    return m.reshape(depth * 2, 8, d)


def _mod_spec(layer):
    return pl.BlockSpec((1, 8, D_MODEL), lambda i, *_: (2 * layer + jnp.minimum(i, 1), 0, 0))


def _rope_chunk(z, cos, sin, quarter):
    lane = lax.broadcasted_iota(jnp.int32, z.shape, 1)
    first = (lane % (2 * quarter)) < quarter
    nxt = pltpu.roll(z, LANES - quarter, axis=1)
    prv = pltpu.roll(z, quarter, axis=1)
    return z * cos + jnp.where(first, nxt, prv) * sin


def _inproj_kernel(segs, quarter, n_gain, x_ref, mod_ref, w_ref, cos_ref, sin_ref, *rest):
    gains = rest[:n_gain]
    outs = rest[n_gain:]
    sh = mod_ref[0, 0:1, :]
    sc = mod_ref[0, 1:2, :]
    hb = (x_ref[...] * (1.0 + sc) + sh).astype(BF16)
    cos = cos_ref[...]
    sin = sin_ref[...]
    off = 0
    for (width, rope, gain_idx, scale, form), o_ref in zip(segs, outs):
        z = _dot(hb, w_ref[:, off:off + width])
        for c in range(width // LANES):
            zc = z[:, c * LANES:(c + 1) * LANES]
            if gain_idx is not None:
                ms = jnp.mean(zc * zc, axis=-1, keepdims=True)
                zc = zc * lax.rsqrt(ms + QK_EPS) * gains[gain_idx][...]
            if rope:
                zc = _rope_chunk(zc, cos, sin, quarter)
            if scale is not None:
                zc = zc * scale
            if form == "ones":
                o_ref[:, 2 * c * LANES:(2 * c + 1) * LANES] = zc.astype(BF16)
                o_ref[:, (2 * c + 1) * LANES:(2 * c + 2) * LANES] = jnp.ones(zc.shape, BF16)
            elif form == "transposed":
                o_ref[c, 0] = zc.T.astype(BF16)
            else:
                o_ref[:, c * LANES:(c + 1) * LANES] = zc.astype(BF16)
        off += width


def _inproj(x, mods, layer, w, cos, sin, segs, quarter, gains=()):
    n, d = x.shape
    wtot = w.shape[1]
    row = lambda i: (i, 0)
    fixed = lambda i: (0, 0)
    in_specs = [
        pl.BlockSpec((TQ, d), row),
        _mod_spec(layer),
        pl.BlockSpec((d, wtot), fixed),
        pl.BlockSpec((TQ, LANES), row),
        pl.BlockSpec((TQ, LANES), row),
    ] + [pl.BlockSpec((1, LANES), fixed) for _ in gains]
    out_specs, out_shape = [], []
    for width, _, _, _, form in segs:
        if form == "transposed":
            heads = width // LANES
            out_specs.append(pl.BlockSpec((heads, 1, LANES, TQ), lambda i: (0, i, 0, 0)))
            out_shape.append(jax.ShapeDtypeStruct((heads, n // TQ, LANES, TQ), BF16))
        else:
            w_ = width * (2 if form == "ones" else 1)
            out_specs.append(pl.BlockSpec((TQ, w_), row))
            out_shape.append(jax.ShapeDtypeStruct((n, w_), BF16))
    return pl.pallas_call(
        functools.partial(_inproj_kernel, segs, quarter, len(gains)),
        grid=(n // TQ,),
        in_specs=in_specs,
        out_specs=out_specs,
        out_shape=out_shape,
        compiler_params=_cparams("arbitrary"),
        name="inproj",
    )(x, mods, w, cos, sin, *gains)


def _flash(qs_ref, k_ref, vt_ref, m_ref, l_ref, acc_ref, s_refs, is_ctx_block):
    n_chunks = (k_ref.shape[0] - CTX_LEN) // TK
    n_sets = k_ref.shape[1] // LANES
    cols = qs_ref.shape[0] // n_sets
    m_ref[...] = jnp.full(m_ref.shape, -jnp.inf, F32)
    l_ref[...] = jnp.zeros(l_ref.shape, F32)
    acc_ref[...] = jnp.zeros(acc_ref.shape, F32)

    def scores(slot, start, size):
        for g in range(n_sets):
            cs = slice(g * cols, (g + 1) * cols)
            s_refs[slot][0:size, cs] = _dot_nt(k_ref[pl.ds(start, size), g * LANES:(g + 1) * LANES], qs_ref[cs, :])

    def update(slot, first_block, n_blocks):
        s = s_refs[slot][0:n_blocks * TQ, :]
        m_prev = m_ref[0:1, :]
        m_new = jnp.maximum(m_prev, jnp.max(s, axis=0, keepdims=True))
        p = jnp.exp2(s - m_new)
        alpha = jnp.exp2(m_prev - m_new)
        l_ref[...] = jnp.broadcast_to(alpha * l_ref[0:1, :] + jnp.sum(p, axis=0, keepdims=True), l_ref.shape)
        pb = p.astype(BF16)
        for g in range(n_sets):
            cs = slice(g * cols, (g + 1) * cols)
            pv = _dot(vt_ref[g, first_block], pb[0:TQ, cs])
            for b in range(1, n_blocks):
                pv = pv + _dot(vt_ref[g, first_block + b], pb[b * TQ:(b + 1) * TQ, cs])
            acc_ref[:, cs] = acc_ref[:, cs] * alpha[:, cs] + pv
        m_ref[...] = jnp.broadcast_to(m_new, m_ref.shape)

    def lat(j):
        return pl.multiple_of(CTX_LEN + j * TK, CTX_LEN)

    blocks_per_chunk = TK // TQ
    ctx_blocks = CTX_LEN // TQ
    scores(1, 0, CTX_LEN)
    update(1, 0, ctx_blocks)

    @pl.when(jnp.logical_not(is_ctx_block))
    def _():
        scores(0, lat(0), TK)

        def trip(jj, carry):
            j = UNROLL * jj
            for u in range(UNROLL):
                scores((u + 1) % 2, lat(jnp.minimum(j + u + 1, n_chunks - 1)), TK)
                update(u % 2, ctx_blocks + (j + u) * blocks_per_chunk, blocks_per_chunk)
            return carry

        lax.fori_loop(0, n_chunks // UNROLL, trip, 0)


def _flash_out(l_ref, acc_ref, idx):
    cs = slice(idx * TQ, (idx + 1) * TQ)
    return (acc_ref[:, cs] / l_ref[0:1, cs]).T


def _flash_scratch(n_stack):
    rows = n_stack * TQ
    return [
        pltpu.VMEM((rows, LANES), BF16),
        pltpu.VMEM((8, rows), F32),
        pltpu.VMEM((8, rows), F32),
        pltpu.VMEM((LANES, rows), F32),
        pltpu.VMEM((TK, rows), F32),
        pltpu.VMEM((TK, rows), F32),
    ]


def _attn_c_kernel(q_ref, k_ref, vt_ref, o_ref, qs_ref, m_ref, l_ref, acc_ref, s0_ref, s1_ref):
    grp = C_HQ // C_HKV
    for h in range(grp):
        qs_ref[h * TQ:(h + 1) * TQ, :] = q_ref[:, h * C_HD:(h + 1) * C_HD]
    _flash(qs_ref, k_ref, vt_ref, m_ref, l_ref, acc_ref, (s0_ref, s1_ref), pl.program_id(1) == 0)
    for h in range(grp):
        o_ref[:, h * C_HD:(h + 1) * C_HD] = _flash_out(l_ref, acc_ref, h).astype(BF16)


def _attn_c(q, k, vt):
    n = q.shape[0]
    grp = C_HQ // C_HKV
    wq = grp * C_HD
    return pl.pallas_call(
        _attn_c_kernel,
        grid=(C_HKV, n // TQ),
        in_specs=[
            pl.BlockSpec((TQ, wq), lambda g, i: (i, g)),
            pl.BlockSpec((n, C_HD), lambda g, i: (0, g), pipeline_mode=pl.Buffered(1)),
            pl.BlockSpec((1, n // TQ, C_HD, TQ), lambda g, i: (g, 0, 0, 0), pipeline_mode=pl.Buffered(1)),
        ],
        out_specs=pl.BlockSpec((TQ, wq), lambda g, i: (i, g)),
        out_shape=jax.ShapeDtypeStruct((n, C_HQ * C_HD), BF16),
        scratch_shapes=_flash_scratch(grp),
        compiler_params=_cparams("arbitrary", "arbitrary"),
        name="attn_c",
    )(q, k, vt)


B_HEADS_PER_STEP = 2


def _attn_b_kernel(lam_init, q_ref, k_ref, vt_ref, lq1_ref, lk1_ref, lq2_ref, lk2_ref, g_ref, o_ref,
                   qs_ref, m_ref, l_ref, acc_ref, s0_ref, s1_ref):
    for h in range(B_HEADS_PER_STEP):
        q = q_ref[:, h * B_VD:(h + 1) * B_VD]
        lane = lax.broadcasted_iota(jnp.int32, q.shape, 1)
        zero = jnp.zeros_like(q)
        qs_ref[(2 * h) * TQ:(2 * h + 1) * TQ, :] = jnp.where(lane < B_HD, q, zero)
        qs_ref[(2 * h + 1) * TQ:(2 * h + 2) * TQ, :] = jnp.where(lane >= B_HD, q, zero)
    _flash(qs_ref, k_ref, vt_ref, m_ref, l_ref, acc_ref, (s0_ref, s1_ref), pl.program_id(1) == 0)
    lam = (jnp.exp(jnp.sum(lq1_ref[...] * lk1_ref[...], axis=-1, keepdims=True))
           - jnp.exp(jnp.sum(lq2_ref[...] * lk2_ref[...], axis=-1, keepdims=True))) + lam_init
    for h in range(B_HEADS_PER_STEP):
        o = _flash_out(l_ref, acc_ref, 2 * h) - lam * _flash_out(l_ref, acc_ref, 2 * h + 1)
        ms = jnp.mean(o * o, axis=-1, keepdims=True)
        o = o * lax.rsqrt(ms + SUBLN_EPS) * g_ref[...] * (1.0 - lam_init)
        o_ref[:, h * B_VD:(h + 1) * B_VD] = o.astype(BF16)


def _attn_b(q, k, vt, lq1, lk1, lq2, lk2, subln_g, lam_init):
    n = q.shape[0]
    hps = B_HEADS_PER_STEP
    blk = lambda h, i: (i, h)
    col = lambda h, i: (0, h)
    fixed = lambda h, i: (0, 0)
    small = pl.BlockSpec((1, B_HD), fixed)
    return pl.pallas_call(
        functools.partial(_attn_b_kernel, lam_init),
        grid=(B_H // hps, n // TQ),
        in_specs=[
            pl.BlockSpec((TQ, hps * B_VD), blk),
            pl.BlockSpec((n, hps * B_VD), col, pipeline_mode=pl.Buffered(1)),
            pl.BlockSpec((hps, n // TQ, B_VD, TQ), lambda h, i: (h, 0, 0, 0), pipeline_mode=pl.Buffered(1)),
            small, small, small, small,
            pl.BlockSpec((1, B_VD), fixed),
        ],
        out_specs=pl.BlockSpec((TQ, hps * B_VD), blk),
        out_shape=jax.ShapeDtypeStruct((n, B_H * B_VD), BF16),
        scratch_shapes=_flash_scratch(2 * hps),
        compiler_params=_cparams("arbitrary", "arbitrary"),
        name="attn_b",
    )(q, k, vt, lq1, lk1, lq2, lk2, subln_g)


A_WIN_KEYS = TQ + 2 * WINDOW


def _attn_a_kernel(q_ref, k_ref, v_ref, sink_ref, o_ref, qs_ref):
    i = pl.program_id(0)
    n_lat = k_ref.shape[0] - CTX_LEN
    n_blk = A_HQ * A_HD // LANES
    for j in range(n_blk):
        q = q_ref[:, j * LANES:(j + 1) * LANES]
        lane = lax.broadcasted_iota(jnp.int32, q.shape, 1)
        zero = jnp.zeros_like(q)
        qs_ref[(2 * j) * TQ:(2 * j + 1) * TQ, :] = jnp.where(lane < A_HD, q, zero)
        qs_ref[(2 * j + 1) * TQ:(2 * j + 2) * TQ, :] = jnp.where(lane >= A_HD, q, zero)
    sink = jnp.concatenate([jnp.broadcast_to(sink_ref[r:r + 1, :], (TQ, LANES)) for r in range(A_HQ)], axis=0)
    s_c = _dot_nt(qs_ref[...], k_ref[0:CTX_LEN, :])
    vc = v_ref[0:CTX_LEN, :]

    def finish(m, acc):
        o = acc[:, :LANES] / (acc[:, LANES:] + jnp.exp2(sink - m))
        lane = lax.broadcasted_iota(jnp.int32, (TQ, LANES), 1)
        for j in range(n_blk):
            lo = o[(2 * j) * TQ:(2 * j + 1) * TQ, :]
            hi = o[(2 * j + 1) * TQ:(2 * j + 2) * TQ, :]
            o_ref[:, j * LANES:(j + 1) * LANES] = jnp.where(lane < A_HD, lo, hi).astype(BF16)

    @pl.when(i == 0)
    def _():
        m = jnp.maximum(sink, jnp.max(s_c, axis=1, keepdims=True))
        p_c = jnp.exp2(s_c - _lane_tile(m, CTX_LEN)).astype(BF16)
        finish(m, _dot(p_c, vc))

    @pl.when(i > 0)
    def _():
        q0 = (i - 1) * TQ
        ws = jnp.clip(q0 - WINDOW, 0, n_lat - A_WIN_KEYS)
        ws = pl.multiple_of(ws, WINDOW)
        qpos = q0 + lax.broadcasted_iota(jnp.int32, (TQ, A_WIN_KEYS), 0)
        kpos = ws + lax.broadcasted_iota(jnp.int32, (TQ, A_WIN_KEYS), 1)
        ok = jnp.abs(qpos - kpos) <= WINDOW
        s_l = _dot_nt(qs_ref[...], k_ref[pl.ds(CTX_LEN + ws, A_WIN_KEYS), :])
        s_l = jnp.where(ok[None], s_l.reshape(A_HQ, TQ, A_WIN_KEYS), -jnp.inf).reshape(A_HQ * TQ, A_WIN_KEYS)
        m = jnp.maximum(sink, jnp.maximum(jnp.max(s_c, axis=1, keepdims=True),
                                          jnp.max(s_l, axis=1, keepdims=True)))
        p_c = jnp.exp2(s_c - _lane_tile(m, CTX_LEN)).astype(BF16)
        p_l = jnp.exp2(s_l - _lane_tile(m, A_WIN_KEYS)).astype(BF16)
        finish(m, _dot(p_c, vc) + _dot(p_l, v_ref[pl.ds(CTX_LEN + ws, A_WIN_KEYS), :]))


def _attn_a(q, k, v_aug, sink_rows):
    n = q.shape[0]
    wq = A_HQ * A_HD
    wkv = A_HKV * A_HD
    return pl.pallas_call(
        _attn_a_kernel,
        grid=(n // TQ,),
        in_specs=[
            pl.BlockSpec((TQ, wq), lambda i: (i, 0)),
            pl.BlockSpec((n, wkv), lambda i: (0, 0)),
            pl.BlockSpec((n, 2 * wkv), lambda i: (0, 0)),
            pl.BlockSpec((A_HQ, LANES), lambda i: (0, 0)),
        ],
        out_specs=pl.BlockSpec((TQ, wq), lambda i: (i, 0)),
        out_shape=jax.ShapeDtypeStruct((n, wq), BF16),
        scratch_shapes=[pltpu.VMEM((A_HQ * TQ, LANES), BF16)],
        compiler_params=_cparams("arbitrary"),
        name="attn_a",
    )(q, k, v_aug, sink_rows)


def _layer_norm(z, g, b):
    mu = jnp.mean(z, axis=-1, keepdims=True)
    zc = z - mu
    var = jnp.mean(zc * zc, axis=-1, keepdims=True)
    return zc * lax.rsqrt(var + LN_EPS) * g + b


def _route(logits_t, bias_t):
    t = logits_t.shape[1]
    score = [jax.nn.sigmoid(logits_t[e:e + 1, :]) for e in range(N_EXPERTS)]
    sel = [score[e] + bias_t[e:e + 1, :] for e in range(N_EXPERTS)]
    best_g = jnp.zeros((1, t), jnp.int32)
    best_s = None
    for g in range(N_GROUPS):
        a = sel[g * EXPERTS_PER_GROUP:(g + 1) * EXPERTS_PER_GROUP]
        gs = None
        for x in range(EXPERTS_PER_GROUP):
            for y in range(x + 1, EXPERTS_PER_GROUP):
                pair = a[x] + a[y]
                gs = pair if gs is None else jnp.maximum(gs, pair)
        if best_s is None:
            best_s = gs
        else:
            upd = gs > best_s
            best_g = jnp.where(upd, g, best_g)
            best_s = jnp.where(upd, gs, best_s)
    neg = jnp.full((1, t), -jnp.inf, F32)
    masked = [jnp.where(best_g == (e // EXPERTS_PER_GROUP), sel[e], neg) for e in range(N_EXPERTS)]
    i1 = jnp.zeros((1, t), jnp.int32)
    v1 = masked[0]
    for e in range(1, N_EXPERTS):
        upd = masked[e] > v1
        i1 = jnp.where(upd, e, i1)
        v1 = jnp.where(upd, masked[e], v1)
    i2 = jnp.full((1, t), -1, jnp.int32)
    v2 = neg
    for e in range(N_EXPERTS):
        upd = (masked[e] > v2) & (i1 != e)
        i2 = jnp.where(upd, e, i2)
        v2 = jnp.where(upd, masked[e], v2)
    zero = jnp.zeros((1, t), F32)
    w1 = zero
    w2 = zero
    for e in range(N_EXPERTS):
        w1 = w1 + jnp.where(i1 == e, score[e], zero)
        w2 = w2 + jnp.where(i2 == e, score[e], zero)
    den = w1 + w2
    return i1, i2, w1 / den, w2 / den


def _post_kernel(o1_ref, o2_ref, w1_ref, w2_ref, x_ref, mod_ref, lng_ref, lnb_ref, wr_ref, br_ref,
                 x1_ref, h2_ref, route_ref, wcol_ref, cnt_out_ref, cnt_ref):
    @pl.when(pl.program_id(0) == 0)
    def _():
        cnt_ref[...] = jnp.zeros(cnt_ref.shape, F32)

    g1 = mod_ref[0, 2:3, :]
    sh2 = mod_ref[0, 3:4, :]
    sc2 = mod_ref[0, 4:5, :]
    y = _dot(o1_ref[...], w1_ref[...]) + _dot(o2_ref[...], w2_ref[...])
    x1 = _layer_norm(ALPHA * x_ref[...] + g1 * y, lng_ref[...], lnb_ref[...])
    x1_ref[...] = x1
    h2 = x1 * (1.0 + sc2) + sh2
    h2_ref[...] = h2
    h_hi = h2.astype(BF16)
    h_lo = (h2 - h_hi.astype(F32)).astype(BF16)
    wr = wr_ref[...]
    w_hi = wr.astype(BF16)
    w_lo = (wr - w_hi.astype(F32)).astype(BF16)
    logits = _dot(h_hi, w_hi) + (_dot(h_hi, w_lo) + _dot(h_lo, w_hi))
    i1, i2, w1, w2 = _route(logits.T, br_ref[...])

    row = lax.broadcasted_iota(jnp.int32, (LANES, TQ), 0)
    hit1 = row == i1
    hit2 = row == i2
    assigned = jnp.where(hit1 | hit2, 1.0, 0.0)
    t_from = lax.broadcasted_iota(jnp.int32, (TQ, TQ), 0)
    t_to = lax.broadcasted_iota(jnp.int32, (TQ, TQ), 1)
    earlier = jnp.where(t_from < t_to, 1.0, 0.0).astype(BF16)
    rank = cnt_ref[:, 0:1] + _dot(assigned.astype(BF16), earlier)
    r1 = jnp.sum(jnp.where(hit1, rank, 0.0), axis=0, keepdims=True).astype(jnp.int32)
    r2 = jnp.sum(jnp.where(hit2, rank, 0.0), axis=0, keepdims=True).astype(jnp.int32)
    cnt_ref[...] = cnt_ref[...] + jnp.sum(assigned, axis=1, keepdims=True)
    cnt_out_ref[...] = cnt_ref[...]

    row8 = lax.broadcasted_iota(jnp.int32, (8, TQ), 0)
    route_ref[0] = jnp.where(row8 == 0, i1, jnp.where(row8 == 1, i2, jnp.where(row8 == 2, r1,
                             jnp.where(row8 == 3, r2, 0))))
    wcol_ref[...] = jnp.where(row == 0, w1, jnp.where(row == 1, w2, 0.0)).T


def _post(o1, o2, o1_col, o2_col, w1, w2, x, mods, layer, ln_g, ln_b, wr, br):
    n, d = x.shape
    half = w1.shape[0]
    row = lambda i: (i, 0)
    fixed = lambda i: (0, 0)
    return pl.pallas_call(
        _post_kernel,
        grid=(n // TQ,),
        in_specs=[
            pl.BlockSpec((TQ, half), lambda i: (i, o1_col)),
            pl.BlockSpec((TQ, half), lambda i: (i, o2_col)),
            pl.BlockSpec((half, d), fixed),
            pl.BlockSpec((half, d), fixed),
            pl.BlockSpec((TQ, d), row),
            _mod_spec(layer),
            pl.BlockSpec((1, d), fixed),
            pl.BlockSpec((1, d), fixed),
            pl.BlockSpec((d, LANES), fixed),
            pl.BlockSpec((LANES, 1), fixed),
        ],
        out_specs=[
            pl.BlockSpec((TQ, d), row),
            pl.BlockSpec((TQ, d), row),
            pl.BlockSpec((1, 8, TQ), lambda i: (i, 0, 0)),
            pl.BlockSpec((TQ, LANES), row),
            pl.BlockSpec((LANES, LANES), fixed),
        ],
        out_shape=[
            jax.ShapeDtypeStruct((n, d), F32),
            jax.ShapeDtypeStruct((n, d), F32),
            jax.ShapeDtypeStruct((n // TQ, 8, TQ), jnp.int32),
            jax.ShapeDtypeStruct((n, LANES), F32),
            jax.ShapeDtypeStruct((LANES, LANES), F32),
        ],
        scratch_shapes=[pltpu.VMEM((LANES, LANES), F32)],
        compiler_params=_cparams("arbitrary"),
        name="post_attn",
    )(o1, o2, w1, w2, x, mods, ln_g, ln_b, wr, br)


def _moe_layout(route, counts, n):
    n_tiles_max = (2 * n) // MOE_TM + N_EXPERTS
    cnt = counts[:N_EXPERTS, 0].astype(jnp.int32)
    tiles = (cnt + MOE_TM - 1) // MOE_TM
    tiles_end = jnp.cumsum(tiles)
    seg_start = (tiles_end - tiles) * MOE_TM
    experts = jnp.arange(N_EXPERTS)

    def start_of(e):
        return jnp.sum(jnp.where(e[..., None] == experts, seg_start, 0), axis=-1)

    pos = jnp.concatenate([start_of(route[:, 0]) + route[:, 2],
                           start_of(route[:, 1]) + route[:, 3]], axis=1)[:, None, :]
    n_used = tiles_end[-1]
    j = jnp.minimum(jnp.arange(n_tiles_max), n_used - 1)
    tile_expert = jnp.sum(j[:, None] >= tiles_end[None, :], axis=1).astype(jnp.int32)
    last_tiles = jnp.maximum(tiles_end - 1, 0)
    tail_tiles = jnp.minimum(n_used + experts, n_tiles_max - 1)
    live = jnp.concatenate([tiles > 0, n_used + experts < n_tiles_max])
    zero_tiles = jnp.concatenate([last_tiles, tail_tiles, live]).astype(jnp.int32)
    return pos.astype(jnp.int32), tile_expert, n_used.reshape(1).astype(jnp.int32), zero_tiles, n_tiles_max


def _for_rows(fn):
    sublanes = 8

    def group(t, carry):
        base = pl.multiple_of(t * sublanes, sublanes)
        for u in range(sublanes):
            fn(base + u)
        return carry

    lax.fori_loop(0, TQ // sublanes, group, 0)


def _scatter_kernel(zt_ref, pos_ref, h_ref, xs_ref, zero_ref, sem, zsem):
    def zero_copy(t):
        return pltpu.make_async_copy(zero_ref, xs_ref.at[pl.ds(zt_ref[t] * MOE_TM, MOE_TM)], zsem)

    @pl.when(pl.program_id(0) == 0)
    def _():
        zero_ref[...] = jnp.zeros(zero_ref.shape, F32)
        for t in range(2 * N_EXPERTS):
            @pl.when(zt_ref[2 * N_EXPERTS + t] != 0)
            def _():
                zero_copy(t).start()
        for t in range(2 * N_EXPERTS):
            @pl.when(zt_ref[2 * N_EXPERTS + t] != 0)
            def _():
                zero_copy(t).wait()

    def row_copy(r, k):
        return pltpu.make_async_copy(h_ref.at[pl.ds(r, 1)], xs_ref.at[pl.ds(pos_ref[0, 0, k * TQ + r], 1)], sem)

    _for_rows(lambda r: (row_copy(r, 0).start(), row_copy(r, 1).start()))
    _for_rows(lambda r: (row_copy(r, 0).wait(), row_copy(r, 1).wait()))


def _scatter(zero_tiles, pos, h2, n_rows):
    n, d = h2.shape
    return pl.pallas_call(
        _scatter_kernel,
        grid_spec=pltpu.PrefetchScalarGridSpec(
            num_scalar_prefetch=1,
            grid=(n // TQ,),
            in_specs=[
                pl.BlockSpec((1, 1, 2 * TQ), lambda i, zt: (i, 0, 0), memory_space=pltpu.SMEM),
                pl.BlockSpec((TQ, d), lambda i, zt: (i, 0)),
            ],
            out_specs=pl.BlockSpec(memory_space=pl.ANY),
            scratch_shapes=[pltpu.VMEM((MOE_TM, d), F32), pltpu.SemaphoreType.DMA(()),
                            pltpu.SemaphoreType.DMA(())],
        ),
        out_shape=jax.ShapeDtypeStruct((n_rows, d), F32),
        compiler_params=_cparams("arbitrary"),
        name="moe_scatter",
    )(zero_tiles, pos, h2)


def _expert_kernel(te_ref, nt_ref, x_ref, wg_ref, wu_ref, wd_ref, y_ref, wgb_ref, wub_ref, wdb_ref):
    j = pl.program_id(0)
    used = j < nt_ref[0]
    new_expert = jnp.logical_or(j == 0, te_ref[j] != te_ref[jnp.maximum(j - 1, 0)])

    @pl.when(jnp.logical_and(used, new_expert))
    def _():
        wgb_ref[...] = wg_ref[0].astype(BF16)
        wub_ref[...] = wu_ref[0].astype(BF16)
        wdb_ref[...] = wd_ref[0].astype(BF16)

    @pl.when(used)
    def _():
        x = x_ref[...].astype(BF16)
        g = _dot(x, wgb_ref[...])
        u = _dot(x, wub_ref[...])
        a = (g * jax.nn.sigmoid(g)) * u
        y_ref[...] = _dot(a.astype(BF16), wdb_ref[...])

    @pl.when(jnp.logical_not(used))
    def _():
        y_ref[...] = jnp.zeros(y_ref.shape, F32)


def _experts(tile_expert, n_used, xs, wg, wu, wd, layer, n_tiles_max):
    _, d = xs.shape
    f = wg.shape[3]
    weight = lambda j, te, nt: (layer, te[j], 0, 0)
    return pl.pallas_call(
        _expert_kernel,
        grid_spec=pltpu.PrefetchScalarGridSpec(
            num_scalar_prefetch=2,
            grid=(n_tiles_max,),
            in_specs=[
                pl.BlockSpec((MOE_TM, d), lambda j, te, nt: (jnp.minimum(j, nt[0] - 1), 0)),
                pl.BlockSpec((None, 1, d, f), weight),
                pl.BlockSpec((None, 1, d, f), weight),
                pl.BlockSpec((None, 1, f, d), weight),
            ],
            out_specs=pl.BlockSpec((MOE_TM, d), lambda j, te, nt: (j, 0)),
            scratch_shapes=[pltpu.VMEM((d, f), BF16), pltpu.VMEM((d, f), BF16), pltpu.VMEM((f, d), BF16)],
        ),
        out_shape=jax.ShapeDtypeStruct(xs.shape, F32),
        compiler_params=_cparams("arbitrary"),
        name="moe_experts",
    )(tile_expert, n_used, xs, wg, wu, wd)


def _combine_kernel(pos_ref, pos_next_ref, ys_ref, x_ref, wcol_ref, mod_ref, lng_ref, lnb_ref, o_ref,
                    buf_ref, sem):
    i = pl.program_id(0)
    n_steps = pl.num_programs(0)
    slot = lax.rem(i, 2)

    def row_copy(p_ref, dst_slot, r, k):
        return pltpu.make_async_copy(ys_ref.at[pl.ds(p_ref[0, 0, k * TQ + r], 1)],
                                     buf_ref.at[dst_slot, k, pl.ds(r, 1)], sem.at[dst_slot])

    def gather(p_ref, dst_slot):
        _for_rows(lambda r: (row_copy(p_ref, dst_slot, r, 0).start(), row_copy(p_ref, dst_slot, r, 1).start()))

    @pl.when(i == 0)
    def _():
        gather(pos_ref, 0)

    @pl.when(i + 1 < n_steps)
    def _():
        gather(pos_next_ref, 1 - slot)

    _for_rows(lambda r: (row_copy(pos_ref, slot, r, 0).wait(), row_copy(pos_ref, slot, r, 1).wait()))

    w = wcol_ref[...]
    moe = w[:, 0:1] * buf_ref[slot, 0] + w[:, 1:2] * buf_ref[slot, 1]
    g2 = mod_ref[0, 5:6, :]
    o_ref[...] = _layer_norm(ALPHA * x_ref[...] + g2 * moe, lng_ref[...], lnb_ref[...])


def _combine(pos, ys, x1, wcol, mods, layer, ln_g, ln_b, latents_only):
    n, d = x1.shape
    n_steps = n // TQ
    row = lambda i: (i, 0)
    fixed = lambda i: (0, 0)
    if latents_only:
        out_rows, out_map = n - CTX_LEN, lambda i: (jnp.maximum(i - CTX_LEN // TQ, 0), 0)
    else:
        out_rows, out_map = n, row
    return pl.pallas_call(
        _combine_kernel,
        grid=(n_steps,),
        in_specs=[
            pl.BlockSpec((1, 1, 2 * TQ), lambda i: (i, 0, 0), memory_space=pltpu.SMEM),
            pl.BlockSpec((1, 1, 2 * TQ), lambda i: (jnp.minimum(i + 1, n_steps - 1), 0, 0),
                         memory_space=pltpu.SMEM),
            pl.BlockSpec(memory_space=pl.ANY),
            pl.BlockSpec((TQ, d), row),
            pl.BlockSpec((TQ, LANES), row),
            _mod_spec(layer),
            pl.BlockSpec((1, d), fixed),
            pl.BlockSpec((1, d), fixed),
        ],
        out_specs=pl.BlockSpec((TQ, d), out_map),
        out_shape=jax.ShapeDtypeStruct((out_rows, d), F32),
        scratch_shapes=[pltpu.VMEM((2, 2, TQ, d), F32), pltpu.SemaphoreType.DMA((2,))],
        compiler_params=_cparams("arbitrary"),
        name="moe_combine",
    )(pos, pos, ys, x1, wcol, mods, ln_g, ln_b)


def _moe(h2, route, wcol, counts, wg, wu, wd, x1, mods, layer, ln_g, ln_b, latents_only):
    n = h2.shape[0]
    pos, tile_expert, n_used, zero_tiles, n_tiles_max = _moe_layout(route, counts, n)
    xs = _scatter(zero_tiles, pos, h2, n_tiles_max * MOE_TM)
    ys = _experts(tile_expert, n_used, xs, wg, wu, wd, layer, n_tiles_max)
    return _combine(pos, ys, x1, wcol, mods, layer, ln_g, ln_b, latents_only)


def _rope_tables(n_lat, head_dim):
    quarter = head_dim // 4
    n_rows = n_lat // GRID_W
    inv_freq = ROPE_THETA ** (-jnp.arange(quarter, dtype=F32) / quarter)
    ang_r = jnp.arange(n_rows)[:, None].astype(F32) * inv_freq
    ang_c = jnp.arange(GRID_W)[:, None].astype(F32) * inv_freq
    by_row = lambda a: jnp.repeat(a, GRID_W, axis=0)
    by_col = lambda a: jnp.tile(a, (n_rows, 1))
    cos_r, sin_r = by_row(jnp.cos(ang_r)), by_row(jnp.sin(ang_r))
    cos_c, sin_c = by_col(jnp.cos(ang_c)), by_col(jnp.sin(ang_c))
    cos = jnp.concatenate([cos_r, cos_r, cos_c, cos_c], axis=1)
    sin = jnp.concatenate([-sin_r, sin_r, -sin_c, sin_c], axis=1)
    reps = LANES // head_dim
    cos = jnp.tile(cos, (1, reps))
    sin = jnp.tile(sin, (1, reps))
    cos = jnp.concatenate([jnp.ones((CTX_LEN, LANES), F32), cos], axis=0)
    sin = jnp.concatenate([jnp.zeros((CTX_LEN, LANES), F32), sin], axis=0)
    return cos, sin


def _a_head_order():
    grp = A_HQ // A_HKV
    order = []
    for j in range(grp):
        for g in range(A_HKV):
            order.append(g * grp + j)
    return order


def kernel(x, c, ctx, c_ctx, w_mod, b_mod, ln_g, ln_b, w_in_even, w_out_even, sink_logits, lam_q1, lam_k1,
           lam_q2, lam_k2, subln_g, w_in_odd, w_out_odd, q_norm_g, k_norm_g, w_router, b_router, w_gate,
           w_up, w_down):
    n_lat = x.shape[1]
    assert x.shape == (1, n_lat, D_MODEL) and ctx.shape == (1, CTX_LEN, D_MODEL)
    assert n_lat % max(UNROLL * TK, MOE_TM) == 0 and n_lat % GRID_W == 0
    xs = jnp.concatenate([ctx[0], x[0]], axis=0)

    c8 = jnp.zeros((8, D_MODEL), F32).at[0].set(c_ctx).at[1].set(c[0])
    mods = _modulation(c8, w_mod, b_mod)

    cos64, sin64 = _rope_tables(n_lat, A_HD)
    cos128, sin128 = _rope_tables(n_lat, C_HD)

    order = _a_head_order()
    wq_a = A_HQ * A_HD
    head = lambda h: slice(h * A_HD, (h + 1) * A_HD)
    w_in_even_p = jnp.concatenate([w_in_even[:, :, head(h)] for h in order] + [w_in_even[:, :, wq_a:]],
                                  axis=2).astype(BF16)
    w_out_even_p = jnp.concatenate([w_out_even[:, head(h)] for h in order] + [w_out_even[:, wq_a:]],
                                   axis=1).astype(BF16)
    w_in_odd_b = w_in_odd.astype(BF16)
    w_out_odd_b = w_out_odd.astype(BF16)
    wg, wu, wd = w_gate, w_up, w_down
    wr = jnp.pad(w_router, ((0, 0), (0, LANES - N_EXPERTS)))
    br = jnp.pad(b_router.astype(F32), (0, LANES - N_EXPERTS)).reshape(LANES, 1)

    segs_even = ((A_HQ * A_HD, True, None, (A_HD ** -0.5) * LOG2E, None), (A_HKV * A_HD, True, None, None, None),
                 (A_HKV * A_HD, False, None, None, "ones"),
                 (B_H * 2 * B_HD, True, None, (B_HD ** -0.5) * LOG2E, None),
                 (B_H * 2 * B_HD, True, None, None, None), (B_H * B_VD, False, None, None, "transposed"))
    segs_odd = ((C_HQ * C_HD, True, 0, (C_HD ** -0.5) * LOG2E, None), (C_HKV * C_HD, True, 1, None, None),
                (C_HKV * C_HD, False, None, None, "transposed"))

    for l in range(DEPTH):
        i = l // 2
        lng = ln_g[l]
        lnb = ln_b[l]
        if l % 2 == 0:
            lam_init = 0.8 - 0.6 * math.exp(-0.3 * l)
            aq, ak, av, bq, bk, bv = _inproj(xs, mods, l, w_in_even_p[i], cos64, sin64, segs_even, A_HD // 4)
            sink_rows = jnp.broadcast_to(
                (sink_logits[i][jnp.array(order)] * LOG2E)[:, None], (A_HQ, LANES)).astype(F32)
            oa = _attn_a(aq, ak, av, sink_rows)
            ob = _attn_b(bq, bk, bv, lam_q1[i][None], lam_k1[i][None], lam_q2[i][None], lam_k2[i][None],
                         subln_g[i][None], lam_init)
            half = wq_a
            routed = _post(oa, ob, 0, 0, w_out_even_p[i][:half], w_out_even_p[i][half:], xs, mods, l,
                           lng[0][None], lnb[0][None], wr, br)
        else:
            q, k, v = _inproj(xs, mods, l, w_in_odd_b[i], cos128, sin128, segs_odd, C_HD // 4,
                              gains=(q_norm_g[i][None], k_norm_g[i][None]))
            o = _attn_c(q, k, v)
            half = C_HQ * C_HD // 2
            routed = _post(o, o, 0, 1, w_out_odd_b[i][:half], w_out_odd_b[i][half:], xs, mods, l,
                           lng[0][None], lnb[0][None], wr, br)
        x1, h2, route, wcol, counts = routed
        xs = _moe(h2, route, wcol, counts, wg, wu, wd, x1, mods, l, lng[1][None], lnb[1][None],
                  latents_only=(l == DEPTH - 1))
    return xs[None]
```

```python
import functools
import math

import jax
import jax.numpy as jnp
from jax import lax
from jax.experimental import pallas as pl
from jax.experimental.pallas import tpu as pltpu

D_MODEL = 1024
SEQ = 16384
DEPTH = 4
GRID_W = 64
CTX_LEN = 256
WINDOW = 128
ROPE_THETA = 10000.0
A_HQ, A_HKV, A_HD = 8, 2, 64
B_H, B_HD = 4, 64
B_VD = 2 * B_HD
C_HQ, C_HKV, C_HD = 8, 2, 128
N_EXPERTS = 16
N_GROUPS = 4
EXPERTS_PER_GROUP = N_EXPERTS // N_GROUPS
D_EXPERT = 512
ALPHA = (2 * DEPTH) ** 0.25
LN_EPS = 1e-5
QK_EPS = 1e-6
SUBLN_EPS = 1e-5

LANES = 128
TQ = CTX_LEN
TK = 512
UNROLL = 4
MOE_TM = 256
VMEM_LIMIT = 56 * 1024 * 1024
LOG2E = 1.4426950408889634
BF16 = jnp.bfloat16
F32 = jnp.float32


def _cparams(*sem):
    return pltpu.CompilerParams(dimension_semantics=sem, vmem_limit_bytes=VMEM_LIMIT)


def _dot(a, b):
    return jnp.dot(a, b, preferred_element_type=F32)


def _dot_nt(a, b):
    return lax.dot_general(a, b, (((1,), (1,)), ((), ())), preferred_element_type=F32)


def _lane_tile(x, width):
    reps = width // x.shape[-1]
    return x if reps == 1 else jnp.concatenate([x] * reps, axis=-1)


def _mod_kernel(c_ref, w_ref, b_ref, o_ref):
    c = c_ref[...]
    s = c * jax.nn.sigmoid(c)
    o_ref[0] = _dot(s.astype(BF16), w_ref[0].astype(BF16)) + b_ref[0]


def _modulation(c8, w_mod, b_mod):
    depth, d, n6 = w_mod.shape
    tn = 1536
    out = pl.pallas_call(
        _mod_kernel,
        grid=(depth, n6 // tn),
        in_specs=[
            pl.BlockSpec((8, d), lambda l, j: (0, 0)),
            pl.BlockSpec((1, d, tn), lambda l, j: (l, 0, j)),
            pl.BlockSpec((1, 1, tn), lambda l, j: (l, 0, j)),
        ],
        out_specs=pl.BlockSpec((1, 8, tn), lambda l, j: (l, 0, j)),
        out_shape=jax.ShapeDtypeStruct((depth, 8, n6), F32),
        compiler_params=_cparams("arbitrary", "arbitrary"),
        name="modulation",
    )(c8, w_mod, b_mod.reshape(depth, 1, n6))
    m = out[:, :2].reshape(depth, 2, 6, d)
    m = jnp.pad(m, ((0, 0), (0, 0), (0, 2), (0, 0)))
    return m.reshape(depth * 2, 8, d)


def _mod_spec(layer):
    return pl.BlockSpec((1, 8, D_MODEL), lambda i, *_: (2 * layer + jnp.minimum(i, 1), 0, 0))


def _rope_chunk(z, cos, sin, quarter):
    lane = lax.broadcasted_iota(jnp.int32, z.shape, 1)
    first = (lane % (2 * quarter)) < quarter
    nxt = pltpu.roll(z, LANES - quarter, axis=1)
    prv = pltpu.roll(z, quarter, axis=1)
    return z * cos + jnp.where(first, nxt, prv) * sin


def _inproj_kernel(segs, quarter, n_gain, x_ref, mod_ref, w_ref, cos_ref, sin_ref, *rest):
    gains = rest[:n_gain]
    outs = rest[n_gain:]
    sh = mod_ref[0, 0:1, :]
    sc = mod_ref[0, 1:2, :]
    hb = (x_ref[...] * (1.0 + sc) + sh).astype(BF16)
    cos = cos_ref[...]
    sin = sin_ref[...]
    off = 0
    for (width, rope, gain_idx, scale, form), o_ref in zip(segs, outs):
        z = _dot(hb, w_ref[:, off:off + width])
        for c in range(width // LANES):
            zc = z[:, c * LANES:(c + 1) * LANES]
            if gain_idx is not None:
                ms = jnp.mean(zc * zc, axis=-1, keepdims=True)
                zc = zc * lax.rsqrt(ms + QK_EPS) * gains[gain_idx][...]
            if rope:
                zc = _rope_chunk(zc, cos, sin, quarter)
            if scale is not None:
                zc = zc * scale
            if form == "ones":
                o_ref[:, 2 * c * LANES:(2 * c + 1) * LANES] = zc.astype(BF16)
                o_ref[:, (2 * c + 1) * LANES:(2 * c + 2) * LANES] = jnp.ones(zc.shape, BF16)
            elif form == "transposed":
                o_ref[c, 0, 0:LANES, :] = zc.T.astype(BF16)
                pad_row = lax.broadcasted_iota(jnp.int32, (o_ref.shape[2] - LANES, TQ), 0)
                o_ref[c, 0, LANES:, :] = jnp.where(pad_row == 0, 1.0, 0.0).astype(BF16)
            else:
                o_ref[:, c * LANES:(c + 1) * LANES] = zc.astype(BF16)
        off += width


def _inproj(x, mods, layer, w, cos, sin, segs, quarter, gains=()):
    n, d = x.shape
    wtot = w.shape[1]
    row = lambda i: (i, 0)
    fixed = lambda i: (0, 0)
    in_specs = [
        pl.BlockSpec((TQ, d), row),
        _mod_spec(layer),
        pl.BlockSpec((d, wtot), fixed),
        pl.BlockSpec((TQ, LANES), row),
        pl.BlockSpec((TQ, LANES), row),
    ] + [pl.BlockSpec((1, LANES), fixed) for _ in gains]
    out_specs, out_shape = [], []
    for width, _, _, _, form in segs:
        if form == "transposed":
            heads = width // LANES
            out_specs.append(pl.BlockSpec((heads, 1, VT_ROWS, TQ), lambda i: (0, i, 0, 0)))
            out_shape.append(jax.ShapeDtypeStruct((heads, n // TQ, VT_ROWS, TQ), BF16))
        else:
            w_ = width * (2 if form == "ones" else 1)
            out_specs.append(pl.BlockSpec((TQ, w_), row))
            out_shape.append(jax.ShapeDtypeStruct((n, w_), BF16))
    return pl.pallas_call(
        functools.partial(_inproj_kernel, segs, quarter, len(gains)),
        grid=(n // TQ,),
        in_specs=in_specs,
        out_specs=out_specs,
        out_shape=out_shape,
        compiler_params=_cparams("arbitrary"),
        name="inproj",
    )(x, mods, w, cos, sin, *gains)


VT_ROWS = LANES + 16


def _flash(qs_ref, k_ref, vt_ref, m_ref, acc_ref, s_refs, is_ctx_block):
    n_chunks = (k_ref.shape[0] - CTX_LEN) // TK
    n_sets = k_ref.shape[1] // LANES
    cols = qs_ref.shape[0] // n_sets
    m_ref[...] = jnp.full(m_ref.shape, -jnp.inf, F32)
    acc_ref[...] = jnp.zeros(acc_ref.shape, F32)

    def scores(slot, start, size):
        for g in range(n_sets):
            cs = slice(g * cols, (g + 1) * cols)
            s_refs[slot][0:size, cs] = _dot_nt(k_ref[pl.ds(start, size), g * LANES:(g + 1) * LANES], qs_ref[cs, :])

    def update(slot, first_block, n_blocks):
        s = s_refs[slot][0:n_blocks * TQ, :]
        m_prev = m_ref[0:1, :]
        m_new = jnp.maximum(m_prev, jnp.max(s, axis=0, keepdims=True))
        pb = jnp.exp2((s - m_new).astype(BF16))
        alpha = jnp.exp2(m_prev - m_new)
        for g in range(n_sets):
            cs = slice(g * cols, (g + 1) * cols)
            pv = _dot(vt_ref[g, first_block], pb[0:TQ, cs])
            for b in range(1, n_blocks):
                pv = pv + _dot(vt_ref[g, first_block + b], pb[b * TQ:(b + 1) * TQ, cs])
            acc_ref[:, cs] = acc_ref[:, cs] * alpha[:, cs] + pv
        m_ref[...] = jnp.broadcast_to(m_new, m_ref.shape)

    def lat(j):
        return pl.multiple_of(CTX_LEN + j * TK, CTX_LEN)

    blocks_per_chunk = TK // TQ
    ctx_blocks = CTX_LEN // TQ
    scores(1, 0, CTX_LEN)
    update(1, 0, ctx_blocks)

    @pl.when(jnp.logical_not(is_ctx_block))
    def _():
        scores(0, lat(0), TK)

        def trip(jj, carry):
            j = UNROLL * jj
            for u in range(UNROLL):
                scores((u + 1) % 2, lat(jnp.minimum(j + u + 1, n_chunks - 1)), TK)
                update(u % 2, ctx_blocks + (j + u) * blocks_per_chunk, blocks_per_chunk)
            return carry

        lax.fori_loop(0, n_chunks // UNROLL, trip, 0)


def _flash_out(acc_ref, idx):
    cs = slice(idx * TQ, (idx + 1) * TQ)
    return (acc_ref[0:LANES, cs] / acc_ref[LANES:LANES + 1, cs]).T


def _flash_scratch(n_stack):
    rows = n_stack * TQ
    return [
        pltpu.VMEM((rows, LANES), BF16),
        pltpu.VMEM((8, rows), F32),
        pltpu.VMEM((VT_ROWS, rows), F32),
        pltpu.VMEM((TK, rows), F32),
        pltpu.VMEM((TK, rows), F32),
    ]


def _attn_c_kernel(q_ref, k_ref, vt_ref, o_ref, qs_ref, m_ref, acc_ref, s0_ref, s1_ref):
    grp = C_HQ // C_HKV
    for h in range(grp):
        qs_ref[h * TQ:(h + 1) * TQ, :] = q_ref[:, h * C_HD:(h + 1) * C_HD]
    _flash(qs_ref, k_ref, vt_ref, m_ref, acc_ref, (s0_ref, s1_ref), pl.program_id(1) == 0)
    for h in range(grp):
        o_ref[:, h * C_HD:(h + 1) * C_HD] = _flash_out(acc_ref, h).astype(BF16)


def _attn_c(q, k, vt):
    n = q.shape[0]
    grp = C_HQ // C_HKV
    wq = grp * C_HD
    return pl.pallas_call(
        _attn_c_kernel,
        grid=(C_HKV, n // TQ),
        in_specs=[
            pl.BlockSpec((TQ, wq), lambda g, i: (i, g)),
            pl.BlockSpec((n, C_HD), lambda g, i: (0, g), pipeline_mode=pl.Buffered(1)),
            pl.BlockSpec((1, n // TQ, VT_ROWS, TQ), lambda g, i: (g, 0, 0, 0), pipeline_mode=pl.Buffered(1)),
        ],
        out_specs=pl.BlockSpec((TQ, wq), lambda g, i: (i, g)),
        out_shape=jax.ShapeDtypeStruct((n, C_HQ * C_HD), BF16),
        scratch_shapes=_flash_scratch(grp),
        compiler_params=_cparams("arbitrary", "arbitrary"),
        name="attn_c",
    )(q, k, vt)


B_HEADS_PER_STEP = 2


def _attn_b_kernel(lam_init, q_ref, k_ref, vt_ref, lq1_ref, lk1_ref, lq2_ref, lk2_ref, g_ref, o_ref,
                   qs_ref, m_ref, acc_ref, s0_ref, s1_ref):
    for h in range(B_HEADS_PER_STEP):
        q = q_ref[:, h * B_VD:(h + 1) * B_VD]
        lane = lax.broadcasted_iota(jnp.int32, q.shape, 1)
        zero = jnp.zeros_like(q)
        qs_ref[(2 * h) * TQ:(2 * h + 1) * TQ, :] = jnp.where(lane < B_HD, q, zero)
        qs_ref[(2 * h + 1) * TQ:(2 * h + 2) * TQ, :] = jnp.where(lane >= B_HD, q, zero)
    _flash(qs_ref, k_ref, vt_ref, m_ref, acc_ref, (s0_ref, s1_ref), pl.program_id(1) == 0)
    lam = (jnp.exp(jnp.sum(lq1_ref[...] * lk1_ref[...], axis=-1, keepdims=True))
           - jnp.exp(jnp.sum(lq2_ref[...] * lk2_ref[...], axis=-1, keepdims=True))) + lam_init
    for h in range(B_HEADS_PER_STEP):
        o = _flash_out(acc_ref, 2 * h) - lam * _flash_out(acc_ref, 2 * h + 1)
        ms = jnp.mean(o * o, axis=-1, keepdims=True)
        o = o * lax.rsqrt(ms + SUBLN_EPS) * g_ref[...] * (1.0 - lam_init)
        o_ref[:, h * B_VD:(h + 1) * B_VD] = o.astype(BF16)


def _attn_b(q, k, vt, lq1, lk1, lq2, lk2, subln_g, lam_init):
    n = q.shape[0]
    hps = B_HEADS_PER_STEP
    blk = lambda h, i: (i, h)
    col = lambda h, i: (0, h)
    fixed = lambda h, i: (0, 0)
    small = pl.BlockSpec((1, B_HD), fixed)
    return pl.pallas_call(
        functools.partial(_attn_b_kernel, lam_init),
        grid=(B_H // hps, n // TQ),
        in_specs=[
            pl.BlockSpec((TQ, hps * B_VD), blk),
            pl.BlockSpec((n, hps * B_VD), col, pipeline_mode=pl.Buffered(1)),
            pl.BlockSpec((hps, n // TQ, VT_ROWS, TQ), lambda h, i: (h, 0, 0, 0), pipeline_mode=pl.Buffered(1)),
            small, small, small, small,
            pl.BlockSpec((1, B_VD), fixed),
        ],
        out_specs=pl.BlockSpec((TQ, hps * B_VD), blk),
        out_shape=jax.ShapeDtypeStruct((n, B_H * B_VD), BF16),
        scratch_shapes=_flash_scratch(2 * hps),
        compiler_params=_cparams("arbitrary", "arbitrary"),
        name="attn_b",
    )(q, k, vt, lq1, lk1, lq2, lk2, subln_g)


A_WIN_KEYS = TQ + 2 * WINDOW


def _attn_a_kernel(q_ref, k_ref, v_ref, sink_ref, o_ref, qs_ref):
    i = pl.program_id(0)
    n_lat = k_ref.shape[0] - CTX_LEN
    n_blk = A_HQ * A_HD // LANES
    for j in range(n_blk):
        q = q_ref[:, j * LANES:(j + 1) * LANES]
        lane = lax.broadcasted_iota(jnp.int32, q.shape, 1)
        zero = jnp.zeros_like(q)
        qs_ref[(2 * j) * TQ:(2 * j + 1) * TQ, :] = jnp.where(lane < A_HD, q, zero)
        qs_ref[(2 * j + 1) * TQ:(2 * j + 2) * TQ, :] = jnp.where(lane >= A_HD, q, zero)
    sink = jnp.concatenate([jnp.broadcast_to(sink_ref[r:r + 1, :], (TQ, LANES)) for r in range(A_HQ)], axis=0)
    s_c = _dot_nt(qs_ref[...], k_ref[0:CTX_LEN, :])
    vc = v_ref[0:CTX_LEN, :]

    def finish(m, acc):
        o = acc[:, :LANES] / (acc[:, LANES:] + jnp.exp2(sink - m))
        lane = lax.broadcasted_iota(jnp.int32, (TQ, LANES), 1)
        for j in range(n_blk):
            lo = o[(2 * j) * TQ:(2 * j + 1) * TQ, :]
            hi = o[(2 * j + 1) * TQ:(2 * j + 2) * TQ, :]
            o_ref[:, j * LANES:(j + 1) * LANES] = jnp.where(lane < A_HD, lo, hi).astype(BF16)

    @pl.when(i == 0)
    def _():
        m = jnp.maximum(sink, jnp.max(s_c, axis=1, keepdims=True))
        p_c = jnp.exp2(s_c - _lane_tile(m, CTX_LEN)).astype(BF16)
        finish(m, _dot(p_c, vc))

    @pl.when(i > 0)
    def _():
        q0 = (i - 1) * TQ
        ws = jnp.clip(q0 - WINDOW, 0, n_lat - A_WIN_KEYS)
        ws = pl.multiple_of(ws, WINDOW)
        qpos = q0 + lax.broadcasted_iota(jnp.int32, (TQ, A_WIN_KEYS), 0)
        kpos = ws + lax.broadcasted_iota(jnp.int32, (TQ, A_WIN_KEYS), 1)
        ok = jnp.abs(qpos - kpos) <= WINDOW
        s_l = _dot_nt(qs_ref[...], k_ref[pl.ds(CTX_LEN + ws, A_WIN_KEYS), :])
        s_l = jnp.where(ok[None], s_l.reshape(A_HQ, TQ, A_WIN_KEYS), -jnp.inf).reshape(A_HQ * TQ, A_WIN_KEYS)
        m = jnp.maximum(sink, jnp.maximum(jnp.max(s_c, axis=1, keepdims=True),
                                          jnp.max(s_l, axis=1, keepdims=True)))
        p_c = jnp.exp2(s_c - _lane_tile(m, CTX_LEN)).astype(BF16)
        p_l = jnp.exp2(s_l - _lane_tile(m, A_WIN_KEYS)).astype(BF16)
        finish(m, _dot(p_c, vc) + _dot(p_l, v_ref[pl.ds(CTX_LEN + ws, A_WIN_KEYS), :]))


def _attn_a(q, k, v_aug, sink_rows):
    n = q.shape[0]
    wq = A_HQ * A_HD
    wkv = A_HKV * A_HD
    return pl.pallas_call(
        _attn_a_kernel,
        grid=(n // TQ,),
        in_specs=[
            pl.BlockSpec((TQ, wq), lambda i: (i, 0)),
            pl.BlockSpec((n, wkv), lambda i: (0, 0)),
            pl.BlockSpec((n, 2 * wkv), lambda i: (0, 0)),
            pl.BlockSpec((A_HQ, LANES), lambda i: (0, 0)),
        ],
        out_specs=pl.BlockSpec((TQ, wq), lambda i: (i, 0)),
        out_shape=jax.ShapeDtypeStruct((n, wq), BF16),
        scratch_shapes=[pltpu.VMEM((A_HQ * TQ, LANES), BF16)],
        compiler_params=_cparams("arbitrary"),
        name="attn_a",
    )(q, k, v_aug, sink_rows)


def _layer_norm(z, g, b):
    mu = jnp.mean(z, axis=-1, keepdims=True)
    zc = z - mu
    var = jnp.mean(zc * zc, axis=-1, keepdims=True)
    return zc * lax.rsqrt(var + LN_EPS) * g + b


def _route(logits_t, bias_t):
    t = logits_t.shape[1]
    score = [jax.nn.sigmoid(logits_t[e:e + 1, :]) for e in range(N_EXPERTS)]
    sel = [score[e] + bias_t[e:e + 1, :] for e in range(N_EXPERTS)]
    best_g = jnp.zeros((1, t), jnp.int32)
    best_s = None
    for g in range(N_GROUPS):
        a = sel[g * EXPERTS_PER_GROUP:(g + 1) * EXPERTS_PER_GROUP]
        gs = None
        for x in range(EXPERTS_PER_GROUP):
            for y in range(x + 1, EXPERTS_PER_GROUP):
                pair = a[x] + a[y]
                gs = pair if gs is None else jnp.maximum(gs, pair)
        if best_s is None:
            best_s = gs
        else:
            upd = gs > best_s
            best_g = jnp.where(upd, g, best_g)
            best_s = jnp.where(upd, gs, best_s)
    neg = jnp.full((1, t), -jnp.inf, F32)
    masked = [jnp.where(best_g == (e // EXPERTS_PER_GROUP), sel[e], neg) for e in range(N_EXPERTS)]
    i1 = jnp.zeros((1, t), jnp.int32)
    v1 = masked[0]
    for e in range(1, N_EXPERTS):
        upd = masked[e] > v1
        i1 = jnp.where(upd, e, i1)
        v1 = jnp.where(upd, masked[e], v1)
    i2 = jnp.full((1, t), -1, jnp.int32)
    v2 = neg
    for e in range(N_EXPERTS):
        upd = (masked[e] > v2) & (i1 != e)
        i2 = jnp.where(upd, e, i2)
        v2 = jnp.where(upd, masked[e], v2)
    zero = jnp.zeros((1, t), F32)
    w1 = zero
    w2 = zero
    for e in range(N_EXPERTS):
        w1 = w1 + jnp.where(i1 == e, score[e], zero)
        w2 = w2 + jnp.where(i2 == e, score[e], zero)
    den = w1 + w2
    return i1, i2, w1 / den, w2 / den


def _post_kernel(o1_ref, o2_ref, w1_ref, w2_ref, x_ref, mod_ref, lng_ref, lnb_ref, wr_ref, br_ref,
                 x1_ref, h2_ref, route_ref, wcol_ref, cnt_out_ref, cnt_ref):
    @pl.when(pl.program_id(0) == 0)
    def _():
        cnt_ref[...] = jnp.zeros(cnt_ref.shape, F32)

    g1 = mod_ref[0, 2:3, :]
    sh2 = mod_ref[0, 3:4, :]
    sc2 = mod_ref[0, 4:5, :]
    y = _dot(o1_ref[...], w1_ref[...]) + _dot(o2_ref[...], w2_ref[...])
    x1 = _layer_norm(ALPHA * x_ref[...] + g1 * y, lng_ref[...], lnb_ref[...])
    x1_ref[...] = x1
    h2 = x1 * (1.0 + sc2) + sh2
    h2_ref[...] = h2
    h_hi = h2.astype(BF16)
    h_lo = (h2 - h_hi.astype(F32)).astype(BF16)
    wr = wr_ref[...]
    w_hi = wr.astype(BF16)
    w_lo = (wr - w_hi.astype(F32)).astype(BF16)
    logits = _dot(h_hi, w_hi) + (_dot(h_hi, w_lo) + _dot(h_lo, w_hi))
    i1, i2, w1, w2 = _route(logits.T, br_ref[...])

    row = lax.broadcasted_iota(jnp.int32, (LANES, TQ), 0)
    hit1 = row == i1
    hit2 = row == i2
    assigned = jnp.where(hit1 | hit2, 1.0, 0.0)
    t_from = lax.broadcasted_iota(jnp.int32, (TQ, TQ), 0)
    t_to = lax.broadcasted_iota(jnp.int32, (TQ, TQ), 1)
    earlier = jnp.where(t_from < t_to, 1.0, 0.0).astype(BF16)
    rank = cnt_ref[:, 0:1] + _dot(assigned.astype(BF16), earlier)
    r1 = jnp.sum(jnp.where(hit1, rank, 0.0), axis=0, keepdims=True).astype(jnp.int32)
    r2 = jnp.sum(jnp.where(hit2, rank, 0.0), axis=0, keepdims=True).astype(jnp.int32)
    cnt_ref[...] = cnt_ref[...] + jnp.sum(assigned, axis=1, keepdims=True)
    cnt_out_ref[...] = cnt_ref[...]

    row8 = lax.broadcasted_iota(jnp.int32, (8, TQ), 0)
    route_ref[0] = jnp.where(row8 == 0, i1, jnp.where(row8 == 1, i2, jnp.where(row8 == 2, r1,
                             jnp.where(row8 == 3, r2, 0))))
    wcol_ref[...] = jnp.where(row == 0, w1, jnp.where(row == 1, w2, 0.0)).T


def _post(o1, o2, o1_col, o2_col, w1, w2, x, mods, layer, ln_g, ln_b, wr, br):
    n, d = x.shape
    half = w1.shape[0]
    row = lambda i: (i, 0)
    fixed = lambda i: (0, 0)
    return pl.pallas_call(
        _post_kernel,
        grid=(n // TQ,),
        in_specs=[
            pl.BlockSpec((TQ, half), lambda i: (i, o1_col)),
            pl.BlockSpec((TQ, half), lambda i: (i, o2_col)),
            pl.BlockSpec((half, d), fixed),
            pl.BlockSpec((half, d), fixed),
            pl.BlockSpec((TQ, d), row),
            _mod_spec(layer),
            pl.BlockSpec((1, d), fixed),
            pl.BlockSpec((1, d), fixed),
            pl.BlockSpec((d, LANES), fixed),
            pl.BlockSpec((LANES, 1), fixed),
        ],
        out_specs=[
            pl.BlockSpec((TQ, d), row),
            pl.BlockSpec((TQ, d), row),
            pl.BlockSpec((1, 8, TQ), lambda i: (i, 0, 0)),
            pl.BlockSpec((TQ, LANES), row),
            pl.BlockSpec((LANES, LANES), fixed),
        ],
        out_shape=[
            jax.ShapeDtypeStruct((n, d), F32),
            jax.ShapeDtypeStruct((n, d), F32),
            jax.ShapeDtypeStruct((n // TQ, 8, TQ), jnp.int32),
            jax.ShapeDtypeStruct((n, LANES), F32),
            jax.ShapeDtypeStruct((LANES, LANES), F32),
        ],
        scratch_shapes=[pltpu.VMEM((LANES, LANES), F32)],
        compiler_params=_cparams("arbitrary"),
        name="post_attn",
    )(o1, o2, w1, w2, x, mods, ln_g, ln_b, wr, br)


def _moe_layout(route, counts, n):
    n_tiles_max = (2 * n) // MOE_TM + N_EXPERTS
    cnt = counts[:N_EXPERTS, 0].astype(jnp.int32)
    tiles = (cnt + MOE_TM - 1) // MOE_TM
    tiles_end = jnp.cumsum(tiles)
    seg_start = (tiles_end - tiles) * MOE_TM
    experts = jnp.arange(N_EXPERTS)

    def start_of(e):
        return jnp.sum(jnp.where(e[..., None] == experts, seg_start, 0), axis=-1)

    pos = jnp.concatenate([start_of(route[:, 0]) + route[:, 2],
                           start_of(route[:, 1]) + route[:, 3]], axis=1)[:, None, :]
    n_used = tiles_end[-1]
    j = jnp.minimum(jnp.arange(n_tiles_max), n_used - 1)
    tile_expert = jnp.sum(j[:, None] >= tiles_end[None, :], axis=1).astype(jnp.int32)
    last_tiles = jnp.maximum(tiles_end - 1, 0)
    tail_tiles = jnp.minimum(n_used + experts, n_tiles_max - 1)
    live = jnp.concatenate([tiles > 0, n_used + experts < n_tiles_max])
    zero_tiles = jnp.concatenate([last_tiles, tail_tiles, live]).astype(jnp.int32)
    return pos.astype(jnp.int32), tile_expert, n_used.reshape(1).astype(jnp.int32), zero_tiles, n_tiles_max


def _for_rows(fn):
    sublanes = 8

    def group(t, carry):
        base = pl.multiple_of(t * sublanes, sublanes)
        for u in range(sublanes):
            fn(base + u)
        return carry

    lax.fori_loop(0, TQ // sublanes, group, 0)


def _scatter_kernel(zt_ref, pos_ref, h_ref, xs_ref, zero_ref, sem, zsem):
    def zero_copy(t):
        return pltpu.make_async_copy(zero_ref, xs_ref.at[pl.ds(zt_ref[t] * MOE_TM, MOE_TM)], zsem)

    @pl.when(pl.program_id(0) == 0)
    def _():
        zero_ref[...] = jnp.zeros(zero_ref.shape, F32)
        for t in range(2 * N_EXPERTS):
            @pl.when(zt_ref[2 * N_EXPERTS + t] != 0)
            def _():
                zero_copy(t).start()
        for t in range(2 * N_EXPERTS):
            @pl.when(zt_ref[2 * N_EXPERTS + t] != 0)
            def _():
                zero_copy(t).wait()

    def row_copy(r, k):
        return pltpu.make_async_copy(h_ref.at[pl.ds(r, 1)], xs_ref.at[pl.ds(pos_ref[0, 0, k * TQ + r], 1)], sem)

    _for_rows(lambda r: (row_copy(r, 0).start(), row_copy(r, 1).start()))
    _for_rows(lambda r: (row_copy(r, 0).wait(), row_copy(r, 1).wait()))


def _scatter(zero_tiles, pos, h2, n_rows):
    n, d = h2.shape
    return pl.pallas_call(
        _scatter_kernel,
        grid_spec=pltpu.PrefetchScalarGridSpec(
            num_scalar_prefetch=1,
            grid=(n // TQ,),
            in_specs=[
                pl.BlockSpec((1, 1, 2 * TQ), lambda i, zt: (i, 0, 0), memory_space=pltpu.SMEM),
                pl.BlockSpec((TQ, d), lambda i, zt: (i, 0)),
            ],
            out_specs=pl.BlockSpec(memory_space=pl.ANY),
            scratch_shapes=[pltpu.VMEM((MOE_TM, d), F32), pltpu.SemaphoreType.DMA(()),
                            pltpu.SemaphoreType.DMA(())],
        ),
        out_shape=jax.ShapeDtypeStruct((n_rows, d), F32),
        compiler_params=_cparams("arbitrary"),
        name="moe_scatter",
    )(zero_tiles, pos, h2)


def _expert_kernel(te_ref, nt_ref, x_ref, wg_ref, wu_ref, wd_ref, y_ref, wgb_ref, wub_ref, wdb_ref):
    j = pl.program_id(0)
    used = j < nt_ref[0]
    new_expert = jnp.logical_or(j == 0, te_ref[j] != te_ref[jnp.maximum(j - 1, 0)])

    @pl.when(jnp.logical_and(used, new_expert))
    def _():
        wgb_ref[...] = wg_ref[0].astype(BF16)
        wub_ref[...] = wu_ref[0].astype(BF16)
        wdb_ref[...] = wd_ref[0].astype(BF16)

    @pl.when(used)
    def _():
        x = x_ref[...].astype(BF16)
        g = _dot(x, wgb_ref[...])
        u = _dot(x, wub_ref[...])
        a = (g * jax.nn.sigmoid(g)) * u
        y_ref[...] = _dot(a.astype(BF16), wdb_ref[...])

    @pl.when(jnp.logical_not(used))
    def _():
        y_ref[...] = jnp.zeros(y_ref.shape, F32)


def _experts(tile_expert, n_used, xs, wg, wu, wd, layer, n_tiles_max):
    _, d = xs.shape
    f = wg.shape[3]
    weight = lambda j, te, nt: (layer, te[j], 0, 0)
    return pl.pallas_call(
        _expert_kernel,
        grid_spec=pltpu.PrefetchScalarGridSpec(
            num_scalar_prefetch=2,
            grid=(n_tiles_max,),
            in_specs=[
                pl.BlockSpec((MOE_TM, d), lambda j, te, nt: (jnp.minimum(j, nt[0] - 1), 0)),
                pl.BlockSpec((None, 1, d, f), weight),
                pl.BlockSpec((None, 1, d, f), weight),
                pl.BlockSpec((None, 1, f, d), weight),
            ],
            out_specs=pl.BlockSpec((MOE_TM, d), lambda j, te, nt: (j, 0)),
            scratch_shapes=[pltpu.VMEM((d, f), BF16), pltpu.VMEM((d, f), BF16), pltpu.VMEM((f, d), BF16)],
        ),
        out_shape=jax.ShapeDtypeStruct(xs.shape, F32),
        compiler_params=_cparams("arbitrary"),
        name="moe_experts",
    )(tile_expert, n_used, xs, wg, wu, wd)


def _combine_kernel(pos_ref, pos_next_ref, ys_ref, x_ref, wcol_ref, mod_ref, lng_ref, lnb_ref, o_ref,
                    buf_ref, sem):
    i = pl.program_id(0)
    n_steps = pl.num_programs(0)
    slot = lax.rem(i, 2)

    def row_copy(p_ref, dst_slot, r, k):
        return pltpu.make_async_copy(ys_ref.at[pl.ds(p_ref[0, 0, k * TQ + r], 1)],
                                     buf_ref.at[dst_slot, k, pl.ds(r, 1)], sem.at[dst_slot])

    def gather(p_ref, dst_slot):
        _for_rows(lambda r: (row_copy(p_ref, dst_slot, r, 0).start(), row_copy(p_ref, dst_slot, r, 1).start()))

    @pl.when(i == 0)
    def _():
        gather(pos_ref, 0)

    @pl.when(i + 1 < n_steps)
    def _():
        gather(pos_next_ref, 1 - slot)

    _for_rows(lambda r: (row_copy(pos_ref, slot, r, 0).wait(), row_copy(pos_ref, slot, r, 1).wait()))

    w = wcol_ref[...]
    moe = w[:, 0:1] * buf_ref[slot, 0] + w[:, 1:2] * buf_ref[slot, 1]
    g2 = mod_ref[0, 5:6, :]
    o_ref[...] = _layer_norm(ALPHA * x_ref[...] + g2 * moe, lng_ref[...], lnb_ref[...])


def _combine(pos, ys, x1, wcol, mods, layer, ln_g, ln_b, latents_only):
    n, d = x1.shape
    n_steps = n // TQ
    row = lambda i: (i, 0)
    fixed = lambda i: (0, 0)
    if latents_only:
        out_rows, out_map = n - CTX_LEN, lambda i: (jnp.maximum(i - CTX_LEN // TQ, 0), 0)
    else:
        out_rows, out_map = n, row
    return pl.pallas_call(
        _combine_kernel,
        grid=(n_steps,),
        in_specs=[
            pl.BlockSpec((1, 1, 2 * TQ), lambda i: (i, 0, 0), memory_space=pltpu.SMEM),
            pl.BlockSpec((1, 1, 2 * TQ), lambda i: (jnp.minimum(i + 1, n_steps - 1), 0, 0),
                         memory_space=pltpu.SMEM),
            pl.BlockSpec(memory_space=pl.ANY),
            pl.BlockSpec((TQ, d), row),
            pl.BlockSpec((TQ, LANES), row),
            _mod_spec(layer),
            pl.BlockSpec((1, d), fixed),
            pl.BlockSpec((1, d), fixed),
        ],
        out_specs=pl.BlockSpec((TQ, d), out_map),
        out_shape=jax.ShapeDtypeStruct((out_rows, d), F32),
        scratch_shapes=[pltpu.VMEM((2, 2, TQ, d), F32), pltpu.SemaphoreType.DMA((2,))],
        compiler_params=_cparams("arbitrary"),
        name="moe_combine",
    )(pos, pos, ys, x1, wcol, mods, ln_g, ln_b)


def _moe(h2, route, wcol, counts, wg, wu, wd, x1, mods, layer, ln_g, ln_b, latents_only):
    n = h2.shape[0]
    pos, tile_expert, n_used, zero_tiles, n_tiles_max = _moe_layout(route, counts, n)
    xs = _scatter(zero_tiles, pos, h2, n_tiles_max * MOE_TM)
    ys = _experts(tile_expert, n_used, xs, wg, wu, wd, layer, n_tiles_max)
    return _combine(pos, ys, x1, wcol, mods, layer, ln_g, ln_b, latents_only)


def _rope_tables(n_lat, head_dim):
    quarter = head_dim // 4
    n_rows = n_lat // GRID_W
    inv_freq = ROPE_THETA ** (-jnp.arange(quarter, dtype=F32) / quarter)
    ang_r = jnp.arange(n_rows)[:, None].astype(F32) * inv_freq
    ang_c = jnp.arange(GRID_W)[:, None].astype(F32) * inv_freq
    by_row = lambda a: jnp.repeat(a, GRID_W, axis=0)
    by_col = lambda a: jnp.tile(a, (n_rows, 1))
    cos_r, sin_r = by_row(jnp.cos(ang_r)), by_row(jnp.sin(ang_r))
    cos_c, sin_c = by_col(jnp.cos(ang_c)), by_col(jnp.sin(ang_c))
    cos = jnp.concatenate([cos_r, cos_r, cos_c, cos_c], axis=1)
    sin = jnp.concatenate([-sin_r, sin_r, -sin_c, sin_c], axis=1)
    reps = LANES // head_dim
    cos = jnp.tile(cos, (1, reps))
    sin = jnp.tile(sin, (1, reps))
    cos = jnp.concatenate([jnp.ones((CTX_LEN, LANES), F32), cos], axis=0)
    sin = jnp.concatenate([jnp.zeros((CTX_LEN, LANES), F32), sin], axis=0)
    return cos, sin


def _a_head_order():
    grp = A_HQ // A_HKV
    order = []
    for j in range(grp):
        for g in range(A_HKV):
            order.append(g * grp + j)
    return order


def kernel(x, c, ctx, c_ctx, w_mod, b_mod, ln_g, ln_b, w_in_even, w_out_even, sink_logits, lam_q1, lam_k1,
           lam_q2, lam_k2, subln_g, w_in_odd, w_out_odd, q_norm_g, k_norm_g, w_router, b_router, w_gate,
           w_up, w_down):
    n_lat = x.shape[1]
    assert x.shape == (1, n_lat, D_MODEL) and ctx.shape == (1, CTX_LEN, D_MODEL)
    assert n_lat % max(UNROLL * TK, MOE_TM) == 0 and n_lat % GRID_W == 0
    xs = jnp.concatenate([ctx[0], x[0]], axis=0)

    c8 = jnp.zeros((8, D_MODEL), F32).at[0].set(c_ctx).at[1].set(c[0])
    mods = _modulation(c8, w_mod, b_mod)

    cos64, sin64 = _rope_tables(n_lat, A_HD)
    cos128, sin128 = _rope_tables(n_lat, C_HD)

    order = _a_head_order()
    wq_a = A_HQ * A_HD
    head = lambda h: slice(h * A_HD, (h + 1) * A_HD)
    w_in_even_p = jnp.concatenate([w_in_even[:, :, head(h)] for h in order] + [w_in_even[:, :, wq_a:]],
                                  axis=2).astype(BF16)
    w_out_even_p = jnp.concatenate([w_out_even[:, head(h)] for h in order] + [w_out_even[:, wq_a:]],
                                   axis=1).astype(BF16)
    w_in_odd_b = w_in_odd.astype(BF16)
    w_out_odd_b = w_out_odd.astype(BF16)
    wg, wu, wd = w_gate, w_up, w_down
    wr = jnp.pad(w_router, ((0, 0), (0, LANES - N_EXPERTS)))
    br = jnp.pad(b_router.astype(F32), (0, LANES - N_EXPERTS)).reshape(LANES, 1)

    segs_even = ((A_HQ * A_HD, True, None, (A_HD ** -0.5) * LOG2E, None), (A_HKV * A_HD, True, None, None, None),
                 (A_HKV * A_HD, False, None, None, "ones"),
                 (B_H * 2 * B_HD, True, None, (B_HD ** -0.5) * LOG2E, None),
                 (B_H * 2 * B_HD, True, None, None, None), (B_H * B_VD, False, None, None, "transposed"))
    segs_odd = ((C_HQ * C_HD, True, 0, (C_HD ** -0.5) * LOG2E, None), (C_HKV * C_HD, True, 1, None, None),
                (C_HKV * C_HD, False, None, None, "transposed"))

    for l in range(DEPTH):
        i = l // 2
        lng = ln_g[l]
        lnb = ln_b[l]
        if l % 2 == 0:
            lam_init = 0.8 - 0.6 * math.exp(-0.3 * l)
            aq, ak, av, bq, bk, bv = _inproj(xs, mods, l, w_in_even_p[i], cos64, sin64, segs_even, A_HD // 4)
            sink_rows = jnp.broadcast_to(
                (sink_logits[i][jnp.array(order)] * LOG2E)[:, None], (A_HQ, LANES)).astype(F32)
            oa = _attn_a(aq, ak, av, sink_rows)
            ob = _attn_b(bq, bk, bv, lam_q1[i][None], lam_k1[i][None], lam_q2[i][None], lam_k2[i][None],
                         subln_g[i][None], lam_init)
            half = wq_a
            routed = _post(oa, ob, 0, 0, w_out_even_p[i][:half], w_out_even_p[i][half:], xs, mods, l,
                           lng[0][None], lnb[0][None], wr, br)
        else:
            q, k, v = _inproj(xs, mods, l, w_in_odd_b[i], cos128, sin128, segs_odd, C_HD // 4,
                              gains=(q_norm_g[i][None], k_norm_g[i][None]))
            o = _attn_c(q, k, v)
            half = C_HQ * C_HD // 2
            routed = _post(o, o, 0, 1, w_out_odd_b[i][:half], w_out_odd_b[i][half:], xs, mods, l,
                           lng[0][None], lnb[0][None], wr, br)
        x1, h2, route, wcol, counts = routed
        xs = _moe(h2, route, wcol, counts, wg, wu, wd, x1, mods, l, lng[1][None], lnb[1][None],
                  latents_only=(l == DEPTH - 1))
    return xs[None]
```

```python
import functools
import math

import jax
import jax.numpy as jnp
from jax import lax
from jax.experimental import pallas as pl
from jax.experimental.pallas import tpu as pltpu

D_MODEL = 1024
SEQ = 16384
DEPTH = 4
GRID_W = 64
CTX_LEN = 256
WINDOW = 128
ROPE_THETA = 10000.0
A_HQ, A_HKV, A_HD = 8, 2, 64
B_H, B_HD = 4, 64
B_VD = 2 * B_HD
C_HQ, C_HKV, C_HD = 8, 2, 128
N_EXPERTS = 16
N_GROUPS = 4
EXPERTS_PER_GROUP = N_EXPERTS // N_GROUPS
D_EXPERT = 512
ALPHA = (2 * DEPTH) ** 0.25
LN_EPS = 1e-5
QK_EPS = 1e-6
SUBLN_EPS = 1e-5

LANES = 128
TQ = CTX_LEN
TK = 512
UNROLL = 4
MOE_TM = 256
VMEM_LIMIT = 56 * 1024 * 1024
LOG2E = 1.4426950408889634
BF16 = jnp.bfloat16
F32 = jnp.float32


def _cparams(*sem):
    return pltpu.CompilerParams(dimension_semantics=sem, vmem_limit_bytes=VMEM_LIMIT)


def _dot(a, b):
    return jnp.dot(a, b, preferred_element_type=F32)


def _dot_nt(a, b):
    return lax.dot_general(a, b, (((1,), (1,)), ((), ())), preferred_element_type=F32)


def _lane_tile(x, width):
    reps = width // x.shape[-1]
    return x if reps == 1 else jnp.concatenate([x] * reps, axis=-1)


def _mod_kernel(c_ref, w_ref, b_ref, o_ref):
    c = c_ref[...]
    s = c * jax.nn.sigmoid(c)
    o_ref[0] = _dot(s.astype(BF16), w_ref[0].astype(BF16)) + b_ref[0]


def _modulation(c8, w_mod, b_mod):
    depth, d, n6 = w_mod.shape
    tn = 1536
    out = pl.pallas_call(
        _mod_kernel,
        grid=(depth, n6 // tn),
        in_specs=[
            pl.BlockSpec((8, d), lambda l, j: (0, 0)),
            pl.BlockSpec((1, d, tn), lambda l, j: (l, 0, j)),
            pl.BlockSpec((1, 1, tn), lambda l, j: (l, 0, j)),
        ],
        out_specs=pl.BlockSpec((1, 8, tn), lambda l, j: (l, 0, j)),
        out_shape=jax.ShapeDtypeStruct((depth, 8, n6), F32),
        compiler_params=_cparams("arbitrary", "arbitrary"),
        name="modulation",
    )(c8, w_mod, b_mod.reshape(depth, 1, n6))
    m = out[:, :2].reshape(depth, 2, 6, d)
    m = jnp.pad(m, ((0, 0), (0, 0), (0, 2), (0, 0)))
    return m.reshape(depth * 2, 8, d)


def _mod_spec(layer):
    return pl.BlockSpec((1, 8, D_MODEL), lambda i, *_: (2 * layer + jnp.minimum(i, 1), 0, 0))


def _rope_chunk(z, cos, sin, quarter):
    lane = lax.broadcasted_iota(jnp.int32, z.shape, 1)
    first = (lane % (2 * quarter)) < quarter
    nxt = pltpu.roll(z, LANES - quarter, axis=1)
    prv = pltpu.roll(z, quarter, axis=1)
    return z * cos + jnp.where(first, nxt, prv) * sin


def _inproj_kernel(segs, quarter, n_gain, x_ref, mod_ref, w_ref, cos_ref, sin_ref, *rest):
    gains = rest[:n_gain]
    outs = rest[n_gain:]
    sh = mod_ref[0, 0:1, :]
    sc = mod_ref[0, 1:2, :]
    hb = (x_ref[...] * (1.0 + sc) + sh).astype(BF16)
    cos = cos_ref[...]
    sin = sin_ref[...]
    off = 0
    for (width, rope, gain_idx, scale, form), o_ref in zip(segs, outs):
        z = _dot(hb, w_ref[:, off:off + width])
        for c in range(width // LANES):
            zc = z[:, c * LANES:(c + 1) * LANES]
            if gain_idx is not None:
                ms = jnp.mean(zc * zc, axis=-1, keepdims=True)
                zc = zc * lax.rsqrt(ms + QK_EPS) * gains[gain_idx][...]
            if rope:
                zc = _rope_chunk(zc, cos, sin, quarter)
            if scale is not None:
                zc = zc * scale
            if form == "ones":
                o_ref[:, 2 * c * LANES:(2 * c + 1) * LANES] = zc.astype(BF16)
                o_ref[:, (2 * c + 1) * LANES:(2 * c + 2) * LANES] = jnp.ones(zc.shape, BF16)
            elif form == "transposed":
                o_ref[c, 0] = zc.T.astype(BF16)
            else:
                o_ref[:, c * LANES:(c + 1) * LANES] = zc.astype(BF16)
        off += width


def _inproj(x, mods, layer, w, cos, sin, segs, quarter, gains=()):
    n, d = x.shape
    wtot = w.shape[1]
    row = lambda i: (i, 0)
    fixed = lambda i: (0, 0)
    in_specs = [
        pl.BlockSpec((TQ, d), row),
        _mod_spec(layer),
        pl.BlockSpec((d, wtot), fixed),
        pl.BlockSpec((TQ, LANES), row),
        pl.BlockSpec((TQ, LANES), row),
    ] + [pl.BlockSpec((1, LANES), fixed) for _ in gains]
    out_specs, out_shape = [], []
    for width, _, _, _, form in segs:
        if form == "transposed":
            heads = width // LANES
            out_specs.append(pl.BlockSpec((heads, 1, LANES, TQ), lambda i: (0, i, 0, 0)))
            out_shape.append(jax.ShapeDtypeStruct((heads, n // TQ, LANES, TQ), BF16))
        else:
            w_ = width * (2 if form == "ones" else 1)
            out_specs.append(pl.BlockSpec((TQ, w_), row))
            out_shape.append(jax.ShapeDtypeStruct((n, w_), BF16))
    return pl.pallas_call(
        functools.partial(_inproj_kernel, segs, quarter, len(gains)),
        grid=(n // TQ,),
        in_specs=in_specs,
        out_specs=out_specs,
        out_shape=out_shape,
        compiler_params=_cparams("arbitrary"),
        name="inproj",
    )(x, mods, w, cos, sin, *gains)


def _flash(qs_ref, k_ref, vt_ref, m_ref, l_ref, acc_ref, s_refs, is_ctx_block):
    n_chunks = (k_ref.shape[0] - CTX_LEN) // TK
    n_sets = k_ref.shape[1] // LANES
    cols = qs_ref.shape[0] // n_sets
    m_ref[...] = jnp.full(m_ref.shape, -jnp.inf, F32)
    l_ref[...] = jnp.zeros(l_ref.shape, F32)
    acc_ref[...] = jnp.zeros(acc_ref.shape, F32)

    def scores(slot, start, size):
        for g in range(n_sets):
            cs = slice(g * cols, (g + 1) * cols)
            s_refs[slot][0:size, cs] = _dot_nt(k_ref[pl.ds(start, size), g * LANES:(g + 1) * LANES], qs_ref[cs, :])

    def update(slot, first_block, n_blocks):
        s = s_refs[slot][0:n_blocks * TQ, :]
        m_prev = m_ref[0:1, :]
        m_new = jnp.maximum(m_prev, jnp.max(s, axis=0, keepdims=True))
        p = jnp.exp2(s - m_new)
        alpha = jnp.exp2(m_prev - m_new)
        l_ref[...] = jnp.broadcast_to(alpha * l_ref[0:1, :] + jnp.sum(p, axis=0, keepdims=True), l_ref.shape)
        pb = p.astype(BF16)
        for g in range(n_sets):
            cs = slice(g * cols, (g + 1) * cols)
            pv = _dot(vt_ref[g, first_block], pb[0:TQ, cs])
            for b in range(1, n_blocks):
                pv = pv + _dot(vt_ref[g, first_block + b], pb[b * TQ:(b + 1) * TQ, cs])
            acc_ref[:, cs] = acc_ref[:, cs] * alpha[:, cs] + pv
        m_ref[...] = jnp.broadcast_to(m_new, m_ref.shape)

    def lat(j):
        return pl.multiple_of(CTX_LEN + j * TK, CTX_LEN)

    blocks_per_chunk = TK // TQ
    ctx_blocks = CTX_LEN // TQ
    scores(1, 0, CTX_LEN)
    update(1, 0, ctx_blocks)

    @pl.when(jnp.logical_not(is_ctx_block))
    def _():
        scores(0, lat(0), TK)

        def trip(jj, carry):
            j = UNROLL * jj
            for u in range(UNROLL):
                scores((u + 1) % 2, lat(jnp.minimum(j + u + 1, n_chunks - 1)), TK)
                update(u % 2, ctx_blocks + (j + u) * blocks_per_chunk, blocks_per_chunk)
            return carry

        lax.fori_loop(0, n_chunks // UNROLL, trip, 0)


def _flash_out(l_ref, acc_ref, idx):
    cs = slice(idx * TQ, (idx + 1) * TQ)
    return (acc_ref[:, cs] / l_ref[0:1, cs]).T


def _flash_scratch(n_stack):
    rows = n_stack * TQ
    return [
        pltpu.VMEM((rows, LANES), BF16),
        pltpu.VMEM((8, rows), F32),
        pltpu.VMEM((8, rows), F32),
        pltpu.VMEM((LANES, rows), F32),
        pltpu.VMEM((TK, rows), F32),
        pltpu.VMEM((TK, rows), F32),
    ]


def _attn_c_kernel(q_ref, k_ref, vt_ref, o_ref, qs_ref, m_ref, l_ref, acc_ref, s0_ref, s1_ref):
    grp = C_HQ // C_HKV
    for h in range(grp):
        qs_ref[h * TQ:(h + 1) * TQ, :] = q_ref[:, h * C_HD:(h + 1) * C_HD]
    _flash(qs_ref, k_ref, vt_ref, m_ref, l_ref, acc_ref, (s0_ref, s1_ref), pl.program_id(1) == 0)
    for h in range(grp):
        o_ref[:, h * C_HD:(h + 1) * C_HD] = _flash_out(l_ref, acc_ref, h).astype(BF16)


def _attn_c(q, k, vt):
    n = q.shape[0]
    grp = C_HQ // C_HKV
    wq = grp * C_HD
    return pl.pallas_call(
        _attn_c_kernel,
        grid=(C_HKV, n // TQ),
        in_specs=[
            pl.BlockSpec((TQ, wq), lambda g, i: (i, g)),
            pl.BlockSpec((n, C_HD), lambda g, i: (0, g), pipeline_mode=pl.Buffered(1)),
            pl.BlockSpec((1, n // TQ, C_HD, TQ), lambda g, i: (g, 0, 0, 0), pipeline_mode=pl.Buffered(1)),
        ],
        out_specs=pl.BlockSpec((TQ, wq), lambda g, i: (i, g)),
        out_shape=jax.ShapeDtypeStruct((n, C_HQ * C_HD), BF16),
        scratch_shapes=_flash_scratch(grp),
        compiler_params=_cparams("arbitrary", "arbitrary"),
        name="attn_c",
    )(q, k, vt)


B_HEADS_PER_STEP = 2


def _attn_b_kernel(lam_init, q_ref, k_ref, vt_ref, lq1_ref, lk1_ref, lq2_ref, lk2_ref, g_ref, o_ref,
                   qs_ref, m_ref, l_ref, acc_ref, s0_ref, s1_ref):
    for h in range(B_HEADS_PER_STEP):
        q = q_ref[:, h * B_VD:(h + 1) * B_VD]
        lane = lax.broadcasted_iota(jnp.int32, q.shape, 1)
        zero = jnp.zeros_like(q)
        qs_ref[(2 * h) * TQ:(2 * h + 1) * TQ, :] = jnp.where(lane < B_HD, q, zero)
        qs_ref[(2 * h + 1) * TQ:(2 * h + 2) * TQ, :] = jnp.where(lane >= B_HD, q, zero)
    _flash(qs_ref, k_ref, vt_ref, m_ref, l_ref, acc_ref, (s0_ref, s1_ref), pl.program_id(1) == 0)
    lam = (jnp.exp(jnp.sum(lq1_ref[...] * lk1_ref[...], axis=-1, keepdims=True))
           - jnp.exp(jnp.sum(lq2_ref[...] * lk2_ref[...], axis=-1, keepdims=True))) + lam_init
    for h in range(B_HEADS_PER_STEP):
        o = _flash_out(l_ref, acc_ref, 2 * h) - lam * _flash_out(l_ref, acc_ref, 2 * h + 1)
        ms = jnp.mean(o * o, axis=-1, keepdims=True)
        o = o * lax.rsqrt(ms + SUBLN_EPS) * g_ref[...] * (1.0 - lam_init)
        o_ref[:, h * B_VD:(h + 1) * B_VD] = o.astype(BF16)


def _attn_b(q, k, vt, lq1, lk1, lq2, lk2, subln_g, lam_init):
    n = q.shape[0]
    hps = B_HEADS_PER_STEP
    blk = lambda h, i: (i, h)
    col = lambda h, i: (0, h)
    fixed = lambda h, i: (0, 0)
    small = pl.BlockSpec((1, B_HD), fixed)
    return pl.pallas_call(
        functools.partial(_attn_b_kernel, lam_init),
        grid=(B_H // hps, n // TQ),
        in_specs=[
            pl.BlockSpec((TQ, hps * B_VD), blk),
            pl.BlockSpec((n, hps * B_VD), col, pipeline_mode=pl.Buffered(1)),
            pl.BlockSpec((hps, n // TQ, B_VD, TQ), lambda h, i: (h, 0, 0, 0), pipeline_mode=pl.Buffered(1)),
            small, small, small, small,
            pl.BlockSpec((1, B_VD), fixed),
        ],
        out_specs=pl.BlockSpec((TQ, hps * B_VD), blk),
        out_shape=jax.ShapeDtypeStruct((n, B_H * B_VD), BF16),
        scratch_shapes=_flash_scratch(2 * hps),
        compiler_params=_cparams("arbitrary", "arbitrary"),
        name="attn_b",
    )(q, k, vt, lq1, lk1, lq2, lk2, subln_g)


A_WIN_KEYS = TQ + 2 * WINDOW


def _attn_a_kernel(q_ref, k_ref, v_ref, sink_ref, o_ref, qs_ref):
    i = pl.program_id(0)
    n_lat = k_ref.shape[0] - CTX_LEN
    n_blk = A_HQ * A_HD // LANES
    for j in range(n_blk):
        q = q_ref[:, j * LANES:(j + 1) * LANES]
        lane = lax.broadcasted_iota(jnp.int32, q.shape, 1)
        zero = jnp.zeros_like(q)
        qs_ref[(2 * j) * TQ:(2 * j + 1) * TQ, :] = jnp.where(lane < A_HD, q, zero)
        qs_ref[(2 * j + 1) * TQ:(2 * j + 2) * TQ, :] = jnp.where(lane >= A_HD, q, zero)
    sink = jnp.concatenate([jnp.broadcast_to(sink_ref[r:r + 1, :], (TQ, LANES)) for r in range(A_HQ)], axis=0)
    s_c = _dot_nt(qs_ref[...], k_ref[0:CTX_LEN, :])
    vc = v_ref[0:CTX_LEN, :]

    def finish(m, acc):
        o = acc[:, :LANES] / (acc[:, LANES:] + jnp.exp2(sink - m))
        lane = lax.broadcasted_iota(jnp.int32, (TQ, LANES), 1)
        for j in range(n_blk):
            lo = o[(2 * j) * TQ:(2 * j + 1) * TQ, :]
            hi = o[(2 * j + 1) * TQ:(2 * j + 2) * TQ, :]
            o_ref[:, j * LANES:(j + 1) * LANES] = jnp.where(lane < A_HD, lo, hi).astype(BF16)

    @pl.when(i == 0)
    def _():
        m = jnp.maximum(sink, jnp.max(s_c, axis=1, keepdims=True))
        p_c = jnp.exp2(s_c - _lane_tile(m, CTX_LEN)).astype(BF16)
        finish(m, _dot(p_c, vc))

    @pl.when(i > 0)
    def _():
        q0 = (i - 1) * TQ
        ws = jnp.clip(q0 - WINDOW, 0, n_lat - A_WIN_KEYS)
        ws = pl.multiple_of(ws, WINDOW)
        qpos = q0 + lax.broadcasted_iota(jnp.int32, (TQ, A_WIN_KEYS), 0)
        kpos = ws + lax.broadcasted_iota(jnp.int32, (TQ, A_WIN_KEYS), 1)
        ok = jnp.abs(qpos - kpos) <= WINDOW
        s_l = _dot_nt(qs_ref[...], k_ref[pl.ds(CTX_LEN + ws, A_WIN_KEYS), :])
        s_l = jnp.where(ok[None], s_l.reshape(A_HQ, TQ, A_WIN_KEYS), -jnp.inf).reshape(A_HQ * TQ, A_WIN_KEYS)
        m = jnp.maximum(sink, jnp.maximum(jnp.max(s_c, axis=1, keepdims=True),
                                          jnp.max(s_l, axis=1, keepdims=True)))
        p_c = jnp.exp2(s_c - _lane_tile(m, CTX_LEN)).astype(BF16)
        p_l = jnp.exp2(s_l - _lane_tile(m, A_WIN_KEYS)).astype(BF16)
        finish(m, _dot(p_c, vc) + _dot(p_l, v_ref[pl.ds(CTX_LEN + ws, A_WIN_KEYS), :]))


def _attn_a(q, k, v_aug, sink_rows):
    n = q.shape[0]
    wq = A_HQ * A_HD
    wkv = A_HKV * A_HD
    return pl.pallas_call(
        _attn_a_kernel,
        grid=(n // TQ,),
        in_specs=[
            pl.BlockSpec((TQ, wq), lambda i: (i, 0)),
            pl.BlockSpec((n, wkv), lambda i: (0, 0)),
            pl.BlockSpec((n, 2 * wkv), lambda i: (0, 0)),
            pl.BlockSpec((A_HQ, LANES), lambda i: (0, 0)),
        ],
        out_specs=pl.BlockSpec((TQ, wq), lambda i: (i, 0)),
        out_shape=jax.ShapeDtypeStruct((n, wq), BF16),
        scratch_shapes=[pltpu.VMEM((A_HQ * TQ, LANES), BF16)],
        compiler_params=_cparams("arbitrary"),
        name="attn_a",
    )(q, k, v_aug, sink_rows)


def _layer_norm(z, g, b):
    mu = jnp.mean(z, axis=-1, keepdims=True)
    zc = z - mu
    var = jnp.mean(zc * zc, axis=-1, keepdims=True)
    return zc * lax.rsqrt(var + LN_EPS) * g + b


def _route(logits_t, bias_t):
    t = logits_t.shape[1]
    score = [jax.nn.sigmoid(logits_t[e:e + 1, :]) for e in range(N_EXPERTS)]
    sel = [score[e] + bias_t[e:e + 1, :] for e in range(N_EXPERTS)]
    best_g = jnp.zeros((1, t), jnp.int32)
    best_s = None
    for g in range(N_GROUPS):
        a = sel[g * EXPERTS_PER_GROUP:(g + 1) * EXPERTS_PER_GROUP]
        gs = None
        for x in range(EXPERTS_PER_GROUP):
            for y in range(x + 1, EXPERTS_PER_GROUP):
                pair = a[x] + a[y]
                gs = pair if gs is None else jnp.maximum(gs, pair)
        if best_s is None:
            best_s = gs
        else:
            upd = gs > best_s
            best_g = jnp.where(upd, g, best_g)
            best_s = jnp.where(upd, gs, best_s)
    neg = jnp.full((1, t), -jnp.inf, F32)
    masked = [jnp.where(best_g == (e // EXPERTS_PER_GROUP), sel[e], neg) for e in range(N_EXPERTS)]
    i1 = jnp.zeros((1, t), jnp.int32)
    v1 = masked[0]
    for e in range(1, N_EXPERTS):
        upd = masked[e] > v1
        i1 = jnp.where(upd, e, i1)
        v1 = jnp.where(upd, masked[e], v1)
    i2 = jnp.full((1, t), -1, jnp.int32)
    v2 = neg
    for e in range(N_EXPERTS):
        upd = (masked[e] > v2) & (i1 != e)
        i2 = jnp.where(upd, e, i2)
        v2 = jnp.where(upd, masked[e], v2)
    zero = jnp.zeros((1, t), F32)
    w1 = zero
    w2 = zero
    for e in range(N_EXPERTS):
        w1 = w1 + jnp.where(i1 == e, score[e], zero)
        w2 = w2 + jnp.where(i2 == e, score[e], zero)
    den = w1 + w2
    return i1, i2, w1 / den, w2 / den


def _post_kernel(o1_ref, o2_ref, w1_ref, w2_ref, x_ref, mod_ref, lng_ref, lnb_ref, wr_ref, br_ref,
                 x1_ref, h2_ref, route_ref, wcol_ref, cnt_out_ref, cnt_ref):
    @pl.when(pl.program_id(0) == 0)
    def _():
        cnt_ref[...] = jnp.zeros(cnt_ref.shape, F32)

    g1 = mod_ref[0, 2:3, :]
    sh2 = mod_ref[0, 3:4, :]
    sc2 = mod_ref[0, 4:5, :]
    y = _dot(o1_ref[...], w1_ref[...]) + _dot(o2_ref[...], w2_ref[...])
    x1 = _layer_norm(ALPHA * x_ref[...] + g1 * y, lng_ref[...], lnb_ref[...])
    x1_ref[...] = x1
    h2 = x1 * (1.0 + sc2) + sh2
    h2_ref[...] = h2
    h_hi = h2.astype(BF16)
    h_lo = (h2 - h_hi.astype(F32)).astype(BF16)
    wr = wr_ref[...]
    w_hi = wr.astype(BF16)
    w_lo = (wr - w_hi.astype(F32)).astype(BF16)
    logits = _dot(h_hi, w_hi) + (_dot(h_hi, w_lo) + _dot(h_lo, w_hi))
    i1, i2, w1, w2 = _route(logits.T, br_ref[...])

    row = lax.broadcasted_iota(jnp.int32, (LANES, TQ), 0)
    hit1 = row == i1
    hit2 = row == i2
    assigned = jnp.where(hit1 | hit2, 1.0, 0.0)
    t_from = lax.broadcasted_iota(jnp.int32, (TQ, TQ), 0)
    t_to = lax.broadcasted_iota(jnp.int32, (TQ, TQ), 1)
    earlier = jnp.where(t_from < t_to, 1.0, 0.0).astype(BF16)
    rank = cnt_ref[:, 0:1] + _dot(assigned.astype(BF16), earlier)
    r1 = jnp.sum(jnp.where(hit1, rank, 0.0), axis=0, keepdims=True).astype(jnp.int32)
    r2 = jnp.sum(jnp.where(hit2, rank, 0.0), axis=0, keepdims=True).astype(jnp.int32)
    cnt_ref[...] = cnt_ref[...] + jnp.sum(assigned, axis=1, keepdims=True)
    cnt_out_ref[...] = cnt_ref[...]

    row8 = lax.broadcasted_iota(jnp.int32, (8, TQ), 0)
    route_ref[0] = jnp.where(row8 == 0, i1, jnp.where(row8 == 1, i2, jnp.where(row8 == 2, r1,
                             jnp.where(row8 == 3, r2, 0))))
    wcol_ref[...] = jnp.where(row == 0, w1, jnp.where(row == 1, w2, 0.0)).T


def _post(o1, o2, o1_col, o2_col, w1, w2, x, mods, layer, ln_g, ln_b, wr, br):
    n, d = x.shape
    half = w1.shape[0]
    row = lambda i: (i, 0)
    fixed = lambda i: (0, 0)
    return pl.pallas_call(
        _post_kernel,
        grid=(n // TQ,),
        in_specs=[
            pl.BlockSpec((TQ, half), lambda i: (i, o1_col)),
            pl.BlockSpec((TQ, half), lambda i: (i, o2_col)),
            pl.BlockSpec((half, d), fixed),
            pl.BlockSpec((half, d), fixed),
            pl.BlockSpec((TQ, d), row),
            _mod_spec(layer),
            pl.BlockSpec((1, d), fixed),
            pl.BlockSpec((1, d), fixed),
            pl.BlockSpec((d, LANES), fixed),
            pl.BlockSpec((LANES, 1), fixed),
        ],
        out_specs=[
            pl.BlockSpec((TQ, d), row),
            pl.BlockSpec((TQ, d), row),
            pl.BlockSpec((1, 8, TQ), lambda i: (i, 0, 0)),
            pl.BlockSpec((TQ, LANES), row),
            pl.BlockSpec((LANES, LANES), fixed),
        ],
        out_shape=[
            jax.ShapeDtypeStruct((n, d), F32),
            jax.ShapeDtypeStruct((n, d), F32),
            jax.ShapeDtypeStruct((n // TQ, 8, TQ), jnp.int32),
            jax.ShapeDtypeStruct((n, LANES), F32),
            jax.ShapeDtypeStruct((LANES, LANES), F32),
        ],
        scratch_shapes=[pltpu.VMEM((LANES, LANES), F32)],
        compiler_params=_cparams("arbitrary"),
        name="post_attn",
    )(o1, o2, w1, w2, x, mods, ln_g, ln_b, wr, br)


def _moe_layout(route, counts, n):
    n_tiles_max = (2 * n) // MOE_TM + N_EXPERTS
    cnt = counts[:N_EXPERTS, 0].astype(jnp.int32)
    tiles = (cnt + MOE_TM - 1) // MOE_TM
    tiles_end = jnp.cumsum(tiles)
    seg_start = (tiles_end - tiles) * MOE_TM
    experts = jnp.arange(N_EXPERTS)

    def start_of(e):
        return jnp.sum(jnp.where(e[..., None] == experts, seg_start, 0), axis=-1)

    pos = jnp.concatenate([start_of(route[:, 0]) + route[:, 2],
                           start_of(route[:, 1]) + route[:, 3]], axis=1)[:, None, :]
    n_used = tiles_end[-1]
    j = jnp.minimum(jnp.arange(n_tiles_max), n_used - 1)
    tile_expert = jnp.sum(j[:, None] >= tiles_end[None, :], axis=1).astype(jnp.int32)
    last_tiles = jnp.maximum(tiles_end - 1, 0)
    tail_tiles = jnp.minimum(n_used + experts, n_tiles_max - 1)
    live = jnp.concatenate([tiles > 0, n_used + experts < n_tiles_max])
    zero_tiles = jnp.concatenate([last_tiles, tail_tiles, live]).astype(jnp.int32)
    return pos.astype(jnp.int32), tile_expert, n_used.reshape(1).astype(jnp.int32), zero_tiles, n_tiles_max


def _for_rows(fn):
    sublanes = 8

    def group(t, carry):
        base = pl.multiple_of(t * sublanes, sublanes)
        for u in range(sublanes):
            fn(base + u)
        return carry

    lax.fori_loop(0, TQ // sublanes, group, 0)


def _scatter_kernel(zt_ref, pos_ref, h_ref, xs_ref, zero_ref, sem, zsem):
    def zero_copy(t):
        return pltpu.make_async_copy(zero_ref, xs_ref.at[pl.ds(zt_ref[t] * MOE_TM, MOE_TM)], zsem)

    @pl.when(pl.program_id(0) == 0)
    def _():
        zero_ref[...] = jnp.zeros(zero_ref.shape, F32)
        for t in range(2 * N_EXPERTS):
            @pl.when(zt_ref[2 * N_EXPERTS + t] != 0)
            def _():
                zero_copy(t).start()
        for t in range(2 * N_EXPERTS):
            @pl.when(zt_ref[2 * N_EXPERTS + t] != 0)
            def _():
                zero_copy(t).wait()

    def row_copy(r, k):
        return pltpu.make_async_copy(h_ref.at[pl.ds(r, 1)], xs_ref.at[pl.ds(pos_ref[0, 0, k * TQ + r], 1)], sem)

    _for_rows(lambda r: (row_copy(r, 0).start(), row_copy(r, 1).start()))
    _for_rows(lambda r: (row_copy(r, 0).wait(), row_copy(r, 1).wait()))


def _scatter(zero_tiles, pos, h2, n_rows):
    n, d = h2.shape
    return pl.pallas_call(
        _scatter_kernel,
        grid_spec=pltpu.PrefetchScalarGridSpec(
            num_scalar_prefetch=1,
            grid=(n // TQ,),
            in_specs=[
                pl.BlockSpec((1, 1, 2 * TQ), lambda i, zt: (i, 0, 0), memory_space=pltpu.SMEM),
                pl.BlockSpec((TQ, d), lambda i, zt: (i, 0)),
            ],
            out_specs=pl.BlockSpec(memory_space=pl.ANY),
            scratch_shapes=[pltpu.VMEM((MOE_TM, d), F32), pltpu.SemaphoreType.DMA(()),
                            pltpu.SemaphoreType.DMA(())],
        ),
        out_shape=jax.ShapeDtypeStruct((n_rows, d), F32),
        compiler_params=_cparams("arbitrary"),
        name="moe_scatter",
    )(zero_tiles, pos, h2)


def _expert_kernel(te_ref, nt_ref, x_ref, wg_ref, wu_ref, wd_ref, y_ref, wgb_ref, wub_ref, wdb_ref):
    j = pl.program_id(0)
    used = j < nt_ref[0]
    new_expert = jnp.logical_or(j == 0, te_ref[j] != te_ref[jnp.maximum(j - 1, 0)])

    @pl.when(jnp.logical_and(used, new_expert))
    def _():
        wgb_ref[...] = wg_ref[0].astype(BF16)
        wub_ref[...] = wu_ref[0].astype(BF16)
        wdb_ref[...] = wd_ref[0].astype(BF16)

    @pl.when(used)
    def _():
        x = x_ref[...].astype(BF16)
        g = _dot(x, wgb_ref[...])
        u = _dot(x, wub_ref[...])
        a = (g * jax.nn.sigmoid(g)) * u
        y_ref[...] = _dot(a.astype(BF16), wdb_ref[...])

    @pl.when(jnp.logical_not(used))
    def _():
        y_ref[...] = jnp.zeros(y_ref.shape, F32)


def _experts(tile_expert, n_used, xs, wg, wu, wd, layer, n_tiles_max):
    _, d = xs.shape
    f = wg.shape[3]
    weight = lambda j, te, nt: (layer, te[j], 0, 0)
    return pl.pallas_call(
        _expert_kernel,
        grid_spec=pltpu.PrefetchScalarGridSpec(
            num_scalar_prefetch=2,
            grid=(n_tiles_max,),
            in_specs=[
                pl.BlockSpec((MOE_TM, d), lambda j, te, nt: (jnp.minimum(j, nt[0] - 1), 0)),
                pl.BlockSpec((None, 1, d, f), weight),
                pl.BlockSpec((None, 1, d, f), weight),
                pl.BlockSpec((None, 1, f, d), weight),
            ],
            out_specs=pl.BlockSpec((MOE_TM, d), lambda j, te, nt: (j, 0)),
            scratch_shapes=[pltpu.VMEM((d, f), BF16), pltpu.VMEM((d, f), BF16), pltpu.VMEM((f, d), BF16)],
        ),
        out_shape=jax.ShapeDtypeStruct(xs.shape, F32),
        compiler_params=_cparams("arbitrary"),
        name="moe_experts",
    )(tile_expert, n_used, xs, wg, wu, wd)


def _combine_kernel(pos_ref, pos_next_ref, ys_ref, x_ref, wcol_ref, mod_ref, lng_ref, lnb_ref, o_ref,
                    buf_ref, sem):
    i = pl.program_id(0)
    n_steps = pl.num_programs(0)
    slot = lax.rem(i, 2)

    def row_copy(p_ref, dst_slot, r, k):
        return pltpu.make_async_copy(ys_ref.at[pl.ds(p_ref[0, 0, k * TQ + r], 1)],
                                     buf_ref.at[dst_slot, k, pl.ds(r, 1)], sem.at[dst_slot])

    def gather(p_ref, dst_slot):
        _for_rows(lambda r: (row_copy(p_ref, dst_slot, r, 0).start(), row_copy(p_ref, dst_slot, r, 1).start()))

    @pl.when(i == 0)
    def _():
        gather(pos_ref, 0)

    @pl.when(i + 1 < n_steps)
    def _():
        gather(pos_next_ref, 1 - slot)

    _for_rows(lambda r: (row_copy(pos_ref, slot, r, 0).wait(), row_copy(pos_ref, slot, r, 1).wait()))

    w = wcol_ref[...]
    moe = w[:, 0:1] * buf_ref[slot, 0] + w[:, 1:2] * buf_ref[slot, 1]
    g2 = mod_ref[0, 5:6, :]
    o_ref[...] = _layer_norm(ALPHA * x_ref[...] + g2 * moe, lng_ref[...], lnb_ref[...])


def _combine(pos, ys, x1, wcol, mods, layer, ln_g, ln_b, latents_only):
    n, d = x1.shape
    n_steps = n // TQ
    row = lambda i: (i, 0)
    fixed = lambda i: (0, 0)
    if latents_only:
        out_rows, out_map = n - CTX_LEN, lambda i: (jnp.maximum(i - CTX_LEN // TQ, 0), 0)
    else:
        out_rows, out_map = n, row
    return pl.pallas_call(
        _combine_kernel,
        grid=(n_steps,),
        in_specs=[
            pl.BlockSpec((1, 1, 2 * TQ), lambda i: (i, 0, 0), memory_space=pltpu.SMEM),
            pl.BlockSpec((1, 1, 2 * TQ), lambda i: (jnp.minimum(i + 1, n_steps - 1), 0, 0),
                         memory_space=pltpu.SMEM),
            pl.BlockSpec(memory_space=pl.ANY),
            pl.BlockSpec((TQ, d), row),
            pl.BlockSpec((TQ, LANES), row),
            _mod_spec(layer),
            pl.BlockSpec((1, d), fixed),
            pl.BlockSpec((1, d), fixed),
        ],
        out_specs=pl.BlockSpec((TQ, d), out_map),
        out_shape=jax.ShapeDtypeStruct((out_rows, d), F32),
        scratch_shapes=[pltpu.VMEM((2, 2, TQ, d), F32), pltpu.SemaphoreType.DMA((2,))],
        compiler_params=_cparams("arbitrary"),
        name="moe_combine",
    )(pos, pos, ys, x1, wcol, mods, ln_g, ln_b)


def _moe(h2, route, wcol, counts, wg, wu, wd, x1, mods, layer, ln_g, ln_b, latents_only):
    n = h2.shape[0]
    pos, tile_expert, n_used, zero_tiles, n_tiles_max = _moe_layout(route, counts, n)
    xs = _scatter(zero_tiles, pos, h2, n_tiles_max * MOE_TM)
    ys = _experts(tile_expert, n_used, xs, wg, wu, wd, layer, n_tiles_max)
    return _combine(pos, ys, x1, wcol, mods, layer, ln_g, ln_b, latents_only)


def _rope_tables(n_lat, head_dim):
    quarter = head_dim // 4
    n_rows = n_lat // GRID_W
    inv_freq = ROPE_THETA ** (-jnp.arange(quarter, dtype=F32) / quarter)
    ang_r = jnp.arange(n_rows)[:, None].astype(F32) * inv_freq
    ang_c = jnp.arange(GRID_W)[:, None].astype(F32) * inv_freq
    by_row = lambda a: jnp.repeat(a, GRID_W, axis=0)
    by_col = lambda a: jnp.tile(a, (n_rows, 1))
    cos_r, sin_r = by_row(jnp.cos(ang_r)), by_row(jnp.sin(ang_r))
    cos_c, sin_c = by_col(jnp.cos(ang_c)), by_col(jnp.sin(ang_c))
    cos = jnp.concatenate([cos_r, cos_r, cos_c, cos_c], axis=1)
    sin = jnp.concatenate([-sin_r, sin_r, -sin_c, sin_c], axis=1)
    reps = LANES // head_dim
    cos = jnp.tile(cos, (1, reps))
    sin = jnp.tile(sin, (1, reps))
    cos = jnp.concatenate([jnp.ones((CTX_LEN, LANES), F32), cos], axis=0)
    sin = jnp.concatenate([jnp.zeros((CTX_LEN, LANES), F32), sin], axis=0)
    return cos, sin


def _a_head_order():
    grp = A_HQ // A_HKV
    order = []
    for j in range(grp):
        for g in range(A_HKV):
            order.append(g * grp + j)
    return order


def kernel(x, c, ctx, c_ctx, w_mod, b_mod, ln_g, ln_b, w_in_even, w_out_even, sink_logits, lam_q1, lam_k1,
           lam_q2, lam_k2, subln_g, w_in_odd, w_out_odd, q_norm_g, k_norm_g, w_router, b_router, w_gate,
           w_up, w_down):
    n_lat = x.shape[1]
    assert x.shape == (1, n_lat, D_MODEL) and ctx.shape == (1, CTX_LEN, D_MODEL)
    assert n_lat % max(UNROLL * TK, MOE_TM) == 0 and n_lat % GRID_W == 0
    xs = jnp.concatenate([ctx[0], x[0]], axis=0)

    c8 = jnp.zeros((8, D_MODEL), F32).at[0].set(c_ctx).at[1].set(c[0])
    mods = _modulation(c8, w_mod, b_mod)

    cos64, sin64 = _rope_tables(n_lat, A_HD)
    cos128, sin128 = _rope_tables(n_lat, C_HD)

    order = _a_head_order()
    wq_a = A_HQ * A_HD
    head = lambda h: slice(h * A_HD, (h + 1) * A_HD)
    w_in_even_p = jnp.concatenate([w_in_even[:, :, head(h)] for h in order] + [w_in_even[:, :, wq_a:]],
                                  axis=2).astype(BF16)
    w_out_even_p = jnp.concatenate([w_out_even[:, head(h)] for h in order] + [w_out_even[:, wq_a:]],
                                   axis=1).astype(BF16)
    w_in_odd_b = w_in_odd.astype(BF16)
    w_out_odd_b = w_out_odd.astype(BF16)
    wg, wu, wd = w_gate, w_up, w_down
    wr = jnp.pad(w_router, ((0, 0), (0, LANES - N_EXPERTS)))
    br = jnp.pad(b_router.astype(F32), (0, LANES - N_EXPERTS)).reshape(LANES, 1)

    segs_even = ((A_HQ * A_HD, True, None, (A_HD ** -0.5) * LOG2E, None), (A_HKV * A_HD, True, None, None, None),
                 (A_HKV * A_HD, False, None, None, "ones"),
                 (B_H * 2 * B_HD, True, None, (B_HD ** -0.5) * LOG2E, None),
                 (B_H * 2 * B_HD, True, None, None, None), (B_H * B_VD, False, None, None, "transposed"))
    segs_odd = ((C_HQ * C_HD, True, 0, (C_HD ** -0.5) * LOG2E, None), (C_HKV * C_HD, True, 1, None, None),
                (C_HKV * C_HD, False, None, None, "transposed"))

    for l in range(DEPTH):
        i = l // 2
        lng = ln_g[l]
        lnb = ln_b[l]
        if l % 2 == 0:
            lam_init = 0.8 - 0.6 * math.exp(-0.3 * l)
            aq, ak, av, bq, bk, bv = _inproj(xs, mods, l, w_in_even_p[i], cos64, sin64, segs_even, A_HD // 4)
            sink_rows = jnp.broadcast_to(
                (sink_logits[i][jnp.array(order)] * LOG2E)[:, None], (A_HQ, LANES)).astype(F32)
            oa = _attn_a(aq, ak, av, sink_rows)
            ob = _attn_b(bq, bk, bv, lam_q1[i][None], lam_k1[i][None], lam_q2[i][None], lam_k2[i][None],
                         subln_g[i][None], lam_init)
            half = wq_a
            routed = _post(oa, ob, 0, 0, w_out_even_p[i][:half], w_out_even_p[i][half:], xs, mods, l,
                           lng[0][None], lnb[0][None], wr, br)
        else:
            q, k, v = _inproj(xs, mods, l, w_in_odd_b[i], cos128, sin128, segs_odd, C_HD // 4,
                              gains=(q_norm_g[i][None], k_norm_g[i][None]))
            o = _attn_c(q, k, v)
            half = C_HQ * C_HD // 2
            routed = _post(o, o, 0, 1, w_out_odd_b[i][:half], w_out_odd_b[i][half:], xs, mods, l,
                           lng[0][None], lnb[0][None], wr, br)
        x1, h2, route, wcol, counts = routed
        xs = _moe(h2, route, wcol, counts, wg, wu, wd, x1, mods, l, lng[1][None], lnb[1][None],
                  latents_only=(l == DEPTH - 1))
    return xs[None]
```

```python
import functools
import math

import jax
import jax.numpy as jnp
from jax import lax
from jax.experimental import pallas as pl
from jax.experimental.pallas import tpu as pltpu

D_MODEL = 1024
SEQ = 16384
DEPTH = 4
GRID_W = 64
CTX_LEN = 256
WINDOW = 128
ROPE_THETA = 10000.0
A_HQ, A_HKV, A_HD = 8, 2, 64
B_H, B_HD = 4, 64
B_VD = 2 * B_HD
C_HQ, C_HKV, C_HD = 8, 2, 128
N_EXPERTS = 16
N_GROUPS = 4
EXPERTS_PER_GROUP = N_EXPERTS // N_GROUPS
D_EXPERT = 512
ALPHA = (2 * DEPTH) ** 0.25
LN_EPS = 1e-5
QK_EPS = 1e-6
SUBLN_EPS = 1e-5

LANES = 128
TQ = CTX_LEN
TK = 512
UNROLL = 4
MOE_TM = 256
VMEM_LIMIT = 56 * 1024 * 1024
LOG2E = 1.4426950408889634
BF16 = jnp.bfloat16
F32 = jnp.float32


def _cparams(*sem):
    return pltpu.CompilerParams(dimension_semantics=sem, vmem_limit_bytes=VMEM_LIMIT)


def _dot(a, b):
    return jnp.dot(a, b, preferred_element_type=F32)


def _dot_nt(a, b):
    return lax.dot_general(a, b, (((1,), (1,)), ((), ())), preferred_element_type=F32)


def _lane_tile(x, width):
    reps = width // x.shape[-1]
    return x if reps == 1 else jnp.concatenate([x] * reps, axis=-1)


def _mod_kernel(c_ref, w_ref, b_ref, o_ref):
    c = c_ref[...]
    s = c * jax.nn.sigmoid(c)
    o_ref[0] = _dot(s.astype(BF16), w_ref[0].astype(BF16)) + b_ref[0]


def _modulation(c8, w_mod, b_mod):
    depth, d, n6 = w_mod.shape
    tn = 1536
    out = pl.pallas_call(
        _mod_kernel,
        grid=(depth, n6 // tn),
        in_specs=[
            pl.BlockSpec((8, d), lambda l, j: (0, 0)),
            pl.BlockSpec((1, d, tn), lambda l, j: (l, 0, j)),
            pl.BlockSpec((1, 1, tn), lambda l, j: (l, 0, j)),
        ],
        out_specs=pl.BlockSpec((1, 8, tn), lambda l, j: (l, 0, j)),
        out_shape=jax.ShapeDtypeStruct((depth, 8, n6), F32),
        compiler_params=_cparams("arbitrary", "arbitrary"),
        name="modulation",
    )(c8, w_mod, b_mod.reshape(depth, 1, n6))
    m = out[:, :2].reshape(depth, 2, 6, d)
    m = jnp.pad(m, ((0, 0), (0, 0), (0, 2), (0, 0)))
    return m.reshape(depth * 2, 8, d)


def _mod_spec(layer):
    return pl.BlockSpec((1, 8, D_MODEL), lambda i, *_: (2 * layer + jnp.minimum(i, 1), 0, 0))


def _rope_chunk(z, cos, sin, quarter):
    lane = lax.broadcasted_iota(jnp.int32, z.shape, 1)
    first = (lane % (2 * quarter)) < quarter
    nxt = pltpu.roll(z, LANES - quarter, axis=1)
    prv = pltpu.roll(z, quarter, axis=1)
    return z * cos + jnp.where(first, nxt, prv) * sin


def _inproj_kernel(segs, quarter, n_gain, x_ref, mod_ref, w_ref, cos_ref, sin_ref, *rest):
    gains = rest[:n_gain]
    outs = rest[n_gain:]
    sh = mod_ref[0, 0:1, :]
    sc = mod_ref[0, 1:2, :]
    hb = (x_ref[...] * (1.0 + sc) + sh).astype(BF16)
    cos = cos_ref[...]
    sin = sin_ref[...]
    off = 0
    for (width, rope, gain_idx, scale, form), o_ref in zip(segs, outs):
        z = _dot(hb, w_ref[:, off:off + width])
        for c in range(width // LANES):
            zc = z[:, c * LANES:(c + 1) * LANES]
            if gain_idx is not None:
                ms = jnp.mean(zc * zc, axis=-1, keepdims=True)
                zc = zc * lax.rsqrt(ms + QK_EPS) * gains[gain_idx][...]
            if rope:
                zc = _rope_chunk(zc, cos, sin, quarter)
            if scale is not None:
                zc = zc * scale
            if form == "ones":
                o_ref[:, 2 * c * LANES:(2 * c + 1) * LANES] = zc.astype(BF16)
                o_ref[:, (2 * c + 1) * LANES:(2 * c + 2) * LANES] = jnp.ones(zc.shape, BF16)
            elif form == "transposed":
                o_ref[c, 0] = zc.T.astype(BF16)
            else:
                o_ref[:, c * LANES:(c + 1) * LANES] = zc.astype(BF16)
        off += width


def _inproj(x, mods, layer, w, cos, sin, segs, quarter, gains=()):
    n, d = x.shape
    wtot = w.shape[1]
    row = lambda i: (i, 0)
    fixed = lambda i: (0, 0)
    in_specs = [
        pl.BlockSpec((TQ, d), row),
        _mod_spec(layer),
        pl.BlockSpec((d, wtot), fixed),
        pl.BlockSpec((TQ, LANES), row),
        pl.BlockSpec((TQ, LANES), row),
    ] + [pl.BlockSpec((1, LANES), fixed) for _ in gains]
    out_specs, out_shape = [], []
    for width, _, _, _, form in segs:
        if form == "transposed":
            heads = width // LANES
            out_specs.append(pl.BlockSpec((heads, 1, LANES, TQ), lambda i: (0, i, 0, 0)))
            out_shape.append(jax.ShapeDtypeStruct((heads, n // TQ, LANES, TQ), BF16))
        else:
            w_ = width * (2 if form == "ones" else 1)
            out_specs.append(pl.BlockSpec((TQ, w_), row))
            out_shape.append(jax.ShapeDtypeStruct((n, w_), BF16))
    return pl.pallas_call(
        functools.partial(_inproj_kernel, segs, quarter, len(gains)),
        grid=(n // TQ,),
        in_specs=in_specs,
        out_specs=out_specs,
        out_shape=out_shape,
        compiler_params=_cparams("arbitrary"),
        name="inproj",
    )(x, mods, w, cos, sin, *gains)


def _transposed(q):
    return q.astype(F32).T.astype(BF16)


def _flash(qt_ref, k_ref, vt_ref, m_ref, l_ref, acc_ref, s_refs, is_ctx_block):
    n_chunks = (k_ref.shape[0] - CTX_LEN) // TK
    n_sets = k_ref.shape[1] // LANES
    cols = qt_ref.shape[1] // n_sets
    m_ref[...] = jnp.full(m_ref.shape, -jnp.inf, F32)
    l_ref[...] = jnp.zeros(l_ref.shape, F32)
    acc_ref[...] = jnp.zeros(acc_ref.shape, F32)

    def scores(slot, start, size):
        for g in range(n_sets):
            cs = slice(g * cols, (g + 1) * cols)
            s_refs[slot][0:size, cs] = _dot(k_ref[pl.ds(start, size), g * LANES:(g + 1) * LANES], qt_ref[:, cs])

    def update(slot, first_block, n_blocks):
        s = s_refs[slot][0:n_blocks * TQ, :]
        m_prev = m_ref[0:1, :]
        m_new = jnp.maximum(m_prev, jnp.max(s, axis=0, keepdims=True))
        p = jnp.exp2(s - m_new)
        alpha = jnp.exp2(m_prev - m_new)
        l_ref[...] = jnp.broadcast_to(alpha * l_ref[0:1, :] + jnp.sum(p, axis=0, keepdims=True), l_ref.shape)
        pb = p.astype(BF16)
        for g in range(n_sets):
            cs = slice(g * cols, (g + 1) * cols)
            pv = _dot(vt_ref[g, first_block], pb[0:TQ, cs])
            for b in range(1, n_blocks):
                pv = pv + _dot(vt_ref[g, first_block + b], pb[b * TQ:(b + 1) * TQ, cs])
            acc_ref[:, cs] = acc_ref[:, cs] * alpha[:, cs] + pv
        m_ref[...] = jnp.broadcast_to(m_new, m_ref.shape)

    def lat(j):
        return pl.multiple_of(CTX_LEN + j * TK, CTX_LEN)

    blocks_per_chunk = TK // TQ
    ctx_blocks = CTX_LEN // TQ
    scores(1, 0, CTX_LEN)
    update(1, 0, ctx_blocks)

    @pl.when(jnp.logical_not(is_ctx_block))
    def _():
        scores(0, lat(0), TK)

        def trip(jj, carry):
            j = UNROLL * jj
            for u in range(UNROLL):
                scores((u + 1) % 2, lat(jnp.minimum(j + u + 1, n_chunks - 1)), TK)
                update(u % 2, ctx_blocks + (j + u) * blocks_per_chunk, blocks_per_chunk)
            return carry

        lax.fori_loop(0, n_chunks // UNROLL, trip, 0)


def _flash_out(l_ref, acc_ref, idx):
    cs = slice(idx * TQ, (idx + 1) * TQ)
    return (acc_ref[:, cs] / l_ref[0:1, cs]).T


def _flash_scratch(n_stack):
    rows = n_stack * TQ
    return [
        pltpu.VMEM((LANES, rows), BF16),
        pltpu.VMEM((8, rows), F32),
        pltpu.VMEM((8, rows), F32),
        pltpu.VMEM((LANES, rows), F32),
        pltpu.VMEM((TK, rows), F32),
        pltpu.VMEM((TK, rows), F32),
    ]


def _attn_c_kernel(q_ref, k_ref, vt_ref, o_ref, qt_ref, m_ref, l_ref, acc_ref, s0_ref, s1_ref):
    grp = C_HQ // C_HKV
    for h in range(grp):
        qt_ref[:, h * TQ:(h + 1) * TQ] = _transposed(q_ref[:, h * C_HD:(h + 1) * C_HD])
    _flash(qt_ref, k_ref, vt_ref, m_ref, l_ref, acc_ref, (s0_ref, s1_ref), pl.program_id(1) == 0)
    for h in range(grp):
        o_ref[:, h * C_HD:(h + 1) * C_HD] = _flash_out(l_ref, acc_ref, h).astype(BF16)


def _attn_c(q, k, vt):
    n = q.shape[0]
    grp = C_HQ // C_HKV
    wq = grp * C_HD
    return pl.pallas_call(
        _attn_c_kernel,
        grid=(C_HKV, n // TQ),
        in_specs=[
            pl.BlockSpec((TQ, wq), lambda g, i: (i, g)),
            pl.BlockSpec((n, C_HD), lambda g, i: (0, g), pipeline_mode=pl.Buffered(1)),
            pl.BlockSpec((1, n // TQ, C_HD, TQ), lambda g, i: (g, 0, 0, 0), pipeline_mode=pl.Buffered(1)),
        ],
        out_specs=pl.BlockSpec((TQ, wq), lambda g, i: (i, g)),
        out_shape=jax.ShapeDtypeStruct((n, C_HQ * C_HD), BF16),
        scratch_shapes=_flash_scratch(grp),
        compiler_params=_cparams("arbitrary", "arbitrary"),
        name="attn_c",
    )(q, k, vt)


B_HEADS_PER_STEP = 2


def _attn_b_kernel(lam_init, q_ref, k_ref, vt_ref, lq1_ref, lk1_ref, lq2_ref, lk2_ref, g_ref, o_ref,
                   qt_ref, m_ref, l_ref, acc_ref, s0_ref, s1_ref):
    for h in range(B_HEADS_PER_STEP):
        q = q_ref[:, h * B_VD:(h + 1) * B_VD]
        lane = lax.broadcasted_iota(jnp.int32, q.shape, 1)
        zero = jnp.zeros_like(q)
        qt_ref[:, (2 * h) * TQ:(2 * h + 1) * TQ] = _transposed(jnp.where(lane < B_HD, q, zero))
        qt_ref[:, (2 * h + 1) * TQ:(2 * h + 2) * TQ] = _transposed(jnp.where(lane >= B_HD, q, zero))
    _flash(qt_ref, k_ref, vt_ref, m_ref, l_ref, acc_ref, (s0_ref, s1_ref), pl.program_id(1) == 0)
    lam = (jnp.exp(jnp.sum(lq1_ref[...] * lk1_ref[...], axis=-1, keepdims=True))
           - jnp.exp(jnp.sum(lq2_ref[...] * lk2_ref[...], axis=-1, keepdims=True))) + lam_init
    for h in range(B_HEADS_PER_STEP):
        o = _flash_out(l_ref, acc_ref, 2 * h) - lam * _flash_out(l_ref, acc_ref, 2 * h + 1)
        ms = jnp.mean(o * o, axis=-1, keepdims=True)
        o = o * lax.rsqrt(ms + SUBLN_EPS) * g_ref[...] * (1.0 - lam_init)
        o_ref[:, h * B_VD:(h + 1) * B_VD] = o.astype(BF16)


def _attn_b(q, k, vt, lq1, lk1, lq2, lk2, subln_g, lam_init):
    n = q.shape[0]
    hps = B_HEADS_PER_STEP
    blk = lambda h, i: (i, h)
    col = lambda h, i: (0, h)
    fixed = lambda h, i: (0, 0)
    small = pl.BlockSpec((1, B_HD), fixed)
    return pl.pallas_call(
        functools.partial(_attn_b_kernel, lam_init),
        grid=(B_H // hps, n // TQ),
        in_specs=[
            pl.BlockSpec((TQ, hps * B_VD), blk),
            pl.BlockSpec((n, hps * B_VD), col, pipeline_mode=pl.Buffered(1)),
            pl.BlockSpec((hps, n // TQ, B_VD, TQ), lambda h, i: (h, 0, 0, 0), pipeline_mode=pl.Buffered(1)),
            small, small, small, small,
            pl.BlockSpec((1, B_VD), fixed),
        ],
        out_specs=pl.BlockSpec((TQ, hps * B_VD), blk),
        out_shape=jax.ShapeDtypeStruct((n, B_H * B_VD), BF16),
        scratch_shapes=_flash_scratch(2 * hps),
        compiler_params=_cparams("arbitrary", "arbitrary"),
        name="attn_b",
    )(q, k, vt, lq1, lk1, lq2, lk2, subln_g)


A_WIN_KEYS = TQ + 2 * WINDOW


def _attn_a_kernel(q_ref, k_ref, v_ref, sink_ref, o_ref, qs_ref):
    i = pl.program_id(0)
    n_lat = k_ref.shape[0] - CTX_LEN
    n_blk = A_HQ * A_HD // LANES
    for j in range(n_blk):
        q = q_ref[:, j * LANES:(j + 1) * LANES]
        lane = lax.broadcasted_iota(jnp.int32, q.shape, 1)
        zero = jnp.zeros_like(q)
        qs_ref[(2 * j) * TQ:(2 * j + 1) * TQ, :] = jnp.where(lane < A_HD, q, zero)
        qs_ref[(2 * j + 1) * TQ:(2 * j + 2) * TQ, :] = jnp.where(lane >= A_HD, q, zero)
    sink = jnp.concatenate([jnp.broadcast_to(sink_ref[r:r + 1, :], (TQ, LANES)) for r in range(A_HQ)], axis=0)
    s_c = _dot_nt(qs_ref[...], k_ref[0:CTX_LEN, :])
    vc = v_ref[0:CTX_LEN, :]

    def finish(m, acc):
        o = acc[:, :LANES] / (acc[:, LANES:] + jnp.exp2(sink - m))
        lane = lax.broadcasted_iota(jnp.int32, (TQ, LANES), 1)
        for j in range(n_blk):
            lo = o[(2 * j) * TQ:(2 * j + 1) * TQ, :]
            hi = o[(2 * j + 1) * TQ:(2 * j + 2) * TQ, :]
            o_ref[:, j * LANES:(j + 1) * LANES] = jnp.where(lane < A_HD, lo, hi).astype(BF16)

    @pl.when(i == 0)
    def _():
        m = jnp.maximum(sink, jnp.max(s_c, axis=1, keepdims=True))
        p_c = jnp.exp2(s_c - _lane_tile(m, CTX_LEN)).astype(BF16)
        finish(m, _dot(p_c, vc))

    @pl.when(i > 0)
    def _():
        q0 = (i - 1) * TQ
        ws = jnp.clip(q0 - WINDOW, 0, n_lat - A_WIN_KEYS)
        ws = pl.multiple_of(ws, WINDOW)
        qpos = q0 + lax.broadcasted_iota(jnp.int32, (TQ, A_WIN_KEYS), 0)
        kpos = ws + lax.broadcasted_iota(jnp.int32, (TQ, A_WIN_KEYS), 1)
        ok = jnp.abs(qpos - kpos) <= WINDOW
        s_l = _dot_nt(qs_ref[...], k_ref[pl.ds(CTX_LEN + ws, A_WIN_KEYS), :])
        s_l = jnp.where(ok[None], s_l.reshape(A_HQ, TQ, A_WIN_KEYS), -jnp.inf).reshape(A_HQ * TQ, A_WIN_KEYS)
        m = jnp.maximum(sink, jnp.maximum(jnp.max(s_c, axis=1, keepdims=True),
                                          jnp.max(s_l, axis=1, keepdims=True)))
        p_c = jnp.exp2(s_c - _lane_tile(m, CTX_LEN)).astype(BF16)
        p_l = jnp.exp2(s_l - _lane_tile(m, A_WIN_KEYS)).astype(BF16)
        finish(m, _dot(p_c, vc) + _dot(p_l, v_ref[pl.ds(CTX_LEN + ws, A_WIN_KEYS), :]))


def _attn_a(q, k, v_aug, sink_rows):
    n = q.shape[0]
    wq = A_HQ * A_HD
    wkv = A_HKV * A_HD
    return pl.pallas_call(
        _attn_a_kernel,
        grid=(n // TQ,),
        in_specs=[
            pl.BlockSpec((TQ, wq), lambda i: (i, 0)),
            pl.BlockSpec((n, wkv), lambda i: (0, 0)),
            pl.BlockSpec((n, 2 * wkv), lambda i: (0, 0)),
            pl.BlockSpec((A_HQ, LANES), lambda i: (0, 0)),
        ],
        out_specs=pl.BlockSpec((TQ, wq), lambda i: (i, 0)),
        out_shape=jax.ShapeDtypeStruct((n, wq), BF16),
        scratch_shapes=[pltpu.VMEM((A_HQ * TQ, LANES), BF16)],
        compiler_params=_cparams("arbitrary"),
        name="attn_a",
    )(q, k, v_aug, sink_rows)


def _layer_norm(z, g, b):
    mu = jnp.mean(z, axis=-1, keepdims=True)
    zc = z - mu
    var = jnp.mean(zc * zc, axis=-1, keepdims=True)
    return zc * lax.rsqrt(var + LN_EPS) * g + b


def _route(logits_t, bias_t):
    t = logits_t.shape[1]
    score = [jax.nn.sigmoid(logits_t[e:e + 1, :]) for e in range(N_EXPERTS)]
    sel = [score[e] + bias_t[e:e + 1, :] for e in range(N_EXPERTS)]
    best_g = jnp.zeros((1, t), jnp.int32)
    best_s = None
    for g in range(N_GROUPS):
        a = sel[g * EXPERTS_PER_GROUP:(g + 1) * EXPERTS_PER_GROUP]
        gs = None
        for x in range(EXPERTS_PER_GROUP):
            for y in range(x + 1, EXPERTS_PER_GROUP):
                pair = a[x] + a[y]
                gs = pair if gs is None else jnp.maximum(gs, pair)
        if best_s is None:
            best_s = gs
        else:
            upd = gs > best_s
            best_g = jnp.where(upd, g, best_g)
            best_s = jnp.where(upd, gs, best_s)
    neg = jnp.full((1, t), -jnp.inf, F32)
    masked = [jnp.where(best_g == (e // EXPERTS_PER_GROUP), sel[e], neg) for e in range(N_EXPERTS)]
    i1 = jnp.zeros((1, t), jnp.int32)
    v1 = masked[0]
    for e in range(1, N_EXPERTS):
        upd = masked[e] > v1
        i1 = jnp.where(upd, e, i1)
        v1 = jnp.where(upd, masked[e], v1)
    i2 = jnp.full((1, t), -1, jnp.int32)
    v2 = neg
    for e in range(N_EXPERTS):
        upd = (masked[e] > v2) & (i1 != e)
        i2 = jnp.where(upd, e, i2)
        v2 = jnp.where(upd, masked[e], v2)
    zero = jnp.zeros((1, t), F32)
    w1 = zero
    w2 = zero
    for e in range(N_EXPERTS):
        w1 = w1 + jnp.where(i1 == e, score[e], zero)
        w2 = w2 + jnp.where(i2 == e, score[e], zero)
    den = w1 + w2
    return i1, i2, w1 / den, w2 / den


def _post_kernel(o1_ref, o2_ref, w1_ref, w2_ref, x_ref, mod_ref, lng_ref, lnb_ref, wr_ref, br_ref,
                 x1_ref, h2_ref, route_ref, wcol_ref, cnt_out_ref, cnt_ref):
    @pl.when(pl.program_id(0) == 0)
    def _():
        cnt_ref[...] = jnp.zeros(cnt_ref.shape, F32)

    g1 = mod_ref[0, 2:3, :]
    sh2 = mod_ref[0, 3:4, :]
    sc2 = mod_ref[0, 4:5, :]
    y = _dot(o1_ref[...], w1_ref[...]) + _dot(o2_ref[...], w2_ref[...])
    x1 = _layer_norm(ALPHA * x_ref[...] + g1 * y, lng_ref[...], lnb_ref[...])
    x1_ref[...] = x1
    h2 = x1 * (1.0 + sc2) + sh2
    h2_ref[...] = h2
    h_hi = h2.astype(BF16)
    h_lo = (h2 - h_hi.astype(F32)).astype(BF16)
    wr = wr_ref[...]
    w_hi = wr.astype(BF16)
    w_lo = (wr - w_hi.astype(F32)).astype(BF16)
    logits = _dot(h_hi, w_hi) + (_dot(h_hi, w_lo) + _dot(h_lo, w_hi))
    i1, i2, w1, w2 = _route(logits.T, br_ref[...])

    row = lax.broadcasted_iota(jnp.int32, (LANES, TQ), 0)
    hit1 = row == i1
    hit2 = row == i2
    assigned = jnp.where(hit1 | hit2, 1.0, 0.0)
    t_from = lax.broadcasted_iota(jnp.int32, (TQ, TQ), 0)
    t_to = lax.broadcasted_iota(jnp.int32, (TQ, TQ), 1)
    earlier = jnp.where(t_from < t_to, 1.0, 0.0).astype(BF16)
    rank = cnt_ref[:, 0:1] + _dot(assigned.astype(BF16), earlier)
    r1 = jnp.sum(jnp.where(hit1, rank, 0.0), axis=0, keepdims=True).astype(jnp.int32)
    r2 = jnp.sum(jnp.where(hit2, rank, 0.0), axis=0, keepdims=True).astype(jnp.int32)
    cnt_ref[...] = cnt_ref[...] + jnp.sum(assigned, axis=1, keepdims=True)
    cnt_out_ref[...] = cnt_ref[...]

    row8 = lax.broadcasted_iota(jnp.int32, (8, TQ), 0)
    route_ref[0] = jnp.where(row8 == 0, i1, jnp.where(row8 == 1, i2, jnp.where(row8 == 2, r1,
                             jnp.where(row8 == 3, r2, 0))))
    wcol_ref[...] = jnp.where(row == 0, w1, jnp.where(row == 1, w2, 0.0)).T


def _post(o1, o2, o1_col, o2_col, w1, w2, x, mods, layer, ln_g, ln_b, wr, br):
    n, d = x.shape
    half = w1.shape[0]
    row = lambda i: (i, 0)
    fixed = lambda i: (0, 0)
    return pl.pallas_call(
        _post_kernel,
        grid=(n // TQ,),
        in_specs=[
            pl.BlockSpec((TQ, half), lambda i: (i, o1_col)),
            pl.BlockSpec((TQ, half), lambda i: (i, o2_col)),
            pl.BlockSpec((half, d), fixed),
            pl.BlockSpec((half, d), fixed),
            pl.BlockSpec((TQ, d), row),
            _mod_spec(layer),
            pl.BlockSpec((1, d), fixed),
            pl.BlockSpec((1, d), fixed),
            pl.BlockSpec((d, LANES), fixed),
            pl.BlockSpec((LANES, 1), fixed),
        ],
        out_specs=[
            pl.BlockSpec((TQ, d), row),
            pl.BlockSpec((TQ, d), row),
            pl.BlockSpec((1, 8, TQ), lambda i: (i, 0, 0)),
            pl.BlockSpec((TQ, LANES), row),
            pl.BlockSpec((LANES, LANES), fixed),
        ],
        out_shape=[
            jax.ShapeDtypeStruct((n, d), F32),
            jax.ShapeDtypeStruct((n, d), F32),
            jax.ShapeDtypeStruct((n // TQ, 8, TQ), jnp.int32),
            jax.ShapeDtypeStruct((n, LANES), F32),
            jax.ShapeDtypeStruct((LANES, LANES), F32),
        ],
        scratch_shapes=[pltpu.VMEM((LANES, LANES), F32)],
        compiler_params=_cparams("arbitrary"),
        name="post_attn",
    )(o1, o2, w1, w2, x, mods, ln_g, ln_b, wr, br)


def _moe_layout(route, counts, n):
    n_tiles_max = (2 * n) // MOE_TM + N_EXPERTS
    cnt = counts[:N_EXPERTS, 0].astype(jnp.int32)
    tiles = (cnt + MOE_TM - 1) // MOE_TM
    tiles_end = jnp.cumsum(tiles)
    seg_start = (tiles_end - tiles) * MOE_TM
    experts = jnp.arange(N_EXPERTS)

    def start_of(e):
        return jnp.sum(jnp.where(e[..., None] == experts, seg_start, 0), axis=-1)

    pos = jnp.concatenate([start_of(route[:, 0]) + route[:, 2],
                           start_of(route[:, 1]) + route[:, 3]], axis=1)[:, None, :]
    n_used = tiles_end[-1]
    j = jnp.minimum(jnp.arange(n_tiles_max), n_used - 1)
    tile_expert = jnp.sum(j[:, None] >= tiles_end[None, :], axis=1).astype(jnp.int32)
    last_tiles = jnp.maximum(tiles_end - 1, 0)
    tail_tiles = jnp.minimum(n_used + experts, n_tiles_max - 1)
    live = jnp.concatenate([tiles > 0, n_used + experts < n_tiles_max])
    zero_tiles = jnp.concatenate([last_tiles, tail_tiles, live]).astype(jnp.int32)
    return pos.astype(jnp.int32), tile_expert, n_used.reshape(1).astype(jnp.int32), zero_tiles, n_tiles_max


def _for_rows(fn):
    sublanes = 8

    def group(t, carry):
        base = pl.multiple_of(t * sublanes, sublanes)
        for u in range(sublanes):
            fn(base + u)
        return carry

    lax.fori_loop(0, TQ // sublanes, group, 0)


def _scatter_kernel(zt_ref, pos_ref, h_ref, xs_ref, zero_ref, sem, zsem):
    def zero_copy(t):
        return pltpu.make_async_copy(zero_ref, xs_ref.at[pl.ds(zt_ref[t] * MOE_TM, MOE_TM)], zsem)

    @pl.when(pl.program_id(0) == 0)
    def _():
        zero_ref[...] = jnp.zeros(zero_ref.shape, F32)
        for t in range(2 * N_EXPERTS):
            @pl.when(zt_ref[2 * N_EXPERTS + t] != 0)
            def _():
                zero_copy(t).start()
        for t in range(2 * N_EXPERTS):
            @pl.when(zt_ref[2 * N_EXPERTS + t] != 0)
            def _():
                zero_copy(t).wait()

    def row_copy(r, k):
        return pltpu.make_async_copy(h_ref.at[pl.ds(r, 1)], xs_ref.at[pl.ds(pos_ref[0, 0, k * TQ + r], 1)], sem)

    _for_rows(lambda r: (row_copy(r, 0).start(), row_copy(r, 1).start()))
    _for_rows(lambda r: (row_copy(r, 0).wait(), row_copy(r, 1).wait()))


def _scatter(zero_tiles, pos, h2, n_rows):
    n, d = h2.shape
    return pl.pallas_call(
        _scatter_kernel,
        grid_spec=pltpu.PrefetchScalarGridSpec(
            num_scalar_prefetch=1,
            grid=(n // TQ,),
            in_specs=[
                pl.BlockSpec((1, 1, 2 * TQ), lambda i, zt: (i, 0, 0), memory_space=pltpu.SMEM),
                pl.BlockSpec((TQ, d), lambda i, zt: (i, 0)),
            ],
            out_specs=pl.BlockSpec(memory_space=pl.ANY),
            scratch_shapes=[pltpu.VMEM((MOE_TM, d), F32), pltpu.SemaphoreType.DMA(()),
                            pltpu.SemaphoreType.DMA(())],
        ),
        out_shape=jax.ShapeDtypeStruct((n_rows, d), F32),
        compiler_params=_cparams("arbitrary"),
        name="moe_scatter",
    )(zero_tiles, pos, h2)


def _expert_kernel(te_ref, nt_ref, x_ref, wg_ref, wu_ref, wd_ref, y_ref, wgb_ref, wub_ref, wdb_ref):
    j = pl.program_id(0)
    used = j < nt_ref[0]
    new_expert = jnp.logical_or(j == 0, te_ref[j] != te_ref[jnp.maximum(j - 1, 0)])

    @pl.when(jnp.logical_and(used, new_expert))
    def _():
        wgb_ref[...] = wg_ref[0].astype(BF16)
        wub_ref[...] = wu_ref[0].astype(BF16)
        wdb_ref[...] = wd_ref[0].astype(BF16)

    @pl.when(used)
    def _():
        x = x_ref[...].astype(BF16)
        g = _dot(x, wgb_ref[...])
        u = _dot(x, wub_ref[...])
        a = (g * jax.nn.sigmoid(g)) * u
        y_ref[...] = _dot(a.astype(BF16), wdb_ref[...])

    @pl.when(jnp.logical_not(used))
    def _():
        y_ref[...] = jnp.zeros(y_ref.shape, F32)


def _experts(tile_expert, n_used, xs, wg, wu, wd, layer, n_tiles_max):
    _, d = xs.shape
    f = wg.shape[3]
    weight = lambda j, te, nt: (layer, te[j], 0, 0)
    return pl.pallas_call(
        _expert_kernel,
        grid_spec=pltpu.PrefetchScalarGridSpec(
            num_scalar_prefetch=2,
            grid=(n_tiles_max,),
            in_specs=[
                pl.BlockSpec((MOE_TM, d), lambda j, te, nt: (jnp.minimum(j, nt[0] - 1), 0)),
                pl.BlockSpec((None, 1, d, f), weight),
                pl.BlockSpec((None, 1, d, f), weight),
                pl.BlockSpec((None, 1, f, d), weight),
            ],
            out_specs=pl.BlockSpec((MOE_TM, d), lambda j, te, nt: (j, 0)),
            scratch_shapes=[pltpu.VMEM((d, f), BF16), pltpu.VMEM((d, f), BF16), pltpu.VMEM((f, d), BF16)],
        ),
        out_shape=jax.ShapeDtypeStruct(xs.shape, F32),
        compiler_params=_cparams("arbitrary"),
        name="moe_experts",
    )(tile_expert, n_used, xs, wg, wu, wd)


def _combine_kernel(pos_ref, pos_next_ref, ys_ref, x_ref, wcol_ref, mod_ref, lng_ref, lnb_ref, o_ref,
                    buf_ref, sem):
    i = pl.program_id(0)
    n_steps = pl.num_programs(0)
    slot = lax.rem(i, 2)

    def row_copy(p_ref, dst_slot, r, k):
        return pltpu.make_async_copy(ys_ref.at[pl.ds(p_ref[0, 0, k * TQ + r], 1)],
                                     buf_ref.at[dst_slot, k, pl.ds(r, 1)], sem.at[dst_slot])

    def gather(p_ref, dst_slot):
        _for_rows(lambda r: (row_copy(p_ref, dst_slot, r, 0).start(), row_copy(p_ref, dst_slot, r, 1).start()))

    @pl.when(i == 0)
    def _():
        gather(pos_ref, 0)

    @pl.when(i + 1 < n_steps)
    def _():
        gather(pos_next_ref, 1 - slot)

    _for_rows(lambda r: (row_copy(pos_ref, slot, r, 0).wait(), row_copy(pos_ref, slot, r, 1).wait()))

    w = wcol_ref[...]
    moe = w[:, 0:1] * buf_ref[slot, 0] + w[:, 1:2] * buf_ref[slot, 1]
    g2 = mod_ref[0, 5:6, :]
    o_ref[...] = _layer_norm(ALPHA * x_ref[...] + g2 * moe, lng_ref[...], lnb_ref[...])


def _combine(pos, ys, x1, wcol, mods, layer, ln_g, ln_b, latents_only):
    n, d = x1.shape
    n_steps = n // TQ
    row = lambda i: (i, 0)
    fixed = lambda i: (0, 0)
    if latents_only:
        out_rows, out_map = n - CTX_LEN, lambda i: (jnp.maximum(i - CTX_LEN // TQ, 0), 0)
    else:
        out_rows, out_map = n, row
    return pl.pallas_call(
        _combine_kernel,
        grid=(n_steps,),
        in_specs=[
            pl.BlockSpec((1, 1, 2 * TQ), lambda i: (i, 0, 0), memory_space=pltpu.SMEM),
            pl.BlockSpec((1, 1, 2 * TQ), lambda i: (jnp.minimum(i + 1, n_steps - 1), 0, 0),
                         memory_space=pltpu.SMEM),
            pl.BlockSpec(memory_space=pl.ANY),
            pl.BlockSpec((TQ, d), row),
            pl.BlockSpec((TQ, LANES), row),
            _mod_spec(layer),
            pl.BlockSpec((1, d), fixed),
            pl.BlockSpec((1, d), fixed),
        ],
        out_specs=pl.BlockSpec((TQ, d), out_map),
        out_shape=jax.ShapeDtypeStruct((out_rows, d), F32),
        scratch_shapes=[pltpu.VMEM((2, 2, TQ, d), F32), pltpu.SemaphoreType.DMA((2,))],
        compiler_params=_cparams("arbitrary"),
        name="moe_combine",
    )(pos, pos, ys, x1, wcol, mods, ln_g, ln_b)


def _moe(h2, route, wcol, counts, wg, wu, wd, x1, mods, layer, ln_g, ln_b, latents_only):
    n = h2.shape[0]
    pos, tile_expert, n_used, zero_tiles, n_tiles_max = _moe_layout(route, counts, n)
    xs = _scatter(zero_tiles, pos, h2, n_tiles_max * MOE_TM)
    ys = _experts(tile_expert, n_used, xs, wg, wu, wd, layer, n_tiles_max)
    return _combine(pos, ys, x1, wcol, mods, layer, ln_g, ln_b, latents_only)


def _rope_tables(n_lat, head_dim):
    quarter = head_dim // 4
    n_rows = n_lat // GRID_W
    inv_freq = ROPE_THETA ** (-jnp.arange(quarter, dtype=F32) / quarter)
    ang_r = jnp.arange(n_rows)[:, None].astype(F32) * inv_freq
    ang_c = jnp.arange(GRID_W)[:, None].astype(F32) * inv_freq
    by_row = lambda a: jnp.repeat(a, GRID_W, axis=0)
    by_col = lambda a: jnp.tile(a, (n_rows, 1))
    cos_r, sin_r = by_row(jnp.cos(ang_r)), by_row(jnp.sin(ang_r))
    cos_c, sin_c = by_col(jnp.cos(ang_c)), by_col(jnp.sin(ang_c))
    cos = jnp.concatenate([cos_r, cos_r, cos_c, cos_c], axis=1)
    sin = jnp.concatenate([-sin_r, sin_r, -sin_c, sin_c], axis=1)
    reps = LANES // head_dim
    cos = jnp.tile(cos, (1, reps))
    sin = jnp.tile(sin, (1, reps))
    cos = jnp.concatenate([jnp.ones((CTX_LEN, LANES), F32), cos], axis=0)
    sin = jnp.concatenate([jnp.zeros((CTX_LEN, LANES), F32), sin], axis=0)
    return cos, sin


def _a_head_order():
    grp = A_HQ // A_HKV
    order = []
    for j in range(grp):
        for g in range(A_HKV):
            order.append(g * grp + j)
    return order


def kernel(x, c, ctx, c_ctx, w_mod, b_mod, ln_g, ln_b, w_in_even, w_out_even, sink_logits, lam_q1, lam_k1,
           lam_q2, lam_k2, subln_g, w_in_odd, w_out_odd, q_norm_g, k_norm_g, w_router, b_router, w_gate,
           w_up, w_down):
    n_lat = x.shape[1]
    assert x.shape == (1, n_lat, D_MODEL) and ctx.shape == (1, CTX_LEN, D_MODEL)
    assert n_lat % max(UNROLL * TK, MOE_TM) == 0 and n_lat % GRID_W == 0
    xs = jnp.concatenate([ctx[0], x[0]], axis=0)

    c8 = jnp.zeros((8, D_MODEL), F32).at[0].set(c_ctx).at[1].set(c[0])
    mods = _modulation(c8, w_mod, b_mod)

    cos64, sin64 = _rope_tables(n_lat, A_HD)
    cos128, sin128 = _rope_tables(n_lat, C_HD)

    order = _a_head_order()
    wq_a = A_HQ * A_HD
    head = lambda h: slice(h * A_HD, (h + 1) * A_HD)
    w_in_even_p = jnp.concatenate([w_in_even[:, :, head(h)] for h in order] + [w_in_even[:, :, wq_a:]],
                                  axis=2).astype(BF16)
    w_out_even_p = jnp.concatenate([w_out_even[:, head(h)] for h in order] + [w_out_even[:, wq_a:]],
                                   axis=1).astype(BF16)
    w_in_odd_b = w_in_odd.astype(BF16)
    w_out_odd_b = w_out_odd.astype(BF16)
    wg, wu, wd = w_gate, w_up, w_down
    wr = jnp.pad(w_router, ((0, 0), (0, LANES - N_EXPERTS)))
    br = jnp.pad(b_router.astype(F32), (0, LANES - N_EXPERTS)).reshape(LANES, 1)

    segs_even = ((A_HQ * A_HD, True, None, (A_HD ** -0.5) * LOG2E, None), (A_HKV * A_HD, True, None, None, None),
                 (A_HKV * A_HD, False, None, None, "ones"),
                 (B_H * 2 * B_HD, True, None, (B_HD ** -0.5) * LOG2E, None),
                 (B_H * 2 * B_HD, True, None, None, None), (B_H * B_VD, False, None, None, "transposed"))
    segs_odd = ((C_HQ * C_HD, True, 0, (C_HD ** -0.5) * LOG2E, None), (C_HKV * C_HD, True, 1, None, None),
                (C_HKV * C_HD, False, None, None, "transposed"))

    for l in range(DEPTH):
        i = l // 2
        lng = ln_g[l]
        lnb = ln_b[l]
        if l % 2 == 0:
            lam_init = 0.8 - 0.6 * math.exp(-0.3 * l)
            aq, ak, av, bq, bk, bv = _inproj(xs, mods, l, w_in_even_p[i], cos64, sin64, segs_even, A_HD // 4)
            sink_rows = jnp.broadcast_to(
                (sink_logits[i][jnp.array(order)] * LOG2E)[:, None], (A_HQ, LANES)).astype(F32)
            oa = _attn_a(aq, ak, av, sink_rows)
            ob = _attn_b(bq, bk, bv, lam_q1[i][None], lam_k1[i][None], lam_q2[i][None], lam_k2[i][None],
                         subln_g[i][None], lam_init)
            half = wq_a
            routed = _post(oa, ob, 0, 0, w_out_even_p[i][:half], w_out_even_p[i][half:], xs, mods, l,
                           lng[0][None], lnb[0][None], wr, br)
        else:
            q, k, v = _inproj(xs, mods, l, w_in_odd_b[i], cos128, sin128, segs_odd, C_HD // 4,
                              gains=(q_norm_g[i][None], k_norm_g[i][None]))
            o = _attn_c(q, k, v)
            half = C_HQ * C_HD // 2
            routed = _post(o, o, 0, 1, w_out_odd_b[i][:half], w_out_odd_b[i][half:], xs, mods, l,
                           lng[0][None], lnb[0][None], wr, br)
        x1, h2, route, wcol, counts = routed
        xs = _moe(h2, route, wcol, counts, wg, wu, wd, x1, mods, l, lng[1][None], lnb[1][None],
                  latents_only=(l == DEPTH - 1))
    return xs[None]
```

```python
import functools
import math

import jax
import jax.numpy as jnp
from jax import lax
from jax.experimental import pallas as pl
from jax.experimental.pallas import tpu as pltpu

D_MODEL = 1024
SEQ = 16384
DEPTH = 4
GRID_W = 64
CTX_LEN = 256
WINDOW = 128
ROPE_THETA = 10000.0
A_HQ, A_HKV, A_HD = 8, 2, 64
B_H, B_HD = 4, 64
B_VD = 2 * B_HD
C_HQ, C_HKV, C_HD = 8, 2, 128
N_EXPERTS = 16
N_GROUPS = 4
EXPERTS_PER_GROUP = N_EXPERTS // N_GROUPS
D_EXPERT = 512
ALPHA = (2 * DEPTH) ** 0.25
LN_EPS = 1e-5
QK_EPS = 1e-6
SUBLN_EPS = 1e-5

LANES = 128
TQ = CTX_LEN
TK = 512
UNROLL = 8
MOE_TM = 512
VMEM_LIMIT = 56 * 1024 * 1024
LOG2E = 1.4426950408889634
BF16 = jnp.bfloat16
F32 = jnp.float32


def _cparams(*sem):
    return pltpu.CompilerParams(dimension_semantics=sem, vmem_limit_bytes=VMEM_LIMIT)


def _dot(a, b):
    return jnp.dot(a, b, preferred_element_type=F32)


def _dot_nt(a, b):
    return lax.dot_general(a, b, (((1,), (1,)), ((), ())), preferred_element_type=F32)


def _lane_tile(x, width):
    reps = width // x.shape[-1]
    return x if reps == 1 else jnp.concatenate([x] * reps, axis=-1)


def _mod_kernel(c_ref, w_ref, b_ref, o_ref):
    c = c_ref[...]
    s = c * jax.nn.sigmoid(c)
    o_ref[0] = _dot(s.astype(BF16), w_ref[0].astype(BF16)) + b_ref[0]


def _modulation(c8, w_mod, b_mod):
    depth, d, n6 = w_mod.shape
    tn = 1536
    out = pl.pallas_call(
        _mod_kernel,
        grid=(depth, n6 // tn),
        in_specs=[
            pl.BlockSpec((8, d), lambda l, j: (0, 0)),
            pl.BlockSpec((1, d, tn), lambda l, j: (l, 0, j)),
            pl.BlockSpec((1, 1, tn), lambda l, j: (l, 0, j)),
        ],
        out_specs=pl.BlockSpec((1, 8, tn), lambda l, j: (l, 0, j)),
        out_shape=jax.ShapeDtypeStruct((depth, 8, n6), F32),
        compiler_params=_cparams("arbitrary", "arbitrary"),
        name="modulation",
    )(c8, w_mod, b_mod.reshape(depth, 1, n6))
    m = out[:, :2].reshape(depth, 2, 6, d)
    m = jnp.pad(m, ((0, 0), (0, 0), (0, 2), (0, 0)))
    return m.reshape(depth * 2, 8, d)


def _mod_spec(layer):
    return pl.BlockSpec((1, 8, D_MODEL), lambda i, *_: (2 * layer + jnp.minimum(i, 1), 0, 0))


def _rope_chunk(z, cos, sin, quarter):
    lane = lax.broadcasted_iota(jnp.int32, z.shape, 1)
    first = (lane % (2 * quarter)) < quarter
    nxt = pltpu.roll(z, LANES - quarter, axis=1)
    prv = pltpu.roll(z, quarter, axis=1)
    return z * cos + jnp.where(first, nxt, prv) * sin


def _inproj_kernel(segs, quarter, n_gain, x_ref, mod_ref, w_ref, cos_ref, sin_ref, *rest):
    gains = rest[:n_gain]
    outs = rest[n_gain:]
    sh = mod_ref[0, 0:1, :]
    sc = mod_ref[0, 1:2, :]
    hb = (x_ref[...] * (1.0 + sc) + sh).astype(BF16)
    cos = cos_ref[...]
    sin = sin_ref[...]
    off = 0
    for (width, rope, gain_idx, scale, form), o_ref in zip(segs, outs):
        z = _dot(hb, w_ref[:, off:off + width])
        for c in range(width // LANES):
            zc = z[:, c * LANES:(c + 1) * LANES]
            if gain_idx is not None:
                ms = jnp.mean(zc * zc, axis=-1, keepdims=True)
                zc = zc * lax.rsqrt(ms + QK_EPS) * gains[gain_idx][...]
            if rope:
                zc = _rope_chunk(zc, cos, sin, quarter)
            if scale is not None:
                zc = zc * scale
            if form == "ones":
                o_ref[:, 2 * c * LANES:(2 * c + 1) * LANES] = zc.astype(BF16)
                o_ref[:, (2 * c + 1) * LANES:(2 * c + 2) * LANES] = jnp.ones(zc.shape, BF16)
            elif form == "transposed":
                o_ref[c, 0] = zc.T.astype(BF16)
            else:
                o_ref[:, c * LANES:(c + 1) * LANES] = zc.astype(BF16)
        off += width


def _inproj(x, mods, layer, w, cos, sin, segs, quarter, gains=()):
    n, d = x.shape
    wtot = w.shape[1]
    row = lambda i: (i, 0)
    fixed = lambda i: (0, 0)
    in_specs = [
        pl.BlockSpec((TQ, d), row),
        _mod_spec(layer),
        pl.BlockSpec((d, wtot), fixed),
        pl.BlockSpec((TQ, LANES), row),
        pl.BlockSpec((TQ, LANES), row),
    ] + [pl.BlockSpec((1, LANES), fixed) for _ in gains]
    out_specs, out_shape = [], []
    for width, _, _, _, form in segs:
        if form == "transposed":
            heads = width // LANES
            out_specs.append(pl.BlockSpec((heads, 1, LANES, TQ), lambda i: (0, i, 0, 0)))
            out_shape.append(jax.ShapeDtypeStruct((heads, n // TQ, LANES, TQ), BF16))
        else:
            w_ = width * (2 if form == "ones" else 1)
            out_specs.append(pl.BlockSpec((TQ, w_), row))
            out_shape.append(jax.ShapeDtypeStruct((n, w_), BF16))
    return pl.pallas_call(
        functools.partial(_inproj_kernel, segs, quarter, len(gains)),
        grid=(n // TQ,),
        in_specs=in_specs,
        out_specs=out_specs,
        out_shape=out_shape,
        compiler_params=_cparams("arbitrary"),
        name="inproj",
    )(x, mods, w, cos, sin, *gains)


def _flash(qs_ref, k_ref, vt_ref, m_ref, l_ref, acc_ref, s_refs, is_ctx_block):
    n_chunks = (k_ref.shape[0] - CTX_LEN) // TK
    n_sets = k_ref.shape[1] // LANES
    cols = qs_ref.shape[0] // n_sets
    m_ref[...] = jnp.full(m_ref.shape, -jnp.inf, F32)
    l_ref[...] = jnp.zeros(l_ref.shape, F32)
    acc_ref[...] = jnp.zeros(acc_ref.shape, F32)

    def scores(slot, start, size):
        for g in range(n_sets):
            cs = slice(g * cols, (g + 1) * cols)
            s_refs[slot][0:size, cs] = _dot_nt(k_ref[pl.ds(start, size), g * LANES:(g + 1) * LANES], qs_ref[cs, :])

    def update(slot, first_block, n_blocks):
        s = s_refs[slot][0:n_blocks * TQ, :]
        m_prev = m_ref[0:1, :]
        m_new = jnp.maximum(m_prev, jnp.max(s, axis=0, keepdims=True))
        p = jnp.exp2(s - m_new)
        alpha = jnp.exp2(m_prev - m_new)
        l_ref[...] = jnp.broadcast_to(alpha * l_ref[0:1, :] + jnp.sum(p, axis=0, keepdims=True), l_ref.shape)
        pb = p.astype(BF16)
        for g in range(n_sets):
            cs = slice(g * cols, (g + 1) * cols)
            pv = _dot(vt_ref[g, first_block], pb[0:TQ, cs])
            for b in range(1, n_blocks):
                pv = pv + _dot(vt_ref[g, first_block + b], pb[b * TQ:(b + 1) * TQ, cs])
            acc_ref[:, cs] = acc_ref[:, cs] * alpha[:, cs] + pv
        m_ref[...] = jnp.broadcast_to(m_new, m_ref.shape)

    def lat(j):
        return pl.multiple_of(CTX_LEN + j * TK, CTX_LEN)

    blocks_per_chunk = TK // TQ
    ctx_blocks = CTX_LEN // TQ
    scores(1, 0, CTX_LEN)
    update(1, 0, ctx_blocks)

    @pl.when(jnp.logical_not(is_ctx_block))
    def _():
        scores(0, lat(0), TK)

        def trip(jj, carry):
            j = UNROLL * jj
            for u in range(UNROLL):
                scores((u + 1) % 2, lat(jnp.minimum(j + u + 1, n_chunks - 1)), TK)
                update(u % 2, ctx_blocks + (j + u) * blocks_per_chunk, blocks_per_chunk)
            return carry

        lax.fori_loop(0, n_chunks // UNROLL, trip, 0)


def _flash_out(l_ref, acc_ref, idx):
    cs = slice(idx * TQ, (idx + 1) * TQ)
    return (acc_ref[:, cs] / l_ref[0:1, cs]).T


def _flash_scratch(n_stack):
    rows = n_stack * TQ
    return [
        pltpu.VMEM((rows, LANES), BF16),
        pltpu.VMEM((8, rows), F32),
        pltpu.VMEM((8, rows), F32),
        pltpu.VMEM((LANES, rows), F32),
        pltpu.VMEM((TK, rows), F32),
        pltpu.VMEM((TK, rows), F32),
    ]


def _attn_c_kernel(q_ref, k_ref, vt_ref, o_ref, qs_ref, m_ref, l_ref, acc_ref, s0_ref, s1_ref):
    grp = C_HQ // C_HKV
    for h in range(grp):
        qs_ref[h * TQ:(h + 1) * TQ, :] = q_ref[:, h * C_HD:(h + 1) * C_HD]
    _flash(qs_ref, k_ref, vt_ref, m_ref, l_ref, acc_ref, (s0_ref, s1_ref), pl.program_id(1) == 0)
    for h in range(grp):
        o_ref[:, h * C_HD:(h + 1) * C_HD] = _flash_out(l_ref, acc_ref, h).astype(BF16)


def _attn_c(q, k, vt):
    n = q.shape[0]
    grp = C_HQ // C_HKV
    wq = grp * C_HD
    return pl.pallas_call(
        _attn_c_kernel,
        grid=(C_HKV, n // TQ),
        in_specs=[
            pl.BlockSpec((TQ, wq), lambda g, i: (i, g)),
            pl.BlockSpec((n, C_HD), lambda g, i: (0, g), pipeline_mode=pl.Buffered(1)),
            pl.BlockSpec((1, n // TQ, C_HD, TQ), lambda g, i: (g, 0, 0, 0), pipeline_mode=pl.Buffered(1)),
        ],
        out_specs=pl.BlockSpec((TQ, wq), lambda g, i: (i, g)),
        out_shape=jax.ShapeDtypeStruct((n, C_HQ * C_HD), BF16),
        scratch_shapes=_flash_scratch(grp),
        compiler_params=_cparams("arbitrary", "arbitrary"),
        name="attn_c",
    )(q, k, vt)


B_HEADS_PER_STEP = 2


def _attn_b_kernel(lam_init, q_ref, k_ref, vt_ref, lq1_ref, lk1_ref, lq2_ref, lk2_ref, g_ref, o_ref,
                   qs_ref, m_ref, l_ref, acc_ref, s0_ref, s1_ref):
    for h in range(B_HEADS_PER_STEP):
        q = q_ref[:, h * B_VD:(h + 1) * B_VD]
        lane = lax.broadcasted_iota(jnp.int32, q.shape, 1)
        zero = jnp.zeros_like(q)
        qs_ref[(2 * h) * TQ:(2 * h + 1) * TQ, :] = jnp.where(lane < B_HD, q, zero)
        qs_ref[(2 * h + 1) * TQ:(2 * h + 2) * TQ, :] = jnp.where(lane >= B_HD, q, zero)
    _flash(qs_ref, k_ref, vt_ref, m_ref, l_ref, acc_ref, (s0_ref, s1_ref), pl.program_id(1) == 0)
    lam = (jnp.exp(jnp.sum(lq1_ref[...] * lk1_ref[...], axis=-1, keepdims=True))
           - jnp.exp(jnp.sum(lq2_ref[...] * lk2_ref[...], axis=-1, keepdims=True))) + lam_init
    for h in range(B_HEADS_PER_STEP):
        o = _flash_out(l_ref, acc_ref, 2 * h) - lam * _flash_out(l_ref, acc_ref, 2 * h + 1)
        ms = jnp.mean(o * o, axis=-1, keepdims=True)
        o = o * lax.rsqrt(ms + SUBLN_EPS) * g_ref[...] * (1.0 - lam_init)
        o_ref[:, h * B_VD:(h + 1) * B_VD] = o.astype(BF16)


def _attn_b(q, k, vt, lq1, lk1, lq2, lk2, subln_g, lam_init):
    n = q.shape[0]
    hps = B_HEADS_PER_STEP
    blk = lambda h, i: (i, h)
    col = lambda h, i: (0, h)
    fixed = lambda h, i: (0, 0)
    small = pl.BlockSpec((1, B_HD), fixed)
    return pl.pallas_call(
        functools.partial(_attn_b_kernel, lam_init),
        grid=(B_H // hps, n // TQ),
        in_specs=[
            pl.BlockSpec((TQ, hps * B_VD), blk),
            pl.BlockSpec((n, hps * B_VD), col, pipeline_mode=pl.Buffered(1)),
            pl.BlockSpec((hps, n // TQ, B_VD, TQ), lambda h, i: (h, 0, 0, 0), pipeline_mode=pl.Buffered(1)),
            small, small, small, small,
            pl.BlockSpec((1, B_VD), fixed),
        ],
        out_specs=pl.BlockSpec((TQ, hps * B_VD), blk),
        out_shape=jax.ShapeDtypeStruct((n, B_H * B_VD), BF16),
        scratch_shapes=_flash_scratch(2 * hps),
        compiler_params=_cparams("arbitrary", "arbitrary"),
        name="attn_b",
    )(q, k, vt, lq1, lk1, lq2, lk2, subln_g)


A_WIN_KEYS = TQ + 2 * WINDOW


def _attn_a_kernel(q_ref, k_ref, v_ref, sink_ref, o_ref, qs_ref):
    i = pl.program_id(0)
    n_lat = k_ref.shape[0] - CTX_LEN
    n_blk = A_HQ * A_HD // LANES
    for j in range(n_blk):
        q = q_ref[:, j * LANES:(j + 1) * LANES]
        lane = lax.broadcasted_iota(jnp.int32, q.shape, 1)
        zero = jnp.zeros_like(q)
        qs_ref[(2 * j) * TQ:(2 * j + 1) * TQ, :] = jnp.where(lane < A_HD, q, zero)
        qs_ref[(2 * j + 1) * TQ:(2 * j + 2) * TQ, :] = jnp.where(lane >= A_HD, q, zero)
    sink = jnp.concatenate([jnp.broadcast_to(sink_ref[r:r + 1, :], (TQ, LANES)) for r in range(A_HQ)], axis=0)
    s_c = _dot_nt(qs_ref[...], k_ref[0:CTX_LEN, :])
    vc = v_ref[0:CTX_LEN, :]

    def finish(m, acc):
        o = acc[:, :LANES] / (acc[:, LANES:] + jnp.exp2(sink - m))
        lane = lax.broadcasted_iota(jnp.int32, (TQ, LANES), 1)
        for j in range(n_blk):
            lo = o[(2 * j) * TQ:(2 * j + 1) * TQ, :]
            hi = o[(2 * j + 1) * TQ:(2 * j + 2) * TQ, :]
            o_ref[:, j * LANES:(j + 1) * LANES] = jnp.where(lane < A_HD, lo, hi).astype(BF16)

    @pl.when(i == 0)
    def _():
        m = jnp.maximum(sink, jnp.max(s_c, axis=1, keepdims=True))
        p_c = jnp.exp2(s_c - _lane_tile(m, CTX_LEN)).astype(BF16)
        finish(m, _dot(p_c, vc))

    @pl.when(i > 0)
    def _():
        q0 = (i - 1) * TQ
        ws = jnp.clip(q0 - WINDOW, 0, n_lat - A_WIN_KEYS)
        ws = pl.multiple_of(ws, WINDOW)
        qpos = q0 + lax.broadcasted_iota(jnp.int32, (TQ, A_WIN_KEYS), 0)
        kpos = ws + lax.broadcasted_iota(jnp.int32, (TQ, A_WIN_KEYS), 1)
        ok = jnp.abs(qpos - kpos) <= WINDOW
        s_l = _dot_nt(qs_ref[...], k_ref[pl.ds(CTX_LEN + ws, A_WIN_KEYS), :])
        s_l = jnp.where(ok[None], s_l.reshape(A_HQ, TQ, A_WIN_KEYS), -jnp.inf).reshape(A_HQ * TQ, A_WIN_KEYS)
        m = jnp.maximum(sink, jnp.maximum(jnp.max(s_c, axis=1, keepdims=True),
                                          jnp.max(s_l, axis=1, keepdims=True)))
        p_c = jnp.exp2(s_c - _lane_tile(m, CTX_LEN)).astype(BF16)
        p_l = jnp.exp2(s_l - _lane_tile(m, A_WIN_KEYS)).astype(BF16)
        finish(m, _dot(p_c, vc) + _dot(p_l, v_ref[pl.ds(CTX_LEN + ws, A_WIN_KEYS), :]))


def _attn_a(q, k, v_aug, sink_rows):
    n = q.shape[0]
    wq = A_HQ * A_HD
    wkv = A_HKV * A_HD
    return pl.pallas_call(
        _attn_a_kernel,
        grid=(n // TQ,),
        in_specs=[
            pl.BlockSpec((TQ, wq), lambda i: (i, 0)),
            pl.BlockSpec((n, wkv), lambda i: (0, 0)),
            pl.BlockSpec((n, 2 * wkv), lambda i: (0, 0)),
            pl.BlockSpec((A_HQ, LANES), lambda i: (0, 0)),
        ],
        out_specs=pl.BlockSpec((TQ, wq), lambda i: (i, 0)),
        out_shape=jax.ShapeDtypeStruct((n, wq), BF16),
        scratch_shapes=[pltpu.VMEM((A_HQ * TQ, LANES), BF16)],
        compiler_params=_cparams("arbitrary"),
        name="attn_a",
    )(q, k, v_aug, sink_rows)


def _layer_norm(z, g, b):
    mu = jnp.mean(z, axis=-1, keepdims=True)
    zc = z - mu
    var = jnp.mean(zc * zc, axis=-1, keepdims=True)
    return zc * lax.rsqrt(var + LN_EPS) * g + b


def _route(logits_t, bias_t):
    t = logits_t.shape[1]
    score = [jax.nn.sigmoid(logits_t[e:e + 1, :]) for e in range(N_EXPERTS)]
    sel = [score[e] + bias_t[e:e + 1, :] for e in range(N_EXPERTS)]
    best_g = jnp.zeros((1, t), jnp.int32)
    best_s = None
    for g in range(N_GROUPS):
        a = sel[g * EXPERTS_PER_GROUP:(g + 1) * EXPERTS_PER_GROUP]
        gs = None
        for x in range(EXPERTS_PER_GROUP):
            for y in range(x + 1, EXPERTS_PER_GROUP):
                pair = a[x] + a[y]
                gs = pair if gs is None else jnp.maximum(gs, pair)
        if best_s is None:
            best_s = gs
        else:
            upd = gs > best_s
            best_g = jnp.where(upd, g, best_g)
            best_s = jnp.where(upd, gs, best_s)
    neg = jnp.full((1, t), -jnp.inf, F32)
    masked = [jnp.where(best_g == (e // EXPERTS_PER_GROUP), sel[e], neg) for e in range(N_EXPERTS)]
    i1 = jnp.zeros((1, t), jnp.int32)
    v1 = masked[0]
    for e in range(1, N_EXPERTS):
        upd = masked[e] > v1
        i1 = jnp.where(upd, e, i1)
        v1 = jnp.where(upd, masked[e], v1)
    i2 = jnp.full((1, t), -1, jnp.int32)
    v2 = neg
    for e in range(N_EXPERTS):
        upd = (masked[e] > v2) & (i1 != e)
        i2 = jnp.where(upd, e, i2)
        v2 = jnp.where(upd, masked[e], v2)
    zero = jnp.zeros((1, t), F32)
    w1 = zero
    w2 = zero
    for e in range(N_EXPERTS):
        w1 = w1 + jnp.where(i1 == e, score[e], zero)
        w2 = w2 + jnp.where(i2 == e, score[e], zero)
    den = w1 + w2
    return i1, i2, w1 / den, w2 / den


def _post_kernel(o1_ref, o2_ref, w1_ref, w2_ref, x_ref, mod_ref, lng_ref, lnb_ref, wr_ref, br_ref,
                 x1_ref, h2_ref, route_ref, wcol_ref, cnt_out_ref, cnt_ref):
    @pl.when(pl.program_id(0) == 0)
    def _():
        cnt_ref[...] = jnp.zeros(cnt_ref.shape, F32)

    g1 = mod_ref[0, 2:3, :]
    sh2 = mod_ref[0, 3:4, :]
    sc2 = mod_ref[0, 4:5, :]
    y = _dot(o1_ref[...], w1_ref[...]) + _dot(o2_ref[...], w2_ref[...])
    x1 = _layer_norm(ALPHA * x_ref[...] + g1 * y, lng_ref[...], lnb_ref[...])
    x1_ref[...] = x1
    h2 = x1 * (1.0 + sc2) + sh2
    h2_ref[...] = h2
    h_hi = h2.astype(BF16)
    h_lo = (h2 - h_hi.astype(F32)).astype(BF16)
    wr = wr_ref[...]
    w_hi = wr.astype(BF16)
    w_lo = (wr - w_hi.astype(F32)).astype(BF16)
    logits = _dot(h_hi, w_hi) + (_dot(h_hi, w_lo) + _dot(h_lo, w_hi))
    i1, i2, w1, w2 = _route(logits.T, br_ref[...])

    row = lax.broadcasted_iota(jnp.int32, (LANES, TQ), 0)
    hit1 = row == i1
    hit2 = row == i2
    assigned = jnp.where(hit1 | hit2, 1.0, 0.0)
    t_from = lax.broadcasted_iota(jnp.int32, (TQ, TQ), 0)
    t_to = lax.broadcasted_iota(jnp.int32, (TQ, TQ), 1)
    earlier = jnp.where(t_from < t_to, 1.0, 0.0).astype(BF16)
    rank = cnt_ref[:, 0:1] + _dot(assigned.astype(BF16), earlier)
    r1 = jnp.sum(jnp.where(hit1, rank, 0.0), axis=0, keepdims=True).astype(jnp.int32)
    r2 = jnp.sum(jnp.where(hit2, rank, 0.0), axis=0, keepdims=True).astype(jnp.int32)
    cnt_ref[...] = cnt_ref[...] + jnp.sum(assigned, axis=1, keepdims=True)
    cnt_out_ref[...] = cnt_ref[...]

    row8 = lax.broadcasted_iota(jnp.int32, (8, TQ), 0)
    route_ref[0] = jnp.where(row8 == 0, i1, jnp.where(row8 == 1, i2, jnp.where(row8 == 2, r1,
                             jnp.where(row8 == 3, r2, 0))))
    wcol_ref[...] = jnp.where(row == 0, w1, jnp.where(row == 1, w2, 0.0)).T


def _post(o1, o2, o1_col, o2_col, w1, w2, x, mods, layer, ln_g, ln_b, wr, br):
    n, d = x.shape
    half = w1.shape[0]
    row = lambda i: (i, 0)
    fixed = lambda i: (0, 0)
    return pl.pallas_call(
        _post_kernel,
        grid=(n // TQ,),
        in_specs=[
            pl.BlockSpec((TQ, half), lambda i: (i, o1_col)),
            pl.BlockSpec((TQ, half), lambda i: (i, o2_col)),
            pl.BlockSpec((half, d), fixed),
            pl.BlockSpec((half, d), fixed),
            pl.BlockSpec((TQ, d), row),
            _mod_spec(layer),
            pl.BlockSpec((1, d), fixed),
            pl.BlockSpec((1, d), fixed),
            pl.BlockSpec((d, LANES), fixed),
            pl.BlockSpec((LANES, 1), fixed),
        ],
        out_specs=[
            pl.BlockSpec((TQ, d), row),
            pl.BlockSpec((TQ, d), row),
            pl.BlockSpec((1, 8, TQ), lambda i: (i, 0, 0)),
            pl.BlockSpec((TQ, LANES), row),
            pl.BlockSpec((LANES, LANES), fixed),
        ],
        out_shape=[
            jax.ShapeDtypeStruct((n, d), F32),
            jax.ShapeDtypeStruct((n, d), F32),
            jax.ShapeDtypeStruct((n // TQ, 8, TQ), jnp.int32),
            jax.ShapeDtypeStruct((n, LANES), F32),
            jax.ShapeDtypeStruct((LANES, LANES), F32),
        ],
        scratch_shapes=[pltpu.VMEM((LANES, LANES), F32)],
        compiler_params=_cparams("arbitrary"),
        name="post_attn",
    )(o1, o2, w1, w2, x, mods, ln_g, ln_b, wr, br)


def _moe_layout(route, counts, n):
    n_tiles_max = (2 * n) // MOE_TM + N_EXPERTS
    cnt = counts[:N_EXPERTS, 0].astype(jnp.int32)
    tiles = (cnt + MOE_TM - 1) // MOE_TM
    tiles_end = jnp.cumsum(tiles)
    seg_start = (tiles_end - tiles) * MOE_TM
    experts = jnp.arange(N_EXPERTS)

    def start_of(e):
        return jnp.sum(jnp.where(e[..., None] == experts, seg_start, 0), axis=-1)

    pos = jnp.concatenate([start_of(route[:, 0]) + route[:, 2],
                           start_of(route[:, 1]) + route[:, 3]], axis=1)[:, None, :]
    n_used = tiles_end[-1]
    j = jnp.minimum(jnp.arange(n_tiles_max), n_used - 1)
    tile_expert = jnp.sum(j[:, None] >= tiles_end[None, :], axis=1).astype(jnp.int32)
    last_tiles = jnp.maximum(tiles_end - 1, 0)
    tail_tiles = jnp.minimum(n_used + experts, n_tiles_max - 1)
    live = jnp.concatenate([tiles > 0, n_used + experts < n_tiles_max])
    zero_tiles = jnp.concatenate([last_tiles, tail_tiles, live]).astype(jnp.int32)
    return pos.astype(jnp.int32), tile_expert, n_used.reshape(1).astype(jnp.int32), zero_tiles, n_tiles_max


def _for_rows(fn):
    sublanes = 8

    def group(t, carry):
        base = pl.multiple_of(t * sublanes, sublanes)
        for u in range(sublanes):
            fn(base + u)
        return carry

    lax.fori_loop(0, TQ // sublanes, group, 0)


def _scatter_kernel(zt_ref, pos_ref, h_ref, xs_ref, zero_ref, sem, zsem):
    def zero_copy(t):
        return pltpu.make_async_copy(zero_ref, xs_ref.at[pl.ds(zt_ref[t] * MOE_TM, MOE_TM)], zsem)

    @pl.when(pl.program_id(0) == 0)
    def _():
        zero_ref[...] = jnp.zeros(zero_ref.shape, F32)
        for t in range(2 * N_EXPERTS):
            @pl.when(zt_ref[2 * N_EXPERTS + t] != 0)
            def _():
                zero_copy(t).start()
        for t in range(2 * N_EXPERTS):
            @pl.when(zt_ref[2 * N_EXPERTS + t] != 0)
            def _():
                zero_copy(t).wait()

    def row_copy(r, k):
        return pltpu.make_async_copy(h_ref.at[pl.ds(r, 1)], xs_ref.at[pl.ds(pos_ref[0, 0, k * TQ + r], 1)], sem)

    _for_rows(lambda r: (row_copy(r, 0).start(), row_copy(r, 1).start()))
    _for_rows(lambda r: (row_copy(r, 0).wait(), row_copy(r, 1).wait()))


def _scatter(zero_tiles, pos, h2, n_rows):
    n, d = h2.shape
    return pl.pallas_call(
        _scatter_kernel,
        grid_spec=pltpu.PrefetchScalarGridSpec(
            num_scalar_prefetch=1,
            grid=(n // TQ,),
            in_specs=[
                pl.BlockSpec((1, 1, 2 * TQ), lambda i, zt: (i, 0, 0), memory_space=pltpu.SMEM),
                pl.BlockSpec((TQ, d), lambda i, zt: (i, 0)),
            ],
            out_specs=pl.BlockSpec(memory_space=pl.ANY),
            scratch_shapes=[pltpu.VMEM((MOE_TM, d), F32), pltpu.SemaphoreType.DMA(()),
                            pltpu.SemaphoreType.DMA(())],
        ),
        out_shape=jax.ShapeDtypeStruct((n_rows, d), F32),
        compiler_params=_cparams("arbitrary"),
        name="moe_scatter",
    )(zero_tiles, pos, h2)


def _expert_kernel(te_ref, nt_ref, x_ref, wg_ref, wu_ref, wd_ref, y_ref, wgb_ref, wub_ref, wdb_ref):
    j = pl.program_id(0)
    used = j < nt_ref[0]
    new_expert = jnp.logical_or(j == 0, te_ref[j] != te_ref[jnp.maximum(j - 1, 0)])

    @pl.when(jnp.logical_and(used, new_expert))
    def _():
        wgb_ref[...] = wg_ref[0].astype(BF16)
        wub_ref[...] = wu_ref[0].astype(BF16)
        wdb_ref[...] = wd_ref[0].astype(BF16)

    @pl.when(used)
    def _():
        x = x_ref[...].astype(BF16)
        g = _dot(x, wgb_ref[...])
        u = _dot(x, wub_ref[...])
        a = (g * jax.nn.sigmoid(g)) * u
        y_ref[...] = _dot(a.astype(BF16), wdb_ref[...])

    @pl.when(jnp.logical_not(used))
    def _():
        y_ref[...] = jnp.zeros(y_ref.shape, F32)


def _experts(tile_expert, n_used, xs, wg, wu, wd, layer, n_tiles_max):
    _, d = xs.shape
    f = wg.shape[3]
    weight = lambda j, te, nt: (layer, te[j], 0, 0)
    return pl.pallas_call(
        _expert_kernel,
        grid_spec=pltpu.PrefetchScalarGridSpec(
            num_scalar_prefetch=2,
            grid=(n_tiles_max,),
            in_specs=[
                pl.BlockSpec((MOE_TM, d), lambda j, te, nt: (jnp.minimum(j, nt[0] - 1), 0)),
                pl.BlockSpec((None, 1, d, f), weight),
                pl.BlockSpec((None, 1, d, f), weight),
                pl.BlockSpec((None, 1, f, d), weight),
            ],
            out_specs=pl.BlockSpec((MOE_TM, d), lambda j, te, nt: (j, 0)),
            scratch_shapes=[pltpu.VMEM((d, f), BF16), pltpu.VMEM((d, f), BF16), pltpu.VMEM((f, d), BF16)],
        ),
        out_shape=jax.ShapeDtypeStruct(xs.shape, F32),
        compiler_params=_cparams("arbitrary"),
        name="moe_experts",
    )(tile_expert, n_used, xs, wg, wu, wd)


def _combine_kernel(pos_ref, pos_next_ref, ys_ref, x_ref, wcol_ref, mod_ref, lng_ref, lnb_ref, o_ref,
                    buf_ref, sem):
    i = pl.program_id(0)
    n_steps = pl.num_programs(0)
    slot = lax.rem(i, 2)

    def row_copy(p_ref, dst_slot, r, k):
        return pltpu.make_async_copy(ys_ref.at[pl.ds(p_ref[0, 0, k * TQ + r], 1)],
                                     buf_ref.at[dst_slot, k, pl.ds(r, 1)], sem.at[dst_slot])

    def gather(p_ref, dst_slot):
        _for_rows(lambda r: (row_copy(p_ref, dst_slot, r, 0).start(), row_copy(p_ref, dst_slot, r, 1).start()))

    @pl.when(i == 0)
    def _():
        gather(pos_ref, 0)

    @pl.when(i + 1 < n_steps)
    def _():
        gather(pos_next_ref, 1 - slot)

    _for_rows(lambda r: (row_copy(pos_ref, slot, r, 0).wait(), row_copy(pos_ref, slot, r, 1).wait()))

    w = wcol_ref[...]
    moe = w[:, 0:1] * buf_ref[slot, 0] + w[:, 1:2] * buf_ref[slot, 1]
    g2 = mod_ref[0, 5:6, :]
    o_ref[...] = _layer_norm(ALPHA * x_ref[...] + g2 * moe, lng_ref[...], lnb_ref[...])


def _combine(pos, ys, x1, wcol, mods, layer, ln_g, ln_b, latents_only):
    n, d = x1.shape
    n_steps = n // TQ
    row = lambda i: (i, 0)
    fixed = lambda i: (0, 0)
    if latents_only:
        out_rows, out_map = n - CTX_LEN, lambda i: (jnp.maximum(i - CTX_LEN // TQ, 0), 0)
    else:
        out_rows, out_map = n, row
    return pl.pallas_call(
        _combine_kernel,
        grid=(n_steps,),
        in_specs=[
            pl.BlockSpec((1, 1, 2 * TQ), lambda i: (i, 0, 0), memory_space=pltpu.SMEM),
            pl.BlockSpec((1, 1, 2 * TQ), lambda i: (jnp.minimum(i + 1, n_steps - 1), 0, 0),
                         memory_space=pltpu.SMEM),
            pl.BlockSpec(memory_space=pl.ANY),
            pl.BlockSpec((TQ, d), row),
            pl.BlockSpec((TQ, LANES), row),
            _mod_spec(layer),
            pl.BlockSpec((1, d), fixed),
            pl.BlockSpec((1, d), fixed),
        ],
        out_specs=pl.BlockSpec((TQ, d), out_map),
        out_shape=jax.ShapeDtypeStruct((out_rows, d), F32),
        scratch_shapes=[pltpu.VMEM((2, 2, TQ, d), F32), pltpu.SemaphoreType.DMA((2,))],
        compiler_params=_cparams("arbitrary"),
        name="moe_combine",
    )(pos, pos, ys, x1, wcol, mods, ln_g, ln_b)


def _moe(h2, route, wcol, counts, wg, wu, wd, x1, mods, layer, ln_g, ln_b, latents_only):
    n = h2.shape[0]
    pos, tile_expert, n_used, zero_tiles, n_tiles_max = _moe_layout(route, counts, n)
    xs = _scatter(zero_tiles, pos, h2, n_tiles_max * MOE_TM)
    ys = _experts(tile_expert, n_used, xs, wg, wu, wd, layer, n_tiles_max)
    return _combine(pos, ys, x1, wcol, mods, layer, ln_g, ln_b, latents_only)


def _rope_tables(n_lat, head_dim):
    quarter = head_dim // 4
    n_rows = n_lat // GRID_W
    inv_freq = ROPE_THETA ** (-jnp.arange(quarter, dtype=F32) / quarter)
    ang_r = jnp.arange(n_rows)[:, None].astype(F32) * inv_freq
    ang_c = jnp.arange(GRID_W)[:, None].astype(F32) * inv_freq
    by_row = lambda a: jnp.repeat(a, GRID_W, axis=0)
    by_col = lambda a: jnp.tile(a, (n_rows, 1))
    cos_r, sin_r = by_row(jnp.cos(ang_r)), by_row(jnp.sin(ang_r))
    cos_c, sin_c = by_col(jnp.cos(ang_c)), by_col(jnp.sin(ang_c))
    cos = jnp.concatenate([cos_r, cos_r, cos_c, cos_c], axis=1)
    sin = jnp.concatenate([-sin_r, sin_r, -sin_c, sin_c], axis=1)
    reps = LANES // head_dim
    cos = jnp.tile(cos, (1, reps))
    sin = jnp.tile(sin, (1, reps))
    cos = jnp.concatenate([jnp.ones((CTX_LEN, LANES), F32), cos], axis=0)
    sin = jnp.concatenate([jnp.zeros((CTX_LEN, LANES), F32), sin], axis=0)
    return cos, sin


def _a_head_order():
    grp = A_HQ // A_HKV
    order = []
    for j in range(grp):
        for g in range(A_HKV):
            order.append(g * grp + j)
    return order


def kernel(x, c, ctx, c_ctx, w_mod, b_mod, ln_g, ln_b, w_in_even, w_out_even, sink_logits, lam_q1, lam_k1,
           lam_q2, lam_k2, subln_g, w_in_odd, w_out_odd, q_norm_g, k_norm_g, w_router, b_router, w_gate,
           w_up, w_down):
    n_lat = x.shape[1]
    assert x.shape == (1, n_lat, D_MODEL) and ctx.shape == (1, CTX_LEN, D_MODEL)
    assert n_lat % max(UNROLL * TK, MOE_TM) == 0 and n_lat % GRID_W == 0
    xs = jnp.concatenate([ctx[0], x[0]], axis=0)

    c8 = jnp.zeros((8, D_MODEL), F32).at[0].set(c_ctx).at[1].set(c[0])
    mods = _modulation(c8, w_mod, b_mod)

    cos64, sin64 = _rope_tables(n_lat, A_HD)
    cos128, sin128 = _rope_tables(n_lat, C_HD)

    order = _a_head_order()
    wq_a = A_HQ * A_HD
    head = lambda h: slice(h * A_HD, (h + 1) * A_HD)
    w_in_even_p = jnp.concatenate([w_in_even[:, :, head(h)] for h in order] + [w_in_even[:, :, wq_a:]],
                                  axis=2).astype(BF16)
    w_out_even_p = jnp.concatenate([w_out_even[:, head(h)] for h in order] + [w_out_even[:, wq_a:]],
                                   axis=1).astype(BF16)
    w_in_odd_b = w_in_odd.astype(BF16)
    w_out_odd_b = w_out_odd.astype(BF16)
    wg, wu, wd = w_gate, w_up, w_down
    wr = jnp.pad(w_router, ((0, 0), (0, LANES - N_EXPERTS)))
    br = jnp.pad(b_router.astype(F32), (0, LANES - N_EXPERTS)).reshape(LANES, 1)

    segs_even = ((A_HQ * A_HD, True, None, (A_HD ** -0.5) * LOG2E, None), (A_HKV * A_HD, True, None, None, None),
                 (A_HKV * A_HD, False, None, None, "ones"),
                 (B_H * 2 * B_HD, True, None, (B_HD ** -0.5) * LOG2E, None),
                 (B_H * 2 * B_HD, True, None, None, None), (B_H * B_VD, False, None, None, "transposed"))
    segs_odd = ((C_HQ * C_HD, True, 0, (C_HD ** -0.5) * LOG2E, None), (C_HKV * C_HD, True, 1, None, None),
                (C_HKV * C_HD, False, None, None, "transposed"))

    for l in range(DEPTH):
        i = l // 2
        lng = ln_g[l]
        lnb = ln_b[l]
        if l % 2 == 0:
            lam_init = 0.8 - 0.6 * math.exp(-0.3 * l)
            aq, ak, av, bq, bk, bv = _inproj(xs, mods, l, w_in_even_p[i], cos64, sin64, segs_even, A_HD // 4)
            sink_rows = jnp.broadcast_to(
                (sink_logits[i][jnp.array(order)] * LOG2E)[:, None], (A_HQ, LANES)).astype(F32)
            oa = _attn_a(aq, ak, av, sink_rows)
            ob = _attn_b(bq, bk, bv, lam_q1[i][None], lam_k1[i][None], lam_q2[i][None], lam_k2[i][None],
                         subln_g[i][None], lam_init)
            half = wq_a
            routed = _post(oa, ob, 0, 0, w_out_even_p[i][:half], w_out_even_p[i][half:], xs, mods, l,
                           lng[0][None], lnb[0][None], wr, br)
        else:
            q, k, v = _inproj(xs, mods, l, w_in_odd_b[i], cos128, sin128, segs_odd, C_HD // 4,
                              gains=(q_norm_g[i][None], k_norm_g[i][None]))
            o = _attn_c(q, k, v)
            half = C_HQ * C_HD // 2
            routed = _post(o, o, 0, 1, w_out_odd_b[i][:half], w_out_odd_b[i][half:], xs, mods, l,
                           lng[0][None], lnb[0][None], wr, br)
        x1, h2, route, wcol, counts = routed
        xs = _moe(h2, route, wcol, counts, wg, wu, wd, x1, mods, l, lng[1][None], lnb[1][None],
                  latents_only=(l == DEPTH - 1))
    return xs[None]
```
